```python
import numpy as np
import jax, jax.numpy as jnp
from jax import lax

D_MODEL = 1024
BATCH = 32
SEQ = 2048
DEPTH = 4

GRID_W = 64
CTX_LEN = 256
N_EVEN = (DEPTH + 1) // 2
N_ODD = DEPTH // 2

GLA_HEADS = 4
GLA_DK = D_MODEL // 16
GLA_DV = D_MODEL // 8
GLA_QK = GLA_HEADS * GLA_DK
GLA_V = GLA_HEADS * GLA_DV
GLA_RANK = 16
GLA_TAU = 16.0
GLA_CHUNK = 64
GLA_SIZES = (GLA_QK, GLA_QK, GLA_V, GLA_V, GLA_RANK, GLA_RANK)
GLA_IN = sum(GLA_SIZES)

RW_HEADS = 8
RW_DH = D_MODEL // 16
RW_W = RW_HEADS * RW_DH
RW_DECAY_RANK = 32
RW_A_RANK = 32
RW_G_RANK = 96
RW_GN_EPS = 64e-5
RW_SIZES = (RW_W, RW_W, RW_W, RW_DECAY_RANK, RW_DECAY_RANK, RW_A_RANK, RW_G_RANK)
RW_IN = sum(RW_SIZES)
EVEN_IN = GLA_IN + RW_IN
EVEN_MIX = GLA_V + RW_W

HEAD_DIM = 64
C_HEADS = 8
C_KV = 2
Q_BLOCK = 128
ROPE_THETA = 10000.0
NA_HEADS = 8
NA_KH = 8
NA_KW = 16
ODD_SIZES = (C_HEADS * HEAD_DIM, C_KV * HEAD_DIM, C_KV * HEAD_DIM,
             NA_HEADS * HEAD_DIM, NA_HEADS * HEAD_DIM, NA_HEADS * HEAD_DIM)
ODD_IN = sum(ODD_SIZES)
ODD_MIX = (C_HEADS + NA_HEADS) * HEAD_DIM

D_FF = -(-8 * D_MODEL // (3 * 256)) * 256

kernel_name = "hybrid_gla_rwkv7_gqa_natten_dit"

F32 = jnp.float32


def _split(z, sizes):
    return jnp.split(z, [int(i) for i in np.cumsum(sizes)[:-1]], axis=-1)


def rmsnorm(x, g, eps=1e-6):
    xf = x.astype(F32)
    y = xf * lax.rsqrt(jnp.mean(xf * xf, axis=-1, keepdims=True) + eps)
    return (y * g.astype(F32)).astype(x.dtype)


def swiglu(h, w13, w2):
    a, b = jnp.split(h @ w13, 2, axis=-1)
    return (jax.nn.silu(a) * b) @ w2


def token_shift(p, mu_prev, mu_next):
    prev = jnp.pad(p, ((0, 0), (1, 0), (0, 0)))[:, :-1]
    nxt = jnp.pad(p, ((0, 0), (0, 1), (0, 0)))[:, 1:]
    return p + mu_prev * (prev - p) + mu_next * (nxt - p)


def gla_chunked(q, k, v, log_a, s0):
    B, T, H, DK = q.shape
    DV = v.shape[-1]
    L = GLA_CHUNK
    n = T // L
    qf = q.astype(F32).reshape(B, n, L, H, DK)
    kf = k.astype(F32).reshape(B, n, L, H, DK)
    vf = v.astype(F32).reshape(B, n, L, H, DV)
    b = jnp.cumsum(log_a.astype(F32).reshape(B, n, L, H, DK), axis=2)
    total = b[:, :, -1]
    qb = qf * jnp.exp(b)
    kb = kf * jnp.exp(-b)
    lower = jnp.tril(jnp.ones((L, L), dtype=bool))
    att = jnp.where(lower, jnp.einsum('bnihd,bnjhd->bnhij', qb, kb), 0.0)
    o = jnp.einsum('bnhij,bnjhv->bnihv', att, vf)
    u = jnp.einsum('bnjhd,bnjhv->bnhdv', kf * jnp.exp(total[:, :, None] - b), vf)

    def step(s, inp):
        dec, uc = inp
        return dec[..., None] * s + uc, s

    s_fin, s_prev = lax.scan(step, s0, (jnp.exp(total).transpose(1, 0, 2, 3), u.transpose(1, 0, 2, 3, 4)))
    o = o + jnp.einsum('bnihd,nbhdv->bnihv', qb, s_prev)
    return o.reshape(B, T, H, DV).astype(v.dtype), s_fin


def rwkv7_scan(r, w, k, v, a, b, s0):
    xs = tuple(t.astype(F32).transpose(1, 0, 2, 3) for t in (r, w, k, v, a, b))

    def step(s, inp):
        rt, wt, kt, vt, at, bt = inp
        sa = jnp.einsum('bhvk,bhk->bhv', s, at)
        s = s * wt[:, :, None, :] + sa[..., None] * bt[:, :, None, :] + vt[..., None] * kt[:, :, None, :]
        return s, jnp.einsum('bhvk,bhk->bhv', s, rt)

    s_fin, o = lax.scan(step, s0, xs)
    return o.transpose(1, 0, 2, 3).astype(v.dtype), s_fin


def bidir(fn, fwd_in, bwd_in, s0_f, s0_b):
    o_f, s_f = fn(*fwd_in, s0_f)
    o_b, s_b = fn(*[jnp.flip(t, axis=1) for t in bwd_in], s0_b)
    return o_f + jnp.flip(o_b, axis=1), s_f, s_b


def gla_features(z, a_up, a_bias):
    B, T = z.shape[:2]
    q, k, v, g, ad_f, ad_b = _split(z, GLA_SIZES)
    la = [(jax.nn.log_sigmoid((ad @ a_up[d] + a_bias[d]).astype(F32)) / GLA_TAU).reshape(B, T, GLA_HEADS, GLA_DK)
          for d, ad in enumerate((ad_f, ad_b))]
    return (q.reshape(B, T, GLA_HEADS, GLA_DK) * GLA_DK ** -0.5, k.reshape(B, T, GLA_HEADS, GLA_DK),
            v.reshape(B, T, GLA_HEADS, GLA_DV), g, la[0], la[1])


def rwkv_features(z, mu, w0, w_up, a0, a_up, g_up, k_k, k_a):
    B, T = z.shape[:2]
    z = token_shift(z, mu[0], mu[1])
    r, k, v, wd_f, wd_b, ad, gd = _split(z, RW_SIZES)
    hd = lambda t: t.reshape(B, T, RW_HEADS, RW_DH)
    decays = []
    for d, wd in enumerate((wd_f, wd_b)):
        wl = -jax.nn.softplus(-(w0[d] + jnp.tanh(wd) @ w_up[d]).astype(F32)) - 0.5
        decays.append(hd(jnp.exp(-jnp.exp(wl))))
    a = jax.nn.sigmoid(a0 + ad @ a_up)
    g = jax.nn.sigmoid(gd) @ g_up
    kk = hd(k * k_k).astype(F32)
    kk = kk / jnp.maximum(jnp.sqrt(jnp.sum(kk * kk, axis=-1, keepdims=True)), 1e-12)
    k = k * (1 + (a - 1) * k_a)
    return hd(r), decays[0], decays[1], hd(k), hd(v), -kk, kk * hd(a).astype(F32), g


def even_mixer(h, hc, w_in, w_out, a_up, a_bias, gla_g, mu, w0, w_up, a0, aa_up, g_up,
               k_k, k_a, r_k, ln_g, ln_b, need_ctx):
    def features(hh):
        z = hh @ w_in
        return (gla_features(z[..., :GLA_IN], a_up, a_bias),
                rwkv_features(z[..., GLA_IN:], mu, w0, w_up, a0, aa_up, g_up, k_k, k_a))

    gla_c, rw_c = features(hc)
    gla_x, rw_x = features(h)
    B = h.shape[0]
    s0_gla = jnp.zeros((B, GLA_HEADS, GLA_DK, GLA_DV), F32)
    s0_rw = jnp.zeros((B, RW_HEADS, RW_DH, RW_DH), F32)

    def run_gla(f, s_f, s_b):
        q, k, v, _, la_f, la_b = f
        return bidir(gla_chunked, (q, k, v, la_f), (q, k, v, la_b), s_f, s_b)

    def run_rw(f, s_f, s_b):
        r, wf, wb, k, v, av, bv, _ = f
        return bidir(rwkv7_scan, (r, wf, k, v, av, bv), (r, wb, k, v, av, bv), s_f, s_b)

    og_c, sg_f, sg_b = run_gla(gla_c, s0_gla, s0_gla)
    or_c, sr_f, sr_b = run_rw(rw_c, s0_rw, s0_rw)
    og_x, _, _ = run_gla(gla_x, sg_f, sg_b)
    or_x, _, _ = run_rw(rw_x, sr_f, sr_b)

    def merge(og, gf, orw, rf):
        Bm, T = og.shape[:2]
        r, _, _, k, v, _, _, g_rw = rf
        y_gla = rmsnorm(og, gla_g) * jax.nn.silu(gf[3]).reshape(Bm, T, GLA_HEADS, GLA_DV)
        of = orw.astype(F32)
        mean = jnp.mean(of, axis=-1, keepdims=True)
        var = jnp.mean(jnp.square(of - mean), axis=-1, keepdims=True)
        y_rw = ((of - mean) * lax.rsqrt(var + RW_GN_EPS)).astype(orw.dtype).reshape(Bm, T, RW_W) * ln_g + ln_b
        bonus = jnp.sum(r * k * r_k, axis=-1, keepdims=True) * v
        y_rw = (y_rw + bonus.reshape(Bm, T, RW_W)) * g_rw
        return jnp.concatenate([y_gla.reshape(Bm, T, GLA_V), y_rw], axis=-1) @ w_out

    y = merge(og_x, gla_x, or_x, rw_x)
    yc = merge(og_c, gla_c, or_c, rw_c) if need_ctx else None
    return y, yc


def axial_rope_tables(T):
    t = jnp.arange(T)
    pos = jnp.stack([t // GRID_W, t % GRID_W], axis=-1).astype(F32)
    half = HEAD_DIM // 2
    inv = ROPE_THETA ** (-jnp.arange(0, half, 2, dtype=F32) / half)
    ang = pos[:, :, None] * inv
    return jnp.cos(ang), jnp.sin(ang)


def apply_rope(x, cos, sin):
    B, T, H, dh = x.shape
    xf = x.astype(F32).reshape(B, T, H, 2, 2, dh // 4)
    x1, x2 = xf[..., 0, :], xf[..., 1, :]
    c, s = cos[None, :, None], sin[None, :, None]
    out = jnp.stack([x1 * c - x2 * s, x1 * s + x2 * c], axis=-2)
    return out.reshape(B, T, H, dh).astype(x.dtype)


def blocked_attention(q, k, v):
    B, T, Hq, dh = q.shape
    Hkv = k.shape[2]
    G = Hq // Hkv
    nb = T // Q_BLOCK
    qb = (q * dh ** -0.5).reshape(B, nb, Q_BLOCK, Hkv, G, dh).transpose(1, 0, 2, 3, 4, 5)

    def blk(qi):
        s = jnp.einsum('bqkgd,bskd->bkgqs', qi, k).astype(F32)
        p = jax.nn.softmax(s, axis=-1).astype(v.dtype)
        return jnp.einsum('bkgqs,bskd->bqkgd', p, v)

    o = lax.map(blk, qb)
    return o.transpose(1, 0, 2, 3, 4, 5).reshape(B, T, Hq * dh)


def neighbourhood_attention(q, k, v, k_ctx, v_ctx, rpb):
    B, T, H, dh = q.shape
    rows = T // GRID_W
    kh = min(NA_KH, rows)
    kw = min(NA_KW, GRID_W)
    qr = (q * dh ** -0.5).reshape(B, rows, GRID_W, H, dh).transpose(1, 0, 2, 3, 4)
    kg = k.reshape(B, rows, GRID_W, H, dh)
    vg = v.reshape(B, rows, GRID_W, H, dh)
    col = jnp.arange(GRID_W)
    start = jnp.clip(col - kw // 2, 0, GRID_W - kw)
    in_win = (col[None, :] >= start[:, None]) & (col[None, :] < start[:, None] + kw)
    dc = jnp.clip(col[None, :] - col[:, None], -(NA_KW - 1), NA_KW - 1) + NA_KW - 1
    bias_cols = rpb[:, :, dc]

    def row(args):
        r, q_r = args
        rs = jnp.clip(r - kh // 2, 0, rows - kh)
        k_r = lax.dynamic_slice_in_dim(kg, rs, kh, axis=1)
        v_r = lax.dynamic_slice_in_dim(vg, rs, kh, axis=1)
        bias = bias_cols[:, rs + jnp.arange(kh) - r + NA_KH - 1]
        s_nb = jnp.einsum('bchd,bkwhd->bhckw', q_r, k_r).astype(F32) + bias.transpose(0, 2, 1, 3)[None].astype(F32)
        s_nb = jnp.where(in_win[:, None, :], s_nb, -jnp.inf)
        s_cx = jnp.einsum('bchd,blhd->bhcl', q_r, k_ctx).astype(F32)
        p = jax.nn.softmax(jnp.concatenate([s_nb.reshape(B, H, GRID_W, kh * GRID_W), s_cx], axis=-1), axis=-1)
        p = p.astype(v.dtype)
        p_nb = p[..., :kh * GRID_W].reshape(B, H, GRID_W, kh, GRID_W)
        return (jnp.einsum('bhckw,bkwhd->bchd', p_nb, v_r)
                + jnp.einsum('bhcl,blhd->bchd', p[..., kh * GRID_W:], v_ctx))

    o = lax.map(row, (jnp.arange(rows), qr))
    return o.transpose(1, 0, 2, 3, 4).reshape(B, T, H * dh)


def odd_mixer(h, hc, w_in, w_out, q_g, k_g, rpb, cos, sin, need_ctx):
    def features(hh):
        B, T = hh.shape[:2]
        qc, kc, vc, qd, kd, vd = _split(hh @ w_in, ODD_SIZES)
        return (rmsnorm(qc.reshape(B, T, C_HEADS, HEAD_DIM), q_g),
                rmsnorm(kc.reshape(B, T, C_KV, HEAD_DIM), k_g),
                vc.reshape(B, T, C_KV, HEAD_DIM),
                qd.reshape(B, T, NA_HEADS, HEAD_DIM),
                kd.reshape(B, T, NA_HEADS, HEAD_DIM),
                vd.reshape(B, T, NA_HEADS, HEAD_DIM))

    qc_c, kc_c, vc_c, qd_c, kd_c, vd_c = features(hc)
    qc, kc, vc, qd, kd, vd = features(h)
    qc, kc = apply_rope(qc, cos, sin), apply_rope(kc, cos, sin)
    y_gqa = blocked_attention(qc, jnp.concatenate([kc_c, kc], axis=1), jnp.concatenate([vc_c, vc], axis=1))
    y_na = neighbourhood_attention(qd, kd, vd, kd_c, vd_c, rpb)
    y = jnp.concatenate([y_gqa, y_na], axis=-1) @ w_out
    yc = None
    if need_ctx:
        yc = jnp.concatenate([blocked_attention(qc_c, kc_c, vc_c), blocked_attention(qd_c, kd_c, vd_c)], axis=-1) @ w_out
    return y, yc


def setup_inputs(seed: int = 0) -> dict:
    key = jax.random.key(seed)
    ks = iter(jax.random.split(key, 40))
    nrm = lambda shape, s: jax.random.normal(next(ks), shape, F32) * s
    uni = lambda shape, lo, hi: jax.random.uniform(next(ks), shape, F32, lo, hi)
    D = D_MODEL
    return {
        "x": nrm((BATCH, SEQ, D), 1.0),
        "c": nrm((BATCH, D), 1.0),
        "ctx": nrm((BATCH, CTX_LEN, D), 1.0),
        "c_ctx": nrm((D,), 1.0),
        "w_mod": nrm((DEPTH, D, 6 * D), 0.02),
        "b_mod": nrm((DEPTH, 6 * D), 0.02),
        "norm1_g": 1.0 + nrm((DEPTH, D), 0.02),
        "norm2_g": 1.0 + nrm((DEPTH, D), 0.02),
        "ffn_w13": nrm((DEPTH, D, 2 * D_FF), D ** -0.5),
        "ffn_w2": nrm((DEPTH, D_FF, D), D_FF ** -0.5),
        "ev_w_in": nrm((N_EVEN, D, EVEN_IN), D ** -0.5),
        "ev_w_out": nrm((N_EVEN, EVEN_MIX, D), EVEN_MIX ** -0.5),
        "gla_a_up": nrm((N_EVEN, 2, GLA_RANK, GLA_QK), GLA_RANK ** -0.5),
        "gla_a_bias": uni((N_EVEN, 2, GLA_QK), 1.0, 4.0),
        "gla_norm_g": 1.0 + nrm((N_EVEN, GLA_DV), 0.02),
        "rw_mu": uni((N_EVEN, 2, RW_IN), 0.0, 0.5),
        "rw_w0": uni((N_EVEN, 2, RW_W), -6.0, -1.0),
        "rw_w_up": nrm((N_EVEN, 2, RW_DECAY_RANK, RW_W), 0.5 * RW_DECAY_RANK ** -0.5),
        "rw_a0": nrm((N_EVEN, RW_W), 0.1),
        "rw_a_up": nrm((N_EVEN, RW_A_RANK, RW_W), 0.5 * RW_A_RANK ** -0.5),
        "rw_g_up": nrm((N_EVEN, RW_G_RANK, RW_W), RW_G_RANK ** -0.5),
        "rw_k_k": 0.85 + nrm((N_EVEN, RW_W), 0.02),
        "rw_k_a": 1.0 + nrm((N_EVEN, RW_W), 0.02),
        "rw_r_k": nrm((N_EVEN, RW_HEADS, RW_DH), 0.1),
        "rw_ln_g": 1.0 + nrm((N_EVEN, RW_W), 0.02),
        "rw_ln_b": nrm((N_EVEN, RW_W), 0.02),
        "od_w_in": nrm((N_ODD, D, ODD_IN), D ** -0.5),
        "od_w_out": nrm((N_ODD, ODD_MIX, D), ODD_MIX ** -0.5),
        "cq_norm_g": 1.0 + nrm((N_ODD, HEAD_DIM), 0.02),
        "ck_norm_g": 1.0 + nrm((N_ODD, HEAD_DIM), 0.02),
        "na_rpb": nrm((N_ODD, NA_HEADS, 2 * NA_KH - 1, 2 * NA_KW - 1), 0.1),
        "final_g": 1.0 + nrm((D,), 0.02),
    }


def reference(x, c, ctx, c_ctx, w_mod, b_mod, norm1_g, norm2_g, ffn_w13, ffn_w2,
              ev_w_in, ev_w_out, gla_a_up, gla_a_bias, gla_norm_g, rw_mu, rw_w0, rw_w_up,
              rw_a0, rw_a_up, rw_g_up, rw_k_k, rw_k_a, rw_r_k, rw_ln_g, rw_ln_b,
              od_w_in, od_w_out, cq_norm_g, ck_norm_g, na_rpb, final_g):
    cos, sin = axial_rope_tables(x.shape[1])
    s_lat = jax.nn.silu(c)
    s_ctx = jax.nn.silu(c_ctx)
    for i in range(DEPTH):
        need_ctx = i < DEPTH - 1
        sh1, sc1, g1, sh2, sc2, g2 = [m[:, None, :] for m in _split(s_lat @ w_mod[i] + b_mod[i], [D_MODEL] * 6)]
        sh1c, sc1c, g1c, sh2c, sc2c, g2c = _split(s_ctx @ w_mod[i] + b_mod[i], [D_MODEL] * 6)
        h = rmsnorm(x, norm1_g[i]) * (1 + sc1) + sh1
        hc = rmsnorm(ctx, norm1_g[i]) * (1 + sc1c) + sh1c
        j = i // 2
        if i % 2 == 0:
            y, yc = even_mixer(h, hc, ev_w_in[j], ev_w_out[j], gla_a_up[j], gla_a_bias[j], gla_norm_g[j],
                               rw_mu[j], rw_w0[j], rw_w_up[j], rw_a0[j], rw_a_up[j], rw_g_up[j],
                               rw_k_k[j], rw_k_a[j], rw_r_k[j], rw_ln_g[j], rw_ln_b[j], need_ctx)
        else:
            y, yc = odd_mixer(h, hc, od_w_in[j], od_w_out[j], cq_norm_g[j], ck_norm_g[j], na_rpb[j],
                              cos, sin, need_ctx)
        x = x + g1 * y
        h = rmsnorm(x, norm2_g[i]) * (1 + sc2) + sh2
        x = x + g2 * swiglu(h, ffn_w13[i], ffn_w2[i])
        if need_ctx:
            ctx = ctx + g1c * yc
            hc = rmsnorm(ctx, norm2_g[i]) * (1 + sc2c) + sh2c
            ctx = ctx + g2c * swiglu(hc, ffn_w13[i], ffn_w2[i])
    return rmsnorm(x, final_g)
```

```python
import functools

import jax
import jax.numpy as jnp
import numpy as np
from jax import lax
from jax.experimental import pallas as pl
from jax.experimental.pallas import tpu as pltpu

F32 = jnp.float32
BF16 = jnp.bfloat16

DEPTH = 4
GRID_W = 64
CHUNK = 64
LANE = 128

GLA_HEADS, GLA_DK, GLA_DV, GLA_RANK, GLA_TAU = 4, 64, 128, 16, 16.0
GLA_QK, GLA_V = GLA_HEADS * GLA_DK, GLA_HEADS * GLA_DV
GLA_Z = 2 * GLA_QK + 2 * GLA_V + 2 * LANE

RW_HEADS, RW_DH, RW_W = 8, 64, 512
RW_DECAY_RANK, RW_A_RANK, RW_G_RANK = 32, 32, 96
RW_GN_EPS = 64e-5
RW_Z = 3 * RW_W + 4 * LANE

HEAD_DIM, C_HEADS, C_KV, NA_HEADS, NA_KH, NA_KW = 64, 8, 2, 8, 8, 16
ROPE_THETA = 10000.0
NEG = -1e30

VMEM_LIMIT = 56 * 1024 * 1024


def _cp(*sem):
    return pltpu.CompilerParams(dimension_semantics=sem, vmem_limit_bytes=VMEM_LIMIT)


def _dot(a, b):
    return jnp.dot(a.astype(BF16), b.astype(BF16), preferred_element_type=F32)


def _dot_nt(a, b):
    return lax.dot_general(a.astype(BF16), b.astype(BF16), (((1,), (1,)), ((), ())), preferred_element_type=F32)


def _dot_tn(a, b):
    return lax.dot_general(a.astype(BF16), b.astype(BF16), (((0,), (0,)), ((), ())), preferred_element_type=F32)


def _split_dot(m, x, parts):
    acc = None
    rem = x
    for _ in range(parts):
        piece = rem.astype(BF16)
        rem = rem - piece.astype(F32)
        t = jnp.dot(m, piece, preferred_element_type=F32)
        acc = t if acc is None else acc + t
    return acc


def _split_dot_r(x, m, parts):
    acc = None
    rem = x
    for _ in range(parts):
        piece = rem.astype(BF16)
        rem = rem - piece.astype(F32)
        t = jnp.dot(piece, m, preferred_element_type=F32)
        acc = t if acc is None else acc + t
    return acc


def _softplus(y):
    return jnp.maximum(y, 0.0) + jnp.log1p(jnp.exp(-jnp.abs(y)))


def _rms_mod(x, g, sc_l, sh_l, sc_c, sh_c, row0, n_ctx):
    y = x * lax.rsqrt(jnp.mean(x * x, axis=-1, keepdims=True) + 1e-6) * g
    rows = row0 + lax.broadcasted_iota(jnp.int32, (x.shape[0], 1), 0)
    is_ctx = rows < n_ctx
    sc = jnp.where(is_ctx, sc_c, sc_l)
    sh = jnp.where(is_ctx, sh_c, sh_l)
    return y * (1.0 + sc) + sh


def _tri_masks(n, reverse):
    ri = lax.broadcasted_iota(jnp.int32, (n, n), 0)
    ci = lax.broadcasted_iota(jnp.int32, (n, n), 1)
    if reverse:
        return ci >= ri, ci > ri
    return ci <= ri, ci < ri


INV_BASE = 16


def _block_ids(n):
    ri = lax.broadcasted_iota(jnp.int32, (n, n), 0)
    ci = lax.broadcasted_iota(jnp.int32, (n, n), 1)
    return ri, ci


def _unit_tri_inverse(a, blk):
    ri, ci = blk
    n = a.shape[0]
    same = lambda bs: (ri // bs) == (ci // bs)
    ad = jnp.where(same(INV_BASE), a, 0.0)
    eye = (ri == ci).astype(F32)
    t = eye + ad
    p = ad
    m = 1
    while 2 * m < INV_BASE:
        p = _dot(p, p)
        t = t + _dot(t, p)
        m *= 2
    bs = INV_BASE
    while bs < n:
        off = jnp.where(same(2 * bs) & jnp.logical_not(same(bs)), a, 0.0)
        t = t + _dot(_dot(t, off), t)
        bs *= 2
    return t


def _mod_kernel(s_ref, w_ref, b_ref, o_ref):
    s = s_ref[...]
    s = s * jax.nn.sigmoid(s)
    o_ref[0] = _dot(s, w_ref[0]) + b_ref[0]


def _mod_vectors(c, c_ctx, w_mod, b_mod):
    B, D = c.shape
    bp = -(-(B + 1) // 8) * 8
    s_in = jnp.zeros((bp, D), F32).at[:B].set(c).at[B].set(c_ctx)
    n6 = w_mod.shape[-1]
    tn = 1536
    out = pl.pallas_call(
        _mod_kernel,
        grid=(DEPTH, n6 // tn),
        in_specs=[pl.BlockSpec((bp, D), lambda l, n: (0, 0)),
                  pl.BlockSpec((1, D, tn), lambda l, n: (l, 0, n)),
                  pl.BlockSpec((1, 1, tn), lambda l, n: (l, 0, n))],
        out_specs=pl.BlockSpec((1, bp, tn), lambda l, n: (l, 0, n)),
        out_shape=jax.ShapeDtypeStruct((DEPTH, bp, n6), F32),
        compiler_params=_cp("parallel", "parallel"),
        name="mod_vectors",
    )(s_in, w_mod, b_mod.reshape(DEPTH, 1, n6))
    return out.reshape(DEPTH, bp, 1, n6)


def _mod_specs(B, D, idx, grid_rank):
    if grid_rank == 2:
        return [pl.BlockSpec((1, 1, D), lambda b, j: (b, 0, idx)),
                pl.BlockSpec((1, 1, D), lambda b, j: (B, 0, idx))]
    return [pl.BlockSpec((1, 1, D), lambda b, j, n: (b, 0, idx)),
            pl.BlockSpec((1, 1, D), lambda b, j, n: (B, 0, idx))]


def _nmm_kernel(x_ref, g_ref, shl, shc, scl, scc, w_ref, o_ref, h_scr, *, tm, n_ctx):
    @pl.when(pl.program_id(2) == 0)
    def _():
        h = _rms_mod(x_ref[0], g_ref[...], scl[0], shl[0], scc[0], shc[0], pl.program_id(1) * tm, n_ctx)
        h_scr[...] = h.astype(BF16)

    o_ref[0] = jnp.dot(h_scr[...], w_ref[...], preferred_element_type=F32).astype(o_ref.dtype)


def _norm_mod_matmul(xs, g, mod, w, n_ctx, *, tm, tn, name):
    B, S, D = xs.shape
    N = w.shape[1]
    nB = B
    return pl.pallas_call(
        functools.partial(_nmm_kernel, tm=tm, n_ctx=n_ctx),
        grid=(B, S // tm, N // tn),
        in_specs=[pl.BlockSpec((1, tm, D), lambda b, j, n: (b, j, 0)),
                  pl.BlockSpec((1, D), lambda b, j, n: (0, 0)),
                  *_mod_specs(nB, D, 0, 3), *_mod_specs(nB, D, 1, 3),
                  pl.BlockSpec((D, tn), lambda b, j, n: (0, n))],
        out_specs=pl.BlockSpec((1, tm, tn), lambda b, j, n: (b, j, n)),
        out_shape=jax.ShapeDtypeStruct((B, S, N), F32),
        scratch_shapes=[pltpu.VMEM((tm, D), BF16)],
        compiler_params=_cp("parallel", "parallel", "arbitrary"),
        name=name,
    )(xs, g.reshape(1, D), mod, mod, mod, mod, w)


def _ffn_kernel(x_ref, g_ref, shl, shc, scl, scc, gl, gc, w1_ref, w3_ref, w2_ref, o_ref, h_scr, acc_scr,
                *, tm, n_ctx):
    f = pl.program_id(2)

    @pl.when(f == 0)
    def _():
        h = _rms_mod(x_ref[0], g_ref[...], scl[0], shl[0], scc[0], shc[0], pl.program_id(1) * tm, n_ctx)
        h_scr[...] = h.astype(BF16)
        acc_scr[...] = jnp.zeros_like(acc_scr)

    h = h_scr[...]
    a = jnp.dot(h, w1_ref[...], preferred_element_type=F32)
    b = jnp.dot(h, w3_ref[...], preferred_element_type=F32)
    u = (a * jax.nn.sigmoid(a) * b).astype(BF16)
    acc_scr[...] += jnp.dot(u, w2_ref[...], preferred_element_type=F32)

    @pl.when(f == pl.num_programs(2) - 1)
    def _():
        rows = pl.program_id(1) * tm + lax.broadcasted_iota(jnp.int32, (tm, 1), 0)
        gate = jnp.where(rows < n_ctx, gc[0], gl[0])
        o_ref[0] = x_ref[0] + gate * acc_scr[...]


def _ffn(xs, g, mod, w13, w2, n_ctx, *, tm, tf):
    B, S, D = xs.shape
    Fd = w2.shape[0]
    nf = Fd // tf
    return pl.pallas_call(
        functools.partial(_ffn_kernel, tm=tm, n_ctx=n_ctx),
        grid=(B, S // tm, nf),
        in_specs=[pl.BlockSpec((1, tm, D), lambda b, j, f: (b, j, 0)),
                  pl.BlockSpec((1, D), lambda b, j, f: (0, 0)),
                  *_mod_specs(B, D, 3, 3), *_mod_specs(B, D, 4, 3), *_mod_specs(B, D, 5, 3),
                  pl.BlockSpec((D, tf), lambda b, j, f: (0, f)),
                  pl.BlockSpec((D, tf), lambda b, j, f: (0, nf + f)),
                  pl.BlockSpec((tf, D), lambda b, j, f: (f, 0))],
        out_specs=pl.BlockSpec((1, tm, D), lambda b, j, f: (b, j, 0)),
        out_shape=jax.ShapeDtypeStruct((B, S, D), F32),
        scratch_shapes=[pltpu.VMEM((tm, D), BF16), pltpu.VMEM((tm, D), F32)],
        compiler_params=_cp("parallel", "parallel", "arbitrary"),
        name="ffn",
    )(xs, g.reshape(1, D), mod, mod, mod, mod, mod, mod, w13, w13, w2)


def _chunk_index(s, nc_ctx, nc, reverse):
    if not reverse:
        return s
    return jnp.where(s < nc_ctx, nc_ctx - 1 - s, nc + nc_ctx - 1 - s)


def _gla_kernel(qkv_ref, ad_ref, aup_ref, bias_ref, o_ref, st_scr, *, reverse):
    @pl.when(pl.program_id(1) == 0)
    def _():
        st_scr[...] = jnp.zeros_like(st_scr)

    d = 1 if reverse else 0
    L = CHUNK
    incl, _ = _tri_masks(L, reverse)
    tri = incl.astype(BF16)
    ad = ad_ref[0][:, d * LANE:(d + 1) * LANE]
    y = _dot(ad, aup_ref[d]) + bias_ref[d:d + 1, :]
    la = (jnp.minimum(y, 0.0) - jnp.log1p(jnp.exp(-jnp.abs(y)))) * (1.0 / GLA_TAU)
    cs = _split_dot(tri, la, 3)
    tot = cs[0:1] if reverse else cs[L - 1:L]
    qkv = qkv_ref[0]
    q = qkv[:, 0:GLA_QK] * GLA_DK ** -0.5
    k = qkv[:, GLA_QK:2 * GLA_QK]
    v = qkv[:, 2 * GLA_QK:2 * GLA_QK + GLA_V].astype(BF16)
    qb = (q * jnp.exp(cs)).astype(BF16)
    kb = (k * jnp.exp(-cs)).astype(BF16)
    kt = (k * jnp.exp(tot - cs)).astype(BF16)
    gt = jnp.exp(tot)
    for h in range(GLA_HEADS):
        sl = slice(h * GLA_DK, (h + 1) * GLA_DK)
        vs = slice(h * GLA_DV, (h + 1) * GLA_DV)
        att = jnp.where(incl, _dot_nt(qb[:, sl], kb[:, sl]), 0.0)
        st = st_scr[h]
        o_ref[0, :, vs] = _dot(att, v[:, vs]) + _dot_nt(qb[:, sl], st)
        st_scr[h] = st * gt[:, sl] + _dot_tn(v[:, vs], kt[:, sl])


def _gla_scan(z_gla, aup_pad, bias, n_ctx, reverse):
    B, S, _ = z_gla.shape
    nc, nc_ctx = S // CHUNK, n_ctx // CHUNK
    cidx = functools.partial(_chunk_index, nc_ctx=nc_ctx, nc=nc, reverse=reverse)
    return pl.pallas_call(
        functools.partial(_gla_kernel, reverse=reverse),
        grid=(B, nc),
        in_specs=[pl.BlockSpec((1, CHUNK, 2 * GLA_QK + GLA_V), lambda b, s: (b, cidx(s), 0)),
                  pl.BlockSpec((1, CHUNK, 2 * LANE), lambda b, s: (b, cidx(s), (2 * GLA_QK + 2 * GLA_V) // (2 * LANE))),
                  pl.BlockSpec((2, LANE, GLA_QK), lambda b, s: (0, 0, 0)),
                  pl.BlockSpec((2, GLA_QK), lambda b, s: (0, 0))],
        out_specs=pl.BlockSpec((1, CHUNK, GLA_V), lambda b, s: (b, cidx(s), 0)),
        out_shape=jax.ShapeDtypeStruct((B, S, GLA_V), F32),
        scratch_shapes=[pltpu.VMEM((GLA_HEADS, GLA_DV, GLA_DK), F32)],
        compiler_params=_cp("parallel", "arbitrary"),
        name="gla_bwd" if reverse else "gla_fwd",
    )(z_gla, z_gla, aup_pad, bias)


def _rw_feat_kernel(z_ref, zp_ref, zn_ref, mu_ref, w0_ref, wup_ref, a0_ref, aup_ref, gup_ref, kk_ref, ka_ref,
                    rk_ref, ones_ref, r_o, lwf_o, lwb_o, k_o, v_o, kk_o, a_o, g_o, bonus_o, *, tt):
    j = pl.program_id(1)
    last = pl.num_programs(1) - 1
    z = z_ref[0]
    prev_row = jnp.where(j <= 1, 0.0, zp_ref[0, 7:8, :])
    next_row = jnp.where((j == 0) | (j == last), 0.0, zn_ref[0, 0:1, :])
    rid = lax.broadcasted_iota(jnp.int32, (tt, 1), 0)
    prev = jnp.where(rid == 0, prev_row, pltpu.roll(z, 1, 0))
    nxt = jnp.where(rid == tt - 1, next_row, pltpu.roll(z, tt - 1, 0))
    zs = z + mu_ref[0:1, :] * (prev - z) + mu_ref[1:2, :] * (nxt - z)

    W = RW_W
    r, k, v = zs[:, 0:W], zs[:, W:2 * W], zs[:, 2 * W:3 * W]
    base = 3 * W
    wd = (zs[:, base:base + LANE], zs[:, base + LANE:base + 2 * LANE])
    ad = zs[:, base + 2 * LANE:base + 3 * LANE]
    gd = zs[:, base + 3 * LANE:base + 4 * LANE]
    for d, lw_o in enumerate((lwf_o, lwb_o)):
        wl = -_softplus(-(w0_ref[d:d + 1, :] + _dot(jnp.tanh(wd[d]), wup_ref[d]))) - 0.5
        lw_o[0] = -jnp.exp(wl)
    a = jax.nn.sigmoid(a0_ref[...] + _dot(ad, aup_ref[...]))
    g = _dot(jax.nn.sigmoid(gd), gup_ref[...])
    ones = ones_ref[...]
    kk = k * kk_ref[...]
    ss = _split_dot_r(kk * kk, ones, 2)
    kk = kk / jnp.maximum(jnp.sqrt(ss), 1e-12)
    kmod = k * (1.0 + (a - 1.0) * ka_ref[...])
    bonus = _split_dot_r(r * kmod * rk_ref[...], ones, 2) * v
    r_o[0] = r
    k_o[0] = kmod
    v_o[0] = v
    kk_o[0] = kk
    a_o[0] = a
    g_o[0] = g
    bonus_o[0] = bonus


def _rw_features(z_rw, p, n_ctx):
    B, S, Z = z_rw.shape
    tt = n_ctx
    nt = S // tt
    n8 = S // 8
    W = RW_W
    full = lambda shape: pl.BlockSpec(shape, lambda b, j: (0,) * len(shape))
    out_spec = pl.BlockSpec((1, tt, W), lambda b, j: (b, j, 0))
    return pl.pallas_call(
        functools.partial(_rw_feat_kernel, tt=tt),
        grid=(B, nt),
        in_specs=[pl.BlockSpec((1, tt, Z), lambda b, j: (b, j, 0)),
                  pl.BlockSpec((1, 8, Z), lambda b, j: (b, jnp.maximum(j * (tt // 8) - 1, 0), 0)),
                  pl.BlockSpec((1, 8, Z), lambda b, j: (b, jnp.minimum((j + 1) * (tt // 8), n8 - 1), 0)),
                  full((2, Z)), full((2, W)), full((2, LANE, W)), full((1, W)), full((LANE, W)), full((LANE, W)),
                  full((1, W)), full((1, W)), full((1, W)), full((W, W))],
        out_specs=[out_spec] * 9,
        out_shape=[jax.ShapeDtypeStruct((B, S, W), F32)] * 9,
        compiler_params=_cp("parallel", "parallel"),
        name="rwkv_features",
    )(z_rw, z_rw, z_rw, p["mu"], p["w0"], p["wup"], p["a0"], p["aup"], p["gup"], p["k_k"], p["k_a"], p["r_k"],
      p["ones64"])


def _rw_scan_kernel(r_ref, lw_ref, k_ref, v_ref, kk_ref, a_ref, o_ref, s_scr, *, reverse):
    @pl.when(pl.program_id(1) == 0)
    def _():
        s_scr[...] = jnp.zeros_like(s_scr)

    L = CHUNK
    incl, strict = _tri_masks(L, reverse)
    blk = _block_ids(L)
    tri = incl.astype(BF16)
    lw = lw_ref[0]
    cs = _split_dot(tri, lw, 3)
    tot = cs[0:1] if reverse else cs[L - 1:L]
    kk, a, r, k = kk_ref[0], a_ref[0], r_ref[0], k_ref[0]
    vb = v_ref[0].astype(BF16)
    bv = kk * a
    g_inv = jnp.exp(-cs)
    g_rem = jnp.exp(tot - cs)
    g_tot = jnp.exp(tot)
    at = (-(jnp.exp(cs - lw) * kk)).astype(BF16)
    rt = (jnp.exp(cs) * r).astype(BF16)
    bh = (g_inv * bv).astype(BF16)
    kh = (g_inv * k).astype(BF16)
    bl = (g_rem * bv).astype(BF16)
    kl = (g_rem * k).astype(BF16)
    for h in range(RW_HEADS):
        sl = slice(h * RW_DH, (h + 1) * RW_DH)
        lhs = jnp.concatenate([at[:, sl], rt[:, sl]], axis=0)
        rhs = jnp.concatenate([bh[:, sl], kh[:, sl]], axis=0)
        sc = _dot_nt(lhs, rhs)
        mab = jnp.where(strict, sc[:L, :L], 0.0)
        mak = jnp.where(strict, sc[:L, L:], 0.0)
        nrb = jnp.where(incl, sc[L:, :L], 0.0)
        nrk = jnp.where(incl, sc[L:, L:], 0.0)
        s0 = s_scr[h]
        zs = _dot_nt(lhs, s0)
        vh = vb[:, sl]
        u = _dot(_unit_tri_inverse(mab, blk), zs[:L] + _dot(mak, vh))
        o_ref[0, :, sl] = zs[L:] + _dot(nrb, u) + _dot(nrk, vh)
        x = jnp.concatenate([u.astype(BF16), vh], axis=0)
        y = jnp.concatenate([bl[:, sl], kl[:, sl]], axis=0)
        s_scr[h] = s0 * g_tot[:, sl] + _dot_tn(x, y)


def _rw_scan(r, lw, k, v, kk, a, n_ctx, reverse):
    B, S, W = r.shape
    nc, nc_ctx = S // CHUNK, n_ctx // CHUNK
    cidx = functools.partial(_chunk_index, nc_ctx=nc_ctx, nc=nc, reverse=reverse)
    spec = pl.BlockSpec((1, CHUNK, W), lambda b, s: (b, cidx(s), 0))
    return pl.pallas_call(
        functools.partial(_rw_scan_kernel, reverse=reverse),
        grid=(B, nc),
        in_specs=[spec] * 6,
        out_specs=spec,
        out_shape=jax.ShapeDtypeStruct((B, S, W), F32),
        scratch_shapes=[pltpu.VMEM((RW_HEADS, RW_DH, RW_DH), F32)],
        compiler_params=_cp("parallel", "arbitrary"),
        name="rwkv_bwd" if reverse else "rwkv_fwd",
    )(r, lw, k, v, kk, a)


def _even_out_kernel(x_ref, ogf, ogb, gate_ref, orf, orb, bonus_ref, grw_ref, glag, lng, lnb, ones128, ones64,
                     w_ref, gl, gc, o_ref, *, tm, n_ctx):
    og = ogf[0] + ogb[0]
    ms = _split_dot_r(og * og, ones128[...], 2) * (1.0 / GLA_DV)
    gt = gate_ref[0]
    y_gla = og * lax.rsqrt(ms + 1e-6) * glag[...] * (gt * jax.nn.sigmoid(gt))
    of = orf[0] + orb[0]
    o64 = ones64[...]
    mean = _split_dot_r(of, o64, 3) * (1.0 / RW_DH)
    cen = of - mean
    var = _split_dot_r(cen * cen, o64, 2) * (1.0 / RW_DH)
    y_rw = (cen * lax.rsqrt(var + RW_GN_EPS) * lng[...] + lnb[...] + bonus_ref[0]) * grw_ref[0]
    y = jnp.concatenate([y_gla, y_rw], axis=1).astype(BF16)
    rows = pl.program_id(1) * tm + lax.broadcasted_iota(jnp.int32, (tm, 1), 0)
    gate = jnp.where(rows < n_ctx, gc[0], gl[0])
    o_ref[0] = x_ref[0] + gate * jnp.dot(y, w_ref[...], preferred_element_type=F32)


def _even_out(xs, og_f, og_b, z_gla, or_f, or_b, bonus, g_rw, p, mod, n_ctx, *, tm):
    B, S, D = xs.shape
    W = RW_W
    row = lambda width: pl.BlockSpec((1, tm, width), lambda b, j: (b, j, 0))
    full = lambda shape: pl.BlockSpec(shape, lambda b, j: (0,) * len(shape))
    return pl.pallas_call(
        functools.partial(_even_out_kernel, tm=tm, n_ctx=n_ctx),
        grid=(B, S // tm),
        in_specs=[row(D), row(GLA_V), row(GLA_V),
                  pl.BlockSpec((1, tm, GLA_V), lambda b, j: (b, j, (2 * GLA_QK + GLA_V) // GLA_V)),
                  row(W), row(W), row(W), row(W),
                  full((1, GLA_V)), full((1, W)), full((1, W)), full((GLA_V, GLA_V)), full((W, W)),
                  full((GLA_V + W, D)), *_mod_specs(B, D, 2, 2)],
        out_specs=row(D),
        out_shape=jax.ShapeDtypeStruct((B, S, D), F32),
        compiler_params=_cp("parallel", "parallel"),
        name="even_out",
    )(xs, og_f, og_b, z_gla, or_f, or_b, bonus, g_rw, p["gla_g"], p["ln_g"], p["ln_b"], p["ones128"], p["ones64"],
      p["w_out"], mod, mod)


def _odd_prep_kernel(z_ref, cos_ref, sin_ref, qg_ref, kg_ref, ones64, qc_o, kc_o, vc_o, qd_o, kd_o, vd_o):
    z = z_ref[0]
    cos, sin = cos_ref[...], sin_ref[...]
    o64 = ones64[...]
    nq = C_HEADS * HEAD_DIM
    nk = C_KV * HEAD_DIM

    def norm_rope(x, g, width):
        ms = _split_dot_r(x * x, o64[:width, :width], 2) * (1.0 / HEAD_DIM)
        y = x * lax.rsqrt(ms + 1e-6) * g
        lane = lax.broadcasted_iota(jnp.int32, (1, width), 1)
        first = (lane % (HEAD_DIM // 2)) < (HEAD_DIM // 4)
        swapped = jnp.where(first, pltpu.roll(y, width - HEAD_DIM // 4, 1), pltpu.roll(y, HEAD_DIM // 4, 1))
        return y * cos[:, :width] + swapped * sin[:, :width]

    qc_o[0] = (norm_rope(z[:, 0:nq], qg_ref[...], nq) * HEAD_DIM ** -0.5).astype(BF16)
    kc_o[0] = norm_rope(z[:, nq:nq + nk], kg_ref[:, :nk], nk).astype(BF16)
    vc_o[0] = z[:, nq + nk:nq + 2 * nk].astype(BF16)
    base = nq + 2 * nk
    nd = NA_HEADS * HEAD_DIM
    qd_o[0] = (z[:, base:base + nd] * HEAD_DIM ** -0.5).astype(BF16)
    kd_o[0] = z[:, base + nd:base + 2 * nd].astype(BF16)
    vd_o[0] = z[:, base + 2 * nd:base + 3 * nd].astype(BF16)


def _odd_prep(z, cos_t, sin_t, qg, kg, ones64, *, tt):
    B, S, Z = z.shape
    nq, nk, nd = C_HEADS * HEAD_DIM, C_KV * HEAD_DIM, NA_HEADS * HEAD_DIM
    full = lambda shape: pl.BlockSpec(shape, lambda b, j: (0,) * len(shape))
    row = lambda width: pl.BlockSpec((1, tt, width), lambda b, j: (b, j, 0))
    widths = (nq, nk, nk, nd, nd, nd)
    return pl.pallas_call(
        _odd_prep_kernel,
        grid=(B, S // tt),
        in_specs=[row(Z), pl.BlockSpec((tt, nq), lambda b, j: (j, 0)), pl.BlockSpec((tt, nq), lambda b, j: (j, 0)),
                  full((1, nq)), full((1, nq)), full((nq, nq))],
        out_specs=[row(w) for w in widths],
        out_shape=[jax.ShapeDtypeStruct((B, S, w), BF16) for w in widths],
        compiler_params=_cp("parallel", "parallel"),
        name="odd_prep",
    )(z, cos_t, sin_t, qg, kg, ones64)


def _pair_attn_kernel(q_ref, k_ref, v_ref, o_ref, *, shared_kv):
    q = q_ref[0]
    tq = q.shape[0]
    ngroups = q.shape[1] // LANE
    lane = lax.broadcasted_iota(jnp.int32, (1, LANE), 1)
    masks = (lane < HEAD_DIM, lane >= HEAD_DIM)
    zero = jnp.zeros((), q.dtype)
    if shared_kv:
        kp, vp = k_ref[0], v_ref[0]
        for hm in range(2):
            lhs = jnp.concatenate([jnp.where(masks[hm], q[:, g * LANE:(g + 1) * LANE], zero) for g in range(ngroups)],
                                  axis=0)
            s = _dot_nt(lhs, kp)
            p = jnp.exp(s - jnp.max(s, axis=-1, keepdims=True))
            o = _dot(p, vp) / jnp.sum(p, axis=-1, keepdims=True)
            for g in range(ngroups):
                blk = o[g * tq:(g + 1) * tq]
                if hm == 0:
                    o_ref[0, :, g * LANE:(g + 1) * LANE] = blk.astype(o_ref.dtype)
                else:
                    prev = o_ref[0, :, g * LANE:(g + 1) * LANE]
                    o_ref[0, :, g * LANE:(g + 1) * LANE] = jnp.where(masks[0], prev, blk.astype(o_ref.dtype))
    else:
        for g in range(ngroups):
            qp = q[:, g * LANE:(g + 1) * LANE]
            kp = k_ref[0, :, g * LANE:(g + 1) * LANE]
            vp = v_ref[0, :, g * LANE:(g + 1) * LANE]
            lhs = jnp.concatenate([jnp.where(masks[0], qp, zero), jnp.where(masks[1], qp, zero)], axis=0)
            s = _dot_nt(lhs, kp)
            p = jnp.exp(s - jnp.max(s, axis=-1, keepdims=True))
            o = _dot(p, vp) / jnp.sum(p, axis=-1, keepdims=True)
            o_ref[0, :, g * LANE:(g + 1) * LANE] = jnp.where(masks[0], o[:tq], o[tq:]).astype(o_ref.dtype)


def _pair_attention(q, k, v, *, q_row0, n_q, n_k, tq, shared_kv, name):
    B, S, QW = q.shape
    KW = k.shape[2]
    return pl.pallas_call(
        functools.partial(_pair_attn_kernel, shared_kv=shared_kv),
        grid=(B, n_q // tq),
        in_specs=[pl.BlockSpec((1, tq, QW), lambda b, i: (b, q_row0 // tq + i, 0)),
                  pl.BlockSpec((1, n_k, KW), lambda b, i: (b, 0, 0)),
                  pl.BlockSpec((1, n_k, KW), lambda b, i: (b, 0, 0))],
        out_specs=pl.BlockSpec((1, tq, QW), lambda b, i: (b, i, 0)),
        out_shape=jax.ShapeDtypeStruct((B, n_q, QW), BF16),
        compiler_params=_cp("parallel", "parallel"),
        name=name,
    )(q, k, v)


def _na_kernel(q_ref, k_ref, v_ref, bias_ref, o_ref, *, n_ctx, n_rows, kh):
    r = pl.program_id(1)
    rs = jnp.clip(r - kh // 2, 0, n_rows - kh)
    start = pl.multiple_of(n_ctx + rs * GRID_W, GRID_W)
    nkeys = kh * GRID_W
    q = q_ref[0]
    lane = lax.broadcasted_iota(jnp.int32, (1, LANE), 1)
    m0 = lane < HEAD_DIM
    zero = jnp.zeros((), q.dtype)
    for g in range(NA_HEADS // 2):
        gs = slice(g * LANE, (g + 1) * LANE)
        qp = q[:, gs]
        lhs = jnp.concatenate([jnp.where(m0, qp, zero), jnp.where(m0, zero, qp)], axis=0)
        k_nb = k_ref[0, pl.ds(start, nkeys), gs]
        v_nb = v_ref[0, pl.ds(start, nkeys), gs]
        k_cx = k_ref[0, 0:n_ctx, gs]
        v_cx = v_ref[0, 0:n_ctx, gs]
        bias = jnp.concatenate([bias_ref[0, 2 * g], bias_ref[0, 2 * g + 1]], axis=0)
        s_nb = _dot_nt(lhs, k_nb) + bias
        s_cx = _dot_nt(lhs, k_cx)
        m = jnp.maximum(jnp.max(s_nb, axis=-1, keepdims=True), jnp.max(s_cx, axis=-1, keepdims=True))
        p_nb = jnp.exp(s_nb - m)
        p_cx = jnp.exp(s_cx - m)
        den = jnp.sum(p_nb, axis=-1, keepdims=True) + jnp.sum(p_cx, axis=-1, keepdims=True)
        o = (_dot(p_nb, v_nb) + _dot(p_cx, v_cx)) / den
        o_ref[0, :, gs] = jnp.where(m0, o[:GRID_W], o[GRID_W:]).astype(o_ref.dtype)


def _na_attention(qd, kd, vd, bias_tab, n_ctx):
    B, S, W = qd.shape
    T = S - n_ctx
    n_rows = T // GRID_W
    kh = min(NA_KH, n_rows)
    nkeys = kh * GRID_W

    def bias_idx(b, r):
        return (jnp.clip(r - kh // 2, 0, n_rows - kh) - r + kh - 1, 0, 0, 0)

    return pl.pallas_call(
        functools.partial(_na_kernel, n_ctx=n_ctx, n_rows=n_rows, kh=kh),
        grid=(B, n_rows),
        in_specs=[pl.BlockSpec((1, GRID_W, W), lambda b, r: (b, n_ctx // GRID_W + r, 0)),
                  pl.BlockSpec((1, S, W), lambda b, r: (b, 0, 0)),
                  pl.BlockSpec((1, S, W), lambda b, r: (b, 0, 0)),
                  pl.BlockSpec((1, NA_HEADS, GRID_W, nkeys), bias_idx)],
        out_specs=pl.BlockSpec((1, GRID_W, W), lambda b, r: (b, r, 0)),
        out_shape=jax.ShapeDtypeStruct((B, T, W), BF16),
        compiler_params=_cp("parallel", "arbitrary"),
        name="na_attention",
    )(qd, kd, vd, bias_tab)


def _na_bias_table(rpb, n_rows):
    kh = min(NA_KH, n_rows)
    kw = min(NA_KW, GRID_W)
    col = np.arange(GRID_W)
    start = np.clip(col - kw // 2, 0, GRID_W - kw)
    in_win = (col[None, :] >= start[:, None]) & (col[None, :] < start[:, None] + kw)
    dc = np.clip(col[None, :] - col[:, None], -(NA_KW - 1), NA_KW - 1) + NA_KW - 1
    bias_cols = rpb[:, :, dc]
    slabs = []
    for st in range(kh):
        sl = bias_cols[:, st + NA_KH - kh:st + NA_KH - kh + kh]
        sl = jnp.where(in_win[None, None], sl, NEG).transpose(0, 2, 1, 3)
        slabs.append(sl.reshape(NA_HEADS, GRID_W, kh * GRID_W))
    return jnp.stack(slabs).astype(F32)


def _proj_res_kernel(x_ref, y_ref, w_ref, gl, gc, o_ref, *, tm, n_ctx):
    rows = pl.program_id(1) * tm + lax.broadcasted_iota(jnp.int32, (tm, 1), 0)
    gate = jnp.where(rows < n_ctx, gc[0], gl[0])
    o_ref[0] = x_ref[0] + gate * jnp.dot(y_ref[0], w_ref[...], preferred_element_type=F32)


def _proj_res(xs, y, w, mod, n_ctx, *, tm):
    B, S, D = xs.shape
    K = y.shape[2]
    row = lambda width: pl.BlockSpec((1, tm, width), lambda b, j: (b, j, 0))
    return pl.pallas_call(
        functools.partial(_proj_res_kernel, tm=tm, n_ctx=n_ctx),
        grid=(B, S // tm),
        in_specs=[row(D), row(K), pl.BlockSpec((K, D), lambda b, j: (0, 0)), *_mod_specs(B, D, 2, 2)],
        out_specs=row(D),
        out_shape=jax.ShapeDtypeStruct((B, S, D), F32),
        compiler_params=_cp("parallel", "parallel"),
        name="proj_res",
    )(xs, y, w, mod, mod)


def _final_norm_kernel(x_ref, g_ref, o_ref):
    x = x_ref[0]
    o_ref[0] = x * lax.rsqrt(jnp.mean(x * x, axis=-1, keepdims=True) + 1e-6) * g_ref[...]


def _final_norm(xs, g, n_ctx, *, tm):
    B, S, D = xs.shape
    T = S - n_ctx
    return pl.pallas_call(
        _final_norm_kernel,
        grid=(B, T // tm),
        in_specs=[pl.BlockSpec((1, tm, D), lambda b, j: (b, n_ctx // tm + j, 0)),
                  pl.BlockSpec((1, D), lambda b, j: (0, 0))],
        out_specs=pl.BlockSpec((1, tm, D), lambda b, j: (b, j, 0)),
        out_shape=jax.ShapeDtypeStruct((B, T, D), F32),
        compiler_params=_cp("parallel", "parallel"),
        name="final_norm",
    )(xs, g.reshape(1, D))


def _block_ones(width, block):
    idx = np.arange(width) // block
    return jnp.asarray(idx[:, None] == idx[None, :], BF16)


def _pad_cols(w, width):
    return jnp.pad(w, ((0, 0), (0, width - w.shape[1])))


def _pad_rows(w, height):
    return jnp.pad(w, ((0, height - w.shape[0]), (0, 0)))


def _even_params(j, ev_w_in, ev_w_out, gla_a_up, gla_a_bias, gla_norm_g, rw_mu, rw_w0, rw_w_up, rw_a0, rw_a_up,
                 rw_g_up, rw_k_k, rw_k_a, rw_r_k, rw_ln_g, rw_ln_b):
    w_in = ev_w_in[j]
    D = w_in.shape[0]
    gq = 2 * GLA_QK + 2 * GLA_V
    w_gla = jnp.concatenate([w_in[:, :gq], _pad_cols(w_in[:, gq:gq + GLA_RANK], LANE),
                             _pad_cols(w_in[:, gq + GLA_RANK:gq + 2 * GLA_RANK], LANE)], axis=1)
    o = gq + 2 * GLA_RANK
    sizes = (3 * RW_W, RW_DECAY_RANK, RW_DECAY_RANK, RW_A_RANK, RW_G_RANK)
    offs = np.cumsum((0,) + sizes)
    pieces = [w_in[:, o:o + 3 * RW_W]] + [_pad_cols(w_in[:, o + offs[i]:o + offs[i + 1]], LANE) for i in range(1, 5)]
    w_rw = jnp.concatenate(pieces, axis=1)
    mu = rw_mu[j]
    mu_pieces = [mu[:, :3 * RW_W]] + [_pad_cols(mu[:, offs[i]:offs[i + 1]], LANE) for i in range(1, 5)]
    return {
        "w_gla": w_gla.astype(BF16), "w_rw": w_rw.astype(BF16), "w_out": ev_w_out[j].astype(BF16),
        "gla_aup": jnp.stack([_pad_rows(gla_a_up[j, d], LANE) for d in range(2)]).astype(BF16),
        "gla_bias": gla_a_bias[j], "gla_g": jnp.tile(gla_norm_g[j], GLA_HEADS).reshape(1, GLA_V),
        "mu": jnp.concatenate(mu_pieces, axis=1), "w0": rw_w0[j],
        "wup": jnp.stack([_pad_rows(rw_w_up[j, d], LANE) for d in range(2)]).astype(BF16),
        "a0": rw_a0[j].reshape(1, RW_W), "aup": _pad_rows(rw_a_up[j], LANE).astype(BF16),
        "gup": _pad_rows(rw_g_up[j], LANE).astype(BF16),
        "k_k": rw_k_k[j].reshape(1, RW_W), "k_a": rw_k_a[j].reshape(1, RW_W), "r_k": rw_r_k[j].reshape(1, RW_W),
        "ln_g": rw_ln_g[j].reshape(1, RW_W), "ln_b": rw_ln_b[j].reshape(1, RW_W),
        "ones64": _block_ones(RW_W, RW_DH), "ones128": _block_ones(GLA_V, GLA_DV),
    }


def _gqa_head_order():
    per_kv = C_HEADS // C_KV
    order = []
    for jj in range(per_kv):
        for kv in range(C_KV):
            h = kv * per_kv + jj
            order.extend(range(h * HEAD_DIM, (h + 1) * HEAD_DIM))
    return np.asarray(order)


def _rope_tables(n_ctx, T, width):
    t = jnp.arange(T)
    pos = jnp.stack([t // GRID_W, t % GRID_W], axis=-1).astype(F32)
    half = HEAD_DIM // 2
    inv = ROPE_THETA ** (-jnp.arange(0, half, 2, dtype=F32) / half)
    ang = pos[:, :, None] * inv
    cos, sin = jnp.cos(ang), jnp.sin(ang)
    cos_h = jnp.concatenate([cos, cos], axis=-1).reshape(T, HEAD_DIM)
    sin_h = jnp.concatenate([-sin, sin], axis=-1).reshape(T, HEAD_DIM)
    cos_h = jnp.concatenate([jnp.ones((n_ctx, HEAD_DIM), F32), cos_h], axis=0)
    sin_h = jnp.concatenate([jnp.zeros((n_ctx, HEAD_DIM), F32), sin_h], axis=0)
    reps = width // HEAD_DIM
    return jnp.tile(cos_h, (1, reps)), jnp.tile(sin_h, (1, reps))


def _even_layer(xs, mod, n_ctx, norm1_g, p):
    z_gla = _norm_mod_matmul(xs, norm1_g, mod, p["w_gla"], n_ctx, tm=256, tn=GLA_Z // 2, name="even_in_gla")
    z_rw = _norm_mod_matmul(xs, norm1_g, mod, p["w_rw"], n_ctx, tm=256, tn=RW_Z // 2, name="even_in_rw")
    og_f = _gla_scan(z_gla, p["gla_aup"], p["gla_bias"], n_ctx, False)
    og_b = _gla_scan(z_gla, p["gla_aup"], p["gla_bias"], n_ctx, True)
    r, lwf, lwb, k, v, kk, a, g_rw, bonus = _rw_features(z_rw, p, n_ctx)
    or_f = _rw_scan(r, lwf, k, v, kk, a, n_ctx, False)
    or_b = _rw_scan(r, lwb, k, v, kk, a, n_ctx, True)
    return _even_out(xs, og_f, og_b, z_gla, or_f, or_b, bonus, g_rw, p, mod, n_ctx, tm=256)


def _odd_layer(xs, mod, n_ctx, norm1_g, p, need_ctx):
    B, S, D = xs.shape
    T = S - n_ctx
    z = _norm_mod_matmul(xs, norm1_g, mod, p["w_in"], n_ctx, tm=256, tn=p["w_in"].shape[1] // 2, name="odd_in")
    qc, kc, vc, qd, kd, vd = _odd_prep(z, p["cos"], p["sin"], p["qg"], p["kg"], p["ones64"], tt=256)
    y_gqa = _pair_attention(qc, kc, vc, q_row0=n_ctx, n_q=T, n_k=S, tq=128, shared_kv=True, name="gqa")
    y_na = _na_attention(qd, kd, vd, p["bias_tab"], n_ctx)
    parts_g, parts_n = [y_gqa], [y_na]
    if need_ctx:
        parts_g.insert(0, _pair_attention(qc, kc, vc, q_row0=0, n_q=n_ctx, n_k=n_ctx, tq=n_ctx, shared_kv=True,
                                          name="gqa_ctx"))
        parts_n.insert(0, _pair_attention(qd, kd, vd, q_row0=0, n_q=n_ctx, n_k=n_ctx, tq=n_ctx, shared_kv=False,
                                          name="na_ctx"))
    else:
        zeros = jnp.zeros((B, n_ctx, y_gqa.shape[2]), BF16)
        parts_g.insert(0, zeros)
        parts_n.insert(0, zeros)
    y = jnp.concatenate([jnp.concatenate(parts_g, axis=1), jnp.concatenate(parts_n, axis=1)], axis=2)
    return _proj_res(xs, y, p["w_out"], mod, n_ctx, tm=256)


def kernel(x, c, ctx, c_ctx, w_mod, b_mod, norm1_g, norm2_g, ffn_w13, ffn_w2, ev_w_in, ev_w_out, gla_a_up,
           gla_a_bias, gla_norm_g, rw_mu, rw_w0, rw_w_up, rw_a0, rw_a_up, rw_g_up, rw_k_k, rw_k_a, rw_r_k, rw_ln_g,
           rw_ln_b, od_w_in, od_w_out, cq_norm_g, ck_norm_g, na_rpb, final_g):
    B, T, D = x.shape
    n_ctx = ctx.shape[1]
    xs = jnp.concatenate([ctx, x], axis=1)
    mods = _mod_vectors(c, c_ctx, w_mod, b_mod)
    nq, nk = C_HEADS * HEAD_DIM, C_KV * HEAD_DIM
    order = _gqa_head_order()
    cos_t, sin_t = _rope_tables(n_ctx, T, nq)
    ones64 = _block_ones(nq, HEAD_DIM)
    for i in range(DEPTH):
        j = i // 2
        mod = mods[i]
        if i % 2 == 0:
            p = _even_params(j, ev_w_in, ev_w_out, gla_a_up, gla_a_bias, gla_norm_g, rw_mu, rw_w0, rw_w_up, rw_a0,
                             rw_a_up, rw_g_up, rw_k_k, rw_k_a, rw_r_k, rw_ln_g, rw_ln_b)
            xs = _even_layer(xs, mod, n_ctx, norm1_g[i], p)
        else:
            w_in = od_w_in[j]
            w_out = od_w_out[j]
            p = {
                "w_in": jnp.concatenate([w_in[:, :nq][:, order], w_in[:, nq:]], axis=1).astype(BF16),
                "w_out": jnp.concatenate([w_out[:nq][order], w_out[nq:]], axis=0).astype(BF16),
                "cos": cos_t, "sin": sin_t, "ones64": ones64,
                "qg": jnp.tile(cq_norm_g[j], C_HEADS).reshape(1, nq),
                "kg": jnp.tile(ck_norm_g[j], C_HEADS).reshape(1, nq),
                "bias_tab": _na_bias_table(na_rpb[j], T // GRID_W),
            }
            xs = _odd_layer(xs, mod, n_ctx, norm1_g[i], p, need_ctx=i < DEPTH - 1)
        xs = _ffn(xs, norm2_g[i], mod, ffn_w13[i].astype(BF16), ffn_w2[i].astype(BF16), n_ctx, tm=256,
                  tf=ffn_w2.shape[1] // 11 if ffn_w2.shape[1] % 11 == 0 else ffn_w2.shape[1])
    return _final_norm(xs, final_g, n_ctx, tm=256)
```

```python
import functools

import jax
import jax.numpy as jnp
import numpy as np
from jax import lax
from jax.experimental import pallas as pl
from jax.experimental.pallas import tpu as pltpu

F32 = jnp.float32
BF16 = jnp.bfloat16

DEPTH = 4
GRID_W = 64
CHUNK = 64
LANE = 128

GLA_HEADS, GLA_DK, GLA_DV, GLA_RANK, GLA_TAU = 4, 64, 128, 16, 16.0
GLA_QK, GLA_V = GLA_HEADS * GLA_DK, GLA_HEADS * GLA_DV
GLA_Z = 2 * GLA_QK + 2 * GLA_V + 2 * LANE

RW_HEADS, RW_DH, RW_W = 8, 64, 512
RW_DECAY_RANK, RW_A_RANK, RW_G_RANK = 32, 32, 96
RW_GN_EPS = 64e-5
RW_Z = 3 * RW_W + 4 * LANE

HEAD_DIM, C_HEADS, C_KV, NA_HEADS, NA_KH, NA_KW = 64, 8, 2, 8, 8, 16
ROPE_THETA = 10000.0
NEG = -1e30

VMEM_LIMIT = 56 * 1024 * 1024
ROW_CAP_MATMUL = 1152
ROW_CAP_ELEMENTWISE = 256
FFN_COL_TILE = 256


def _row_tile(n_rows, cap):
    return max(d for d in range(8, cap + 1, 8) if n_rows % d == 0)


def _cp(*sem):
    return pltpu.CompilerParams(dimension_semantics=sem, vmem_limit_bytes=VMEM_LIMIT)


def _dot(a, b):
    return jnp.dot(a.astype(BF16), b.astype(BF16), preferred_element_type=F32)


def _dot_nt(a, b):
    return lax.dot_general(a.astype(BF16), b.astype(BF16), (((1,), (1,)), ((), ())), preferred_element_type=F32)


def _dot_tn(a, b):
    return lax.dot_general(a.astype(BF16), b.astype(BF16), (((0,), (0,)), ((), ())), preferred_element_type=F32)


def _split_dot(m, x, parts):
    acc = None
    rem = x
    for _ in range(parts):
        piece = rem.astype(BF16)
        rem = rem - piece.astype(F32)
        t = jnp.dot(m, piece, preferred_element_type=F32)
        acc = t if acc is None else acc + t
    return acc


def _split_dot_r(x, m, parts):
    acc = None
    rem = x
    for _ in range(parts):
        piece = rem.astype(BF16)
        rem = rem - piece.astype(F32)
        t = jnp.dot(piece, m, preferred_element_type=F32)
        acc = t if acc is None else acc + t
    return acc


def _softplus(y):
    return jnp.maximum(y, 0.0) + jnp.log1p(jnp.exp(-jnp.abs(y)))


def _rms_mod(x, g, sc_l, sh_l, sc_c, sh_c, row0, n_ctx):
    y = x * lax.rsqrt(jnp.mean(x * x, axis=-1, keepdims=True) + 1e-6) * g
    rows = row0 + lax.broadcasted_iota(jnp.int32, (x.shape[0], 1), 0)
    is_ctx = rows < n_ctx
    sc = jnp.where(is_ctx, sc_c, sc_l)
    sh = jnp.where(is_ctx, sh_c, sh_l)
    return y * (1.0 + sc) + sh


def _tri_masks(n, reverse):
    ri = lax.broadcasted_iota(jnp.int32, (n, n), 0)
    ci = lax.broadcasted_iota(jnp.int32, (n, n), 1)
    if reverse:
        return ci >= ri, ci > ri
    return ci <= ri, ci < ri


INV_BASE = 16


def _block_ids(n):
    ri = lax.broadcasted_iota(jnp.int32, (n, n), 0)
    ci = lax.broadcasted_iota(jnp.int32, (n, n), 1)
    return ri, ci


def _unit_tri_inverse(a, blk):
    ri, ci = blk
    n = a[0].shape[0]
    ids = range(len(a))
    same = lambda bs: (ri // bs) == (ci // bs)
    base = same(INV_BASE)
    eye = (ri == ci).astype(F32)
    p = [jnp.where(base, x, 0.0) for x in a]
    t = [eye + x for x in p]
    m = 1
    while 2 * m < INV_BASE:
        p = [_dot(x, x) for x in p]
        tp = [_dot(t[i], p[i]) for i in ids]
        t = [t[i] + tp[i] for i in ids]
        m *= 2
    bs = INV_BASE
    while bs < n:
        mask = same(2 * bs) & jnp.logical_not(same(bs))
        ta = [_dot(t[i], jnp.where(mask, a[i], 0.0)) for i in ids]
        tat = [_dot(ta[i], t[i]) for i in ids]
        t = [t[i] + tat[i] for i in ids]
        bs *= 2
    return t


def _mod_kernel(s_ref, w_ref, b_ref, o_ref):
    s = s_ref[...]
    s = s * jax.nn.sigmoid(s)
    o_ref[0] = _dot(s, w_ref[0]) + b_ref[0]


def _mod_vectors(c, c_ctx, w_mod, b_mod):
    B, D = c.shape
    bp = -(-(B + 1) // 8) * 8
    s_in = jnp.zeros((bp, D), F32).at[:B].set(c).at[B].set(c_ctx)
    n6 = w_mod.shape[-1]
    tn = 1536
    out = pl.pallas_call(
        _mod_kernel,
        grid=(DEPTH, n6 // tn),
        in_specs=[pl.BlockSpec((bp, D), lambda l, n: (0, 0)),
                  pl.BlockSpec((1, D, tn), lambda l, n: (l, 0, n)),
                  pl.BlockSpec((1, 1, tn), lambda l, n: (l, 0, n))],
        out_specs=pl.BlockSpec((1, bp, tn), lambda l, n: (l, 0, n)),
        out_shape=jax.ShapeDtypeStruct((DEPTH, bp, n6), F32),
        compiler_params=_cp("parallel", "parallel"),
        name="mod_vectors",
    )(s_in, w_mod, b_mod.reshape(DEPTH, 1, n6))
    return out.reshape(DEPTH, bp, 1, n6)


def _mod_specs(B, D, idx, grid_rank):
    if grid_rank == 2:
        return [pl.BlockSpec((1, 1, D), lambda b, j: (b, 0, idx)),
                pl.BlockSpec((1, 1, D), lambda b, j: (B, 0, idx))]
    return [pl.BlockSpec((1, 1, D), lambda b, j, n: (b, 0, idx)),
            pl.BlockSpec((1, 1, D), lambda b, j, n: (B, 0, idx))]


def _nmm_kernel(x_ref, g_ref, shl, shc, scl, scc, w_ref, o_ref, h_scr, *, tm, n_ctx):
    @pl.when(pl.program_id(2) == 0)
    def _():
        h = _rms_mod(x_ref[0], g_ref[...], scl[0], shl[0], scc[0], shc[0], pl.program_id(1) * tm, n_ctx)
        h_scr[...] = h.astype(BF16)

    o_ref[0] = jnp.dot(h_scr[...], w_ref[...], preferred_element_type=F32).astype(o_ref.dtype)


def _norm_mod_matmul(xs, g, mod, w, n_ctx, *, tm, tn, name):
    B, S, D = xs.shape
    N = w.shape[1]
    nB = B
    return pl.pallas_call(
        functools.partial(_nmm_kernel, tm=tm, n_ctx=n_ctx),
        grid=(B, S // tm, N // tn),
        in_specs=[pl.BlockSpec((1, tm, D), lambda b, j, n: (b, j, 0)),
                  pl.BlockSpec((1, D), lambda b, j, n: (0, 0)),
                  *_mod_specs(nB, D, 0, 3), *_mod_specs(nB, D, 1, 3),
                  pl.BlockSpec((D, tn), lambda b, j, n: (0, n))],
        out_specs=pl.BlockSpec((1, tm, tn), lambda b, j, n: (b, j, n)),
        out_shape=jax.ShapeDtypeStruct((B, S, N), F32),
        scratch_shapes=[pltpu.VMEM((tm, D), BF16)],
        compiler_params=_cp("parallel", "parallel", "arbitrary"),
        name=name,
    )(xs, g.reshape(1, D), mod, mod, mod, mod, w)


def _ffn_kernel(x_ref, g_ref, shl, shc, scl, scc, gl, gc, w1_ref, w3_ref, w2_ref, o_ref, h_scr, acc_scr,
                *, tm, n_ctx):
    f = pl.program_id(2)

    @pl.when(f == 0)
    def _():
        h = _rms_mod(x_ref[0], g_ref[...], scl[0], shl[0], scc[0], shc[0], pl.program_id(1) * tm, n_ctx)
        h_scr[...] = h.astype(BF16)
        acc_scr[...] = jnp.zeros_like(acc_scr)

    h = h_scr[...]
    a = jnp.dot(h, w1_ref[...], preferred_element_type=F32)
    b = jnp.dot(h, w3_ref[...], preferred_element_type=F32)
    u = (a * jax.nn.sigmoid(a) * b).astype(BF16)
    acc_scr[...] += jnp.dot(u, w2_ref[...], preferred_element_type=F32)

    @pl.when(f == pl.num_programs(2) - 1)
    def _():
        rows = pl.program_id(1) * tm + lax.broadcasted_iota(jnp.int32, (tm, 1), 0)
        gate = jnp.where(rows < n_ctx, gc[0], gl[0])
        o_ref[0] = x_ref[0] + gate * acc_scr[...]


def _ffn(xs, g, mod, w13, w2, n_ctx, *, tm, tf):
    B, S, D = xs.shape
    Fd = w2.shape[0]
    nf = Fd // tf
    return pl.pallas_call(
        functools.partial(_ffn_kernel, tm=tm, n_ctx=n_ctx),
        grid=(B, S // tm, nf),
        in_specs=[pl.BlockSpec((1, tm, D), lambda b, j, f: (b, j, 0)),
                  pl.BlockSpec((1, D), lambda b, j, f: (0, 0)),
                  *_mod_specs(B, D, 3, 3), *_mod_specs(B, D, 4, 3), *_mod_specs(B, D, 5, 3),
                  pl.BlockSpec((D, tf), lambda b, j, f: (0, f)),
                  pl.BlockSpec((D, tf), lambda b, j, f: (0, nf + f)),
                  pl.BlockSpec((tf, D), lambda b, j, f: (f, 0))],
        out_specs=pl.BlockSpec((1, tm, D), lambda b, j, f: (b, j, 0)),
        out_shape=jax.ShapeDtypeStruct((B, S, D), F32),
        scratch_shapes=[pltpu.VMEM((tm, D), BF16), pltpu.VMEM((tm, D), F32)],
        compiler_params=_cp("parallel", "parallel", "arbitrary"),
        name="ffn",
    )(xs, g.reshape(1, D), mod, mod, mod, mod, mod, mod, w13, w13, w2)


def _chunk_index(s, nc_ctx, nc, reverse):
    if not reverse:
        return s
    return jnp.where(s < nc_ctx, nc_ctx - 1 - s, nc + nc_ctx - 1 - s)


def _gla_kernel(qkv_f, ad_f, qkv_b, ad_b, aup_ref, bias_ref, of_ref, ob_ref, st_scr):
    @pl.when(pl.program_id(1) == 0)
    def _():
        st_scr[...] = jnp.zeros_like(st_scr)

    L = CHUNK
    qb, kb, kt, gt, v, incl = [], [], [], [], [], []
    for d, (qkv_ref, ad_ref) in enumerate(((qkv_f, ad_f), (qkv_b, ad_b))):
        m, _ = _tri_masks(L, d == 1)
        ad = ad_ref[0][:, d * LANE:(d + 1) * LANE]
        y = _dot(ad, aup_ref[d]) + bias_ref[d:d + 1, :]
        la = (jnp.minimum(y, 0.0) - jnp.log1p(jnp.exp(-jnp.abs(y)))) * (1.0 / GLA_TAU)
        cs = _split_dot(m.astype(BF16), la, 3)
        tot = cs[0:1] if d == 1 else cs[L - 1:L]
        qkv = qkv_ref[0]
        q = qkv[:, 0:GLA_QK] * GLA_DK ** -0.5
        k = qkv[:, GLA_QK:2 * GLA_QK]
        incl.append(m)
        v.append(qkv[:, 2 * GLA_QK:2 * GLA_QK + GLA_V].astype(BF16))
        qb.append((q * jnp.exp(cs)).astype(BF16))
        kb.append((k * jnp.exp(-cs)).astype(BF16))
        kt.append((k * jnp.exp(tot - cs)).astype(BF16))
        gt.append(jnp.exp(tot))
    es = [(d, h) for d in range(2) for h in range(GLA_HEADS)]
    ks = lambda h: slice(h * GLA_DK, (h + 1) * GLA_DK)
    vs = lambda h: slice(h * GLA_DV, (h + 1) * GLA_DV)
    st = [st_scr[d, h] for d, h in es]
    att = [jnp.where(incl[d], _dot_nt(qb[d][:, ks(h)], kb[d][:, ks(h)]), 0.0) for d, h in es]
    o_state = [_dot_nt(qb[d][:, ks(h)], st[i]) for i, (d, h) in enumerate(es)]
    o_att = [_dot(att[i], v[d][:, vs(h)]) for i, (d, h) in enumerate(es)]
    upd = [_dot_tn(v[d][:, vs(h)], kt[d][:, ks(h)]) for d, h in es]
    for i, (d, h) in enumerate(es):
        (of_ref, ob_ref)[d][0, :, vs(h)] = o_att[i] + o_state[i]
        st_scr[d, h] = st[i] * gt[d][:, ks(h)] + upd[i]


def _scan_index_maps(S, n_ctx):
    nc, nc_ctx = S // CHUNK, n_ctx // CHUNK
    fwd = functools.partial(_chunk_index, nc_ctx=nc_ctx, nc=nc, reverse=False)
    bwd = functools.partial(_chunk_index, nc_ctx=nc_ctx, nc=nc, reverse=True)
    return nc, fwd, bwd


def _gla_scan(z_gla, aup_pad, bias, n_ctx):
    B, S, _ = z_gla.shape
    nc, fwd, bwd = _scan_index_maps(S, n_ctx)
    ad_blk = (2 * GLA_QK + 2 * GLA_V) // (2 * LANE)
    qkv_spec = lambda cidx: pl.BlockSpec((1, CHUNK, 2 * GLA_QK + GLA_V), lambda b, s: (b, cidx(s), 0))
    ad_spec = lambda cidx: pl.BlockSpec((1, CHUNK, 2 * LANE), lambda b, s: (b, cidx(s), ad_blk))
    out_spec = lambda cidx: pl.BlockSpec((1, CHUNK, GLA_V), lambda b, s: (b, cidx(s), 0))
    return pl.pallas_call(
        _gla_kernel,
        grid=(B, nc),
        in_specs=[qkv_spec(fwd), ad_spec(fwd), qkv_spec(bwd), ad_spec(bwd),
                  pl.BlockSpec((2, LANE, GLA_QK), lambda b, s: (0, 0, 0)),
                  pl.BlockSpec((2, GLA_QK), lambda b, s: (0, 0))],
        out_specs=[out_spec(fwd), out_spec(bwd)],
        out_shape=[jax.ShapeDtypeStruct((B, S, GLA_V), F32)] * 2,
        scratch_shapes=[pltpu.VMEM((2, GLA_HEADS, GLA_DV, GLA_DK), F32)],
        compiler_params=_cp("parallel", "arbitrary"),
        name="gla_scan",
    )(z_gla, z_gla, z_gla, z_gla, aup_pad, bias)


def _rw_feat_kernel(z_ref, zp_ref, zn_ref, mu_ref, w0_ref, wup_ref, a0_ref, aup_ref, gup_ref, kk_ref, ka_ref,
                    rk_ref, ones_ref, r_o, lwf_o, lwb_o, k_o, v_o, kk_o, a_o, g_o, bonus_o, *, tt):
    j = pl.program_id(1)
    last = pl.num_programs(1) - 1
    z = z_ref[0]
    prev_row = jnp.where(j <= 1, 0.0, zp_ref[0, 7:8, :])
    next_row = jnp.where((j == 0) | (j == last), 0.0, zn_ref[0, 0:1, :])
    rid = lax.broadcasted_iota(jnp.int32, (tt, 1), 0)
    prev = jnp.where(rid == 0, prev_row, pltpu.roll(z, 1, 0))
    nxt = jnp.where(rid == tt - 1, next_row, pltpu.roll(z, tt - 1, 0))
    zs = z + mu_ref[0:1, :] * (prev - z) + mu_ref[1:2, :] * (nxt - z)

    W = RW_W
    r, k, v = zs[:, 0:W], zs[:, W:2 * W], zs[:, 2 * W:3 * W]
    base = 3 * W
    wd = (zs[:, base:base + LANE], zs[:, base + LANE:base + 2 * LANE])
    ad = zs[:, base + 2 * LANE:base + 3 * LANE]
    gd = zs[:, base + 3 * LANE:base + 4 * LANE]
    for d, lw_o in enumerate((lwf_o, lwb_o)):
        wl = -_softplus(-(w0_ref[d:d + 1, :] + _dot(jnp.tanh(wd[d]), wup_ref[d]))) - 0.5
        lw_o[0] = -jnp.exp(wl)
    a = jax.nn.sigmoid(a0_ref[...] + _dot(ad, aup_ref[...]))
    g = _dot(jax.nn.sigmoid(gd), gup_ref[...])
    ones = ones_ref[...]
    kk = k * kk_ref[...]
    ss = _split_dot_r(kk * kk, ones, 2)
    kk = kk / jnp.maximum(jnp.sqrt(ss), 1e-12)
    kmod = k * (1.0 + (a - 1.0) * ka_ref[...])
    bonus = _split_dot_r(r * kmod * rk_ref[...], ones, 2) * v
    r_o[0] = r
    k_o[0] = kmod
    v_o[0] = v
    kk_o[0] = kk
    a_o[0] = a
    g_o[0] = g
    bonus_o[0] = bonus


def _rw_features(z_rw, p, n_ctx):
    B, S, Z = z_rw.shape
    tt = n_ctx
    nt = S // tt
    n8 = S // 8
    W = RW_W
    full = lambda shape: pl.BlockSpec(shape, lambda b, j: (0,) * len(shape))
    out_spec = pl.BlockSpec((1, tt, W), lambda b, j: (b, j, 0))
    return pl.pallas_call(
        functools.partial(_rw_feat_kernel, tt=tt),
        grid=(B, nt),
        in_specs=[pl.BlockSpec((1, tt, Z), lambda b, j: (b, j, 0)),
                  pl.BlockSpec((1, 8, Z), lambda b, j: (b, jnp.maximum(j * (tt // 8) - 1, 0), 0)),
                  pl.BlockSpec((1, 8, Z), lambda b, j: (b, jnp.minimum((j + 1) * (tt // 8), n8 - 1), 0)),
                  full((2, Z)), full((2, W)), full((2, LANE, W)), full((1, W)), full((LANE, W)), full((LANE, W)),
                  full((1, W)), full((1, W)), full((1, W)), full((W, W))],
        out_specs=[out_spec] * 9,
        out_shape=[jax.ShapeDtypeStruct((B, S, W), F32)] * 9,
        compiler_params=_cp("parallel", "parallel"),
        name="rwkv_features",
    )(z_rw, z_rw, z_rw, p["mu"], p["w0"], p["wup"], p["a0"], p["aup"], p["gup"], p["k_k"], p["k_a"], p["r_k"],
      p["ones64"])


def _rw_scan_kernel(r_f, lw_f, k_f, v_f, kk_f, a_f, r_b, lw_b, k_b, v_b, kk_b, a_b, of_ref, ob_ref, s_scr):
    @pl.when(pl.program_id(1) == 0)
    def _():
        s_scr[...] = jnp.zeros_like(s_scr)

    L = CHUNK
    blk = _block_ids(L)
    incl, strict, at, rt, bh, kh, bl, kl, vb, g_tot = ([] for _ in range(10))
    for d, (r_ref, lw_ref, k_ref, v_ref, kk_ref, a_ref) in enumerate(((r_f, lw_f, k_f, v_f, kk_f, a_f),
                                                                     (r_b, lw_b, k_b, v_b, kk_b, a_b))):
        m_incl, m_strict = _tri_masks(L, d == 1)
        lw = lw_ref[0]
        cs = _split_dot(m_incl.astype(BF16), lw, 3)
        tot = cs[0:1] if d == 1 else cs[L - 1:L]
        kk, k = kk_ref[0], k_ref[0]
        bv = kk * a_ref[0]
        g_inv = jnp.exp(-cs)
        g_rem = jnp.exp(tot - cs)
        incl.append(m_incl)
        strict.append(m_strict)
        g_tot.append(jnp.exp(tot))
        vb.append(v_ref[0].astype(BF16))
        at.append((-(jnp.exp(cs - lw) * kk)).astype(BF16))
        rt.append((jnp.exp(cs) * r_ref[0]).astype(BF16))
        bh.append((g_inv * bv).astype(BF16))
        kh.append((g_inv * k).astype(BF16))
        bl.append((g_rem * bv).astype(BF16))
        kl.append((g_rem * k).astype(BF16))
    es = [(d, h) for d in range(2) for h in range(RW_HEADS)]
    ids = range(len(es))
    sl = lambda h: slice(h * RW_DH, (h + 1) * RW_DH)
    s0 = [s_scr[d, h] for d, h in es]
    lhs = [jnp.concatenate([at[d][:, sl(h)], rt[d][:, sl(h)]], axis=0) for d, h in es]
    rhs = [jnp.concatenate([bh[d][:, sl(h)], kh[d][:, sl(h)]], axis=0) for d, h in es]
    vh = [vb[d][:, sl(h)] for d, h in es]
    sc = [_dot_nt(lhs[i], rhs[i]) for i in ids]
    zs = [_dot_nt(lhs[i], s0[i]) for i in ids]
    mab = [jnp.where(strict[d], sc[i][:L, :L], 0.0) for i, (d, h) in enumerate(es)]
    mak = [jnp.where(strict[d], sc[i][:L, L:], 0.0) for i, (d, h) in enumerate(es)]
    nrb = [jnp.where(incl[d], sc[i][L:, :L], 0.0) for i, (d, h) in enumerate(es)]
    nrk = [jnp.where(incl[d], sc[i][L:, L:], 0.0) for i, (d, h) in enumerate(es)]
    z0 = [zs[i][:L] + _dot(mak[i], vh[i]) for i in ids]
    tinv = _unit_tri_inverse(mab, blk)
    u = [_dot(tinv[i], z0[i]) for i in ids]
    o1 = [_dot(nrk[i], vh[i]) for i in ids]
    o2 = [_dot(nrb[i], u[i]) for i in ids]
    upd = [_dot_tn(jnp.concatenate([u[i].astype(BF16), vh[i]], axis=0),
                   jnp.concatenate([bl[d][:, sl(h)], kl[d][:, sl(h)]], axis=0)) for i, (d, h) in enumerate(es)]
    for i, (d, h) in enumerate(es):
        (of_ref, ob_ref)[d][0, :, sl(h)] = zs[i][L:] + o1[i] + o2[i]
        s_scr[d, h] = s0[i] * g_tot[d][:, sl(h)] + upd[i]


def _rw_scan(r, lwf, lwb, k, v, kk, a, n_ctx):
    B, S, W = r.shape
    nc, fwd, bwd = _scan_index_maps(S, n_ctx)
    spec = lambda cidx: pl.BlockSpec((1, CHUNK, W), lambda b, s: (b, cidx(s), 0))
    return pl.pallas_call(
        _rw_scan_kernel,
        grid=(B, nc),
        in_specs=[spec(fwd)] * 6 + [spec(bwd)] * 6,
        out_specs=[spec(fwd), spec(bwd)],
        out_shape=[jax.ShapeDtypeStruct((B, S, W), F32)] * 2,
        scratch_shapes=[pltpu.VMEM((2, RW_HEADS, RW_DH, RW_DH), F32)],
        compiler_params=_cp("parallel", "arbitrary"),
        name="rwkv_scan",
    )(r, lwf, k, v, kk, a, r, lwb, k, v, kk, a)


def _even_out_kernel(x_ref, ogf, ogb, gate_ref, orf, orb, bonus_ref, grw_ref, glag, lng, lnb, ones128, ones64,
                     w_ref, gl, gc, o_ref, *, tm, n_ctx):
    og = ogf[0] + ogb[0]
    ms = _split_dot_r(og * og, ones128[...], 2) * (1.0 / GLA_DV)
    gt = gate_ref[0]
    y_gla = og * lax.rsqrt(ms + 1e-6) * glag[...] * (gt * jax.nn.sigmoid(gt))
    of = orf[0] + orb[0]
    o64 = ones64[...]
    mean = _split_dot_r(of, o64, 3) * (1.0 / RW_DH)
    cen = of - mean
    var = _split_dot_r(cen * cen, o64, 2) * (1.0 / RW_DH)
    y_rw = (cen * lax.rsqrt(var + RW_GN_EPS) * lng[...] + lnb[...] + bonus_ref[0]) * grw_ref[0]
    y = jnp.concatenate([y_gla, y_rw], axis=1).astype(BF16)
    rows = pl.program_id(1) * tm + lax.broadcasted_iota(jnp.int32, (tm, 1), 0)
    gate = jnp.where(rows < n_ctx, gc[0], gl[0])
    o_ref[0] = x_ref[0] + gate * jnp.dot(y, w_ref[...], preferred_element_type=F32)


def _even_out(xs, og_f, og_b, z_gla, or_f, or_b, bonus, g_rw, p, mod, n_ctx, *, tm):
    B, S, D = xs.shape
    W = RW_W
    row = lambda width: pl.BlockSpec((1, tm, width), lambda b, j: (b, j, 0))
    full = lambda shape: pl.BlockSpec(shape, lambda b, j: (0,) * len(shape))
    return pl.pallas_call(
        functools.partial(_even_out_kernel, tm=tm, n_ctx=n_ctx),
        grid=(B, S // tm),
        in_specs=[row(D), row(GLA_V), row(GLA_V),
                  pl.BlockSpec((1, tm, GLA_V), lambda b, j: (b, j, (2 * GLA_QK + GLA_V) // GLA_V)),
                  row(W), row(W), row(W), row(W),
                  full((1, GLA_V)), full((1, W)), full((1, W)), full((GLA_V, GLA_V)), full((W, W)),
                  full((GLA_V + W, D)), *_mod_specs(B, D, 2, 2)],
        out_specs=row(D),
        out_shape=jax.ShapeDtypeStruct((B, S, D), F32),
        compiler_params=_cp("parallel", "parallel"),
        name="even_out",
    )(xs, og_f, og_b, z_gla, or_f, or_b, bonus, g_rw, p["gla_g"], p["ln_g"], p["ln_b"], p["ones128"], p["ones64"],
      p["w_out"], mod, mod)


def _odd_prep_kernel(z_ref, cos_ref, sin_ref, qg_ref, kg_ref, ones64, qc_o, kc_o, vc_o, qd_o, kd_o, vd_o):
    z = z_ref[0]
    cos, sin = cos_ref[...], sin_ref[...]
    o64 = ones64[...]
    nq = C_HEADS * HEAD_DIM
    nk = C_KV * HEAD_DIM

    def norm_rope(x, g, width):
        ms = _split_dot_r(x * x, o64[:width, :width], 2) * (1.0 / HEAD_DIM)
        y = x * lax.rsqrt(ms + 1e-6) * g
        lane = lax.broadcasted_iota(jnp.int32, (1, width), 1)
        first = (lane % (HEAD_DIM // 2)) < (HEAD_DIM // 4)
        swapped = jnp.where(first, pltpu.roll(y, width - HEAD_DIM // 4, 1), pltpu.roll(y, HEAD_DIM // 4, 1))
        return y * cos[:, :width] + swapped * sin[:, :width]

    qc_o[0] = (norm_rope(z[:, 0:nq], qg_ref[...], nq) * HEAD_DIM ** -0.5).astype(BF16)
    kc_o[0] = norm_rope(z[:, nq:nq + nk], kg_ref[:, :nk], nk).astype(BF16)
    vc_o[0] = z[:, nq + nk:nq + 2 * nk].astype(BF16)
    base = nq + 2 * nk
    nd = NA_HEADS * HEAD_DIM
    qd_o[0] = (z[:, base:base + nd] * HEAD_DIM ** -0.5).astype(BF16)
    kd_o[0] = z[:, base + nd:base + 2 * nd].astype(BF16)
    vd_o[0] = z[:, base + 2 * nd:base + 3 * nd].astype(BF16)


def _odd_prep(z, cos_t, sin_t, qg, kg, ones64, *, tt):
    B, S, Z = z.shape
    nq, nk, nd = C_HEADS * HEAD_DIM, C_KV * HEAD_DIM, NA_HEADS * HEAD_DIM
    full = lambda shape: pl.BlockSpec(shape, lambda b, j: (0,) * len(shape))
    row = lambda width: pl.BlockSpec((1, tt, width), lambda b, j: (b, j, 0))
    widths = (nq, nk, nk, nd, nd, nd)
    return pl.pallas_call(
        _odd_prep_kernel,
        grid=(B, S // tt),
        in_specs=[row(Z), pl.BlockSpec((tt, nq), lambda b, j: (j, 0)), pl.BlockSpec((tt, nq), lambda b, j: (j, 0)),
                  full((1, nq)), full((1, nq)), full((nq, nq))],
        out_specs=[row(w) for w in widths],
        out_shape=[jax.ShapeDtypeStruct((B, S, w), BF16) for w in widths],
        compiler_params=_cp("parallel", "parallel"),
        name="odd_prep",
    )(z, cos_t, sin_t, qg, kg, ones64)


def _pair_attn_kernel(q_ref, k_ref, v_ref, o_ref, *, shared_kv):
    q = q_ref[0]
    tq = q.shape[0]
    ngroups = q.shape[1] // LANE
    lane = lax.broadcasted_iota(jnp.int32, (1, LANE), 1)
    masks = (lane < HEAD_DIM, lane >= HEAD_DIM)
    zero = jnp.zeros((), q.dtype)
    if shared_kv:
        kp, vp = k_ref[0], v_ref[0]
        for hm in range(2):
            lhs = jnp.concatenate([jnp.where(masks[hm], q[:, g * LANE:(g + 1) * LANE], zero) for g in range(ngroups)],
                                  axis=0)
            s = _dot_nt(lhs, kp)
            p = jnp.exp(s - jnp.max(s, axis=-1, keepdims=True))
            o = _dot(p, vp) / jnp.sum(p, axis=-1, keepdims=True)
            for g in range(ngroups):
                blk = o[g * tq:(g + 1) * tq]
                if hm == 0:
                    o_ref[0, :, g * LANE:(g + 1) * LANE] = blk.astype(o_ref.dtype)
                else:
                    prev = o_ref[0, :, g * LANE:(g + 1) * LANE]
                    o_ref[0, :, g * LANE:(g + 1) * LANE] = jnp.where(masks[0], prev, blk.astype(o_ref.dtype))
    else:
        for g in range(ngroups):
            qp = q[:, g * LANE:(g + 1) * LANE]
            kp = k_ref[0, :, g * LANE:(g + 1) * LANE]
            vp = v_ref[0, :, g * LANE:(g + 1) * LANE]
            lhs = jnp.concatenate([jnp.where(masks[0], qp, zero), jnp.where(masks[1], qp, zero)], axis=0)
            s = _dot_nt(lhs, kp)
            p = jnp.exp(s - jnp.max(s, axis=-1, keepdims=True))
            o = _dot(p, vp) / jnp.sum(p, axis=-1, keepdims=True)
            o_ref[0, :, g * LANE:(g + 1) * LANE] = jnp.where(masks[0], o[:tq], o[tq:]).astype(o_ref.dtype)


def _pair_attention(q, k, v, *, q_row0, n_q, n_k, tq, shared_kv, name):
    B, S, QW = q.shape
    KW = k.shape[2]
    return pl.pallas_call(
        functools.partial(_pair_attn_kernel, shared_kv=shared_kv),
        grid=(B, n_q // tq),
        in_specs=[pl.BlockSpec((1, tq, QW), lambda b, i: (b, q_row0 // tq + i, 0)),
                  pl.BlockSpec((1, n_k, KW), lambda b, i: (b, 0, 0)),
                  pl.BlockSpec((1, n_k, KW), lambda b, i: (b, 0, 0))],
        out_specs=pl.BlockSpec((1, tq, QW), lambda b, i: (b, i, 0)),
        out_shape=jax.ShapeDtypeStruct((B, n_q, QW), BF16),
        compiler_params=_cp("parallel", "parallel"),
        name=name,
    )(q, k, v)


def _na_kernel(q_ref, k_ref, v_ref, bias_ref, o_ref, *, n_ctx, n_rows, kh):
    r = pl.program_id(1)
    rs = jnp.clip(r - kh // 2, 0, n_rows - kh)
    start = pl.multiple_of(n_ctx + rs * GRID_W, GRID_W)
    nkeys = kh * GRID_W
    q = q_ref[0]
    lane = lax.broadcasted_iota(jnp.int32, (1, LANE), 1)
    m0 = lane < HEAD_DIM
    zero = jnp.zeros((), q.dtype)
    for g in range(NA_HEADS // 2):
        gs = slice(g * LANE, (g + 1) * LANE)
        qp = q[:, gs]
        lhs = jnp.concatenate([jnp.where(m0, qp, zero), jnp.where(m0, zero, qp)], axis=0)
        k_nb = k_ref[0, pl.ds(start, nkeys), gs]
        v_nb = v_ref[0, pl.ds(start, nkeys), gs]
        k_cx = k_ref[0, 0:n_ctx, gs]
        v_cx = v_ref[0, 0:n_ctx, gs]
        bias = jnp.concatenate([bias_ref[0, 2 * g], bias_ref[0, 2 * g + 1]], axis=0)
        s_nb = _dot_nt(lhs, k_nb) + bias
        s_cx = _dot_nt(lhs, k_cx)
        m = jnp.maximum(jnp.max(s_nb, axis=-1, keepdims=True), jnp.max(s_cx, axis=-1, keepdims=True))
        p_nb = jnp.exp(s_nb - m)
        p_cx = jnp.exp(s_cx - m)
        den = jnp.sum(p_nb, axis=-1, keepdims=True) + jnp.sum(p_cx, axis=-1, keepdims=True)
        o = (_dot(p_nb, v_nb) + _dot(p_cx, v_cx)) / den
        o_ref[0, :, gs] = jnp.where(m0, o[:GRID_W], o[GRID_W:]).astype(o_ref.dtype)


def _na_attention(qd, kd, vd, bias_tab, n_ctx):
    B, S, W = qd.shape
    T = S - n_ctx
    n_rows = T // GRID_W
    kh = min(NA_KH, n_rows)
    nkeys = kh * GRID_W

    def bias_idx(b, r):
        return (jnp.clip(r - kh // 2, 0, n_rows - kh) - r + kh - 1, 0, 0, 0)

    return pl.pallas_call(
        functools.partial(_na_kernel, n_ctx=n_ctx, n_rows=n_rows, kh=kh),
        grid=(B, n_rows),
        in_specs=[pl.BlockSpec((1, GRID_W, W), lambda b, r: (b, n_ctx // GRID_W + r, 0)),
                  pl.BlockSpec((1, S, W), lambda b, r: (b, 0, 0)),
                  pl.BlockSpec((1, S, W), lambda b, r: (b, 0, 0)),
                  pl.BlockSpec((1, NA_HEADS, GRID_W, nkeys), bias_idx)],
        out_specs=pl.BlockSpec((1, GRID_W, W), lambda b, r: (b, r, 0)),
        out_shape=jax.ShapeDtypeStruct((B, T, W), BF16),
        compiler_params=_cp("parallel", "arbitrary"),
        name="na_attention",
    )(qd, kd, vd, bias_tab)


def _na_bias_table(rpb, n_rows):
    kh = min(NA_KH, n_rows)
    kw = min(NA_KW, GRID_W)
    col = np.arange(GRID_W)
    start = np.clip(col - kw // 2, 0, GRID_W - kw)
    in_win = (col[None, :] >= start[:, None]) & (col[None, :] < start[:, None] + kw)
    dc = np.clip(col[None, :] - col[:, None], -(NA_KW - 1), NA_KW - 1) + NA_KW - 1
    bias_cols = rpb[:, :, dc]
    slabs = []
    for st in range(kh):
        sl = bias_cols[:, st + NA_KH - kh:st + NA_KH - kh + kh]
        sl = jnp.where(in_win[None, None], sl, NEG).transpose(0, 2, 1, 3)
        slabs.append(sl.reshape(NA_HEADS, GRID_W, kh * GRID_W))
    return jnp.stack(slabs).astype(F32)


def _proj_res_kernel(x_ref, y_ref, w_ref, gl, gc, o_ref, *, tm, n_ctx):
    rows = pl.program_id(1) * tm + lax.broadcasted_iota(jnp.int32, (tm, 1), 0)
    gate = jnp.where(rows < n_ctx, gc[0], gl[0])
    o_ref[0] = x_ref[0] + gate * jnp.dot(y_ref[0], w_ref[...], preferred_element_type=F32)


def _proj_res(xs, y, w, mod, n_ctx, *, tm):
    B, S, D = xs.shape
    K = y.shape[2]
    row = lambda width: pl.BlockSpec((1, tm, width), lambda b, j: (b, j, 0))
    return pl.pallas_call(
        functools.partial(_proj_res_kernel, tm=tm, n_ctx=n_ctx),
        grid=(B, S // tm),
        in_specs=[row(D), row(K), pl.BlockSpec((K, D), lambda b, j: (0, 0)), *_mod_specs(B, D, 2, 2)],
        out_specs=row(D),
        out_shape=jax.ShapeDtypeStruct((B, S, D), F32),
        compiler_params=_cp("parallel", "parallel"),
        name="proj_res",
    )(xs, y, w, mod, mod)


def _final_norm_kernel(x_ref, g_ref, o_ref):
    x = x_ref[0]
    o_ref[0] = x * lax.rsqrt(jnp.mean(x * x, axis=-1, keepdims=True) + 1e-6) * g_ref[...]


def _final_norm(xs, g, n_ctx, *, tm):
    B, S, D = xs.shape
    T = S - n_ctx
    return pl.pallas_call(
        _final_norm_kernel,
        grid=(B, T // tm),
        in_specs=[pl.BlockSpec((1, tm, D), lambda b, j: (b, n_ctx // tm + j, 0)),
                  pl.BlockSpec((1, D), lambda b, j: (0, 0))],
        out_specs=pl.BlockSpec((1, tm, D), lambda b, j: (b, j, 0)),
        out_shape=jax.ShapeDtypeStruct((B, T, D), F32),
        compiler_params=_cp("parallel", "parallel"),
        name="final_norm",
    )(xs, g.reshape(1, D))


def _block_ones(width, block):
    idx = np.arange(width) // block
    return jnp.asarray(idx[:, None] == idx[None, :], BF16)


def _pad_cols(w, width):
    return jnp.pad(w, ((0, 0), (0, width - w.shape[1])))


def _pad_rows(w, height):
    return jnp.pad(w, ((0, height - w.shape[0]), (0, 0)))


def _even_params(j, ev_w_in, ev_w_out, gla_a_up, gla_a_bias, gla_norm_g, rw_mu, rw_w0, rw_w_up, rw_a0, rw_a_up,
                 rw_g_up, rw_k_k, rw_k_a, rw_r_k, rw_ln_g, rw_ln_b):
    w_in = ev_w_in[j]
    D = w_in.shape[0]
    gq = 2 * GLA_QK + 2 * GLA_V
    w_gla = jnp.concatenate([w_in[:, :gq], _pad_cols(w_in[:, gq:gq + GLA_RANK], LANE),
                             _pad_cols(w_in[:, gq + GLA_RANK:gq + 2 * GLA_RANK], LANE)], axis=1)
    o = gq + 2 * GLA_RANK
    sizes = (3 * RW_W, RW_DECAY_RANK, RW_DECAY_RANK, RW_A_RANK, RW_G_RANK)
    offs = np.cumsum((0,) + sizes)
    pieces = [w_in[:, o:o + 3 * RW_W]] + [_pad_cols(w_in[:, o + offs[i]:o + offs[i + 1]], LANE) for i in range(1, 5)]
    w_rw = jnp.concatenate(pieces, axis=1)
    mu = rw_mu[j]
    mu_pieces = [mu[:, :3 * RW_W]] + [_pad_cols(mu[:, offs[i]:offs[i + 1]], LANE) for i in range(1, 5)]
    return {
        "w_gla": w_gla.astype(BF16), "w_rw": w_rw.astype(BF16), "w_out": ev_w_out[j].astype(BF16),
        "gla_aup": jnp.stack([_pad_rows(gla_a_up[j, d], LANE) for d in range(2)]).astype(BF16),
        "gla_bias": gla_a_bias[j], "gla_g": jnp.tile(gla_norm_g[j], GLA_HEADS).reshape(1, GLA_V),
        "mu": jnp.concatenate(mu_pieces, axis=1), "w0": rw_w0[j],
        "wup": jnp.stack([_pad_rows(rw_w_up[j, d], LANE) for d in range(2)]).astype(BF16),
        "a0": rw_a0[j].reshape(1, RW_W), "aup": _pad_rows(rw_a_up[j], LANE).astype(BF16),
        "gup": _pad_rows(rw_g_up[j], LANE).astype(BF16),
        "k_k": rw_k_k[j].reshape(1, RW_W), "k_a": rw_k_a[j].reshape(1, RW_W), "r_k": rw_r_k[j].reshape(1, RW_W),
        "ln_g": rw_ln_g[j].reshape(1, RW_W), "ln_b": rw_ln_b[j].reshape(1, RW_W),
        "ones64": _block_ones(RW_W, RW_DH), "ones128": _block_ones(GLA_V, GLA_DV),
    }


def _gqa_head_order():
    per_kv = C_HEADS // C_KV
    order = []
    for jj in range(per_kv):
        for kv in range(C_KV):
            h = kv * per_kv + jj
            order.extend(range(h * HEAD_DIM, (h + 1) * HEAD_DIM))
    return np.asarray(order)


def _rope_tables(n_ctx, T, width):
    t = jnp.arange(T)
    pos = jnp.stack([t // GRID_W, t % GRID_W], axis=-1).astype(F32)
    half = HEAD_DIM // 2
    inv = ROPE_THETA ** (-jnp.arange(0, half, 2, dtype=F32) / half)
    ang = pos[:, :, None] * inv
    cos, sin = jnp.cos(ang), jnp.sin(ang)
    cos_h = jnp.concatenate([cos, cos], axis=-1).reshape(T, HEAD_DIM)
    sin_h = jnp.concatenate([-sin, sin], axis=-1).reshape(T, HEAD_DIM)
    cos_h = jnp.concatenate([jnp.ones((n_ctx, HEAD_DIM), F32), cos_h], axis=0)
    sin_h = jnp.concatenate([jnp.zeros((n_ctx, HEAD_DIM), F32), sin_h], axis=0)
    reps = width // HEAD_DIM
    return jnp.tile(cos_h, (1, reps)), jnp.tile(sin_h, (1, reps))


def _even_layer(xs, mod, n_ctx, norm1_g, p):
    tm = _row_tile(xs.shape[1], ROW_CAP_MATMUL)
    z_gla = _norm_mod_matmul(xs, norm1_g, mod, p["w_gla"], n_ctx, tm=tm, tn=GLA_Z // 2, name="even_in_gla")
    z_rw = _norm_mod_matmul(xs, norm1_g, mod, p["w_rw"], n_ctx, tm=tm, tn=RW_Z // 2, name="even_in_rw")
    og_f, og_b = _gla_scan(z_gla, p["gla_aup"], p["gla_bias"], n_ctx)
    r, lwf, lwb, k, v, kk, a, g_rw, bonus = _rw_features(z_rw, p, n_ctx)
    or_f, or_b = _rw_scan(r, lwf, lwb, k, v, kk, a, n_ctx)
    return _even_out(xs, og_f, og_b, z_gla, or_f, or_b, bonus, g_rw, p, mod, n_ctx,
                     tm=_row_tile(xs.shape[1], ROW_CAP_ELEMENTWISE))


def _odd_layer(xs, mod, n_ctx, norm1_g, p, need_ctx):
    B, S, D = xs.shape
    T = S - n_ctx
    z = _norm_mod_matmul(xs, norm1_g, mod, p["w_in"], n_ctx, tm=_row_tile(S, ROW_CAP_MATMUL),
                         tn=p["w_in"].shape[1] // 2, name="odd_in")
    qc, kc, vc, qd, kd, vd = _odd_prep(z, p["cos"], p["sin"], p["qg"], p["kg"], p["ones64"],
                                       tt=_row_tile(S, ROW_CAP_ELEMENTWISE))
    y_gqa = _pair_attention(qc, kc, vc, q_row0=n_ctx, n_q=T, n_k=S, tq=128, shared_kv=True, name="gqa")
    y_na = _na_attention(qd, kd, vd, p["bias_tab"], n_ctx)
    parts_g, parts_n = [y_gqa], [y_na]
    if need_ctx:
        parts_g.insert(0, _pair_attention(qc, kc, vc, q_row0=0, n_q=n_ctx, n_k=n_ctx, tq=n_ctx, shared_kv=True,
                                          name="gqa_ctx"))
        parts_n.insert(0, _pair_attention(qd, kd, vd, q_row0=0, n_q=n_ctx, n_k=n_ctx, tq=n_ctx, shared_kv=False,
                                          name="na_ctx"))
    else:
        zeros = jnp.zeros((B, n_ctx, y_gqa.shape[2]), BF16)
        parts_g.insert(0, zeros)
        parts_n.insert(0, zeros)
    y = jnp.concatenate([jnp.concatenate(parts_g, axis=1), jnp.concatenate(parts_n, axis=1)], axis=2)
    return _proj_res(xs, y, p["w_out"], mod, n_ctx, tm=_row_tile(S, ROW_CAP_MATMUL))


def kernel(x, c, ctx, c_ctx, w_mod, b_mod, norm1_g, norm2_g, ffn_w13, ffn_w2, ev_w_in, ev_w_out, gla_a_up,
           gla_a_bias, gla_norm_g, rw_mu, rw_w0, rw_w_up, rw_a0, rw_a_up, rw_g_up, rw_k_k, rw_k_a, rw_r_k, rw_ln_g,
           rw_ln_b, od_w_in, od_w_out, cq_norm_g, ck_norm_g, na_rpb, final_g):
    B, T, D = x.shape
    n_ctx = ctx.shape[1]
    S = n_ctx + T
    xs = jnp.concatenate([ctx, x], axis=1)
    mods = _mod_vectors(c, c_ctx, w_mod, b_mod)
    nq, nk = C_HEADS * HEAD_DIM, C_KV * HEAD_DIM
    order = _gqa_head_order()
    cos_t, sin_t = _rope_tables(n_ctx, T, nq)
    ones64 = _block_ones(nq, HEAD_DIM)
    for i in range(DEPTH):
        j = i // 2
        mod = mods[i]
        if i % 2 == 0:
            p = _even_params(j, ev_w_in, ev_w_out, gla_a_up, gla_a_bias, gla_norm_g, rw_mu, rw_w0, rw_w_up, rw_a0,
                             rw_a_up, rw_g_up, rw_k_k, rw_k_a, rw_r_k, rw_ln_g, rw_ln_b)
            xs = _even_layer(xs, mod, n_ctx, norm1_g[i], p)
        else:
            w_in = od_w_in[j]
            w_out = od_w_out[j]
            p = {
                "w_in": jnp.concatenate([w_in[:, :nq][:, order], w_in[:, nq:]], axis=1).astype(BF16),
                "w_out": jnp.concatenate([w_out[:nq][order], w_out[nq:]], axis=0).astype(BF16),
                "cos": cos_t, "sin": sin_t, "ones64": ones64,
                "qg": jnp.tile(cq_norm_g[j], C_HEADS).reshape(1, nq),
                "kg": jnp.tile(ck_norm_g[j], C_HEADS).reshape(1, nq),
                "bias_tab": _na_bias_table(na_rpb[j], T // GRID_W),
            }
            xs = _odd_layer(xs, mod, n_ctx, norm1_g[i], p, need_ctx=i < DEPTH - 1)
        xs = _ffn(xs, norm2_g[i], mod, ffn_w13[i].astype(BF16), ffn_w2[i].astype(BF16), n_ctx,
                  tm=_row_tile(S, ROW_CAP_MATMUL), tf=FFN_COL_TILE)
    return _final_norm(xs, final_g, n_ctx, tm=_row_tile(n_ctx, ROW_CAP_ELEMENTWISE))
```

```python
import functools

import jax
import jax.numpy as jnp
import numpy as np
from jax import lax
from jax.experimental import pallas as pl
from jax.experimental.pallas import tpu as pltpu

F32 = jnp.float32
BF16 = jnp.bfloat16

DEPTH = 4
GRID_W = 64
CHUNK = 64
LANE = 128

GLA_HEADS, GLA_DK, GLA_DV, GLA_RANK, GLA_TAU = 4, 64, 128, 16, 16.0
GLA_QK, GLA_V = GLA_HEADS * GLA_DK, GLA_HEADS * GLA_DV
GLA_Z = 2 * GLA_QK + 2 * GLA_V + 2 * LANE

RW_HEADS, RW_DH, RW_W = 8, 64, 512
RW_DECAY_RANK, RW_A_RANK, RW_G_RANK = 32, 32, 96
RW_GN_EPS = 64e-5
RW_Z = 3 * RW_W + 4 * LANE

HEAD_DIM, C_HEADS, C_KV, NA_HEADS, NA_KH, NA_KW = 64, 8, 2, 8, 8, 16
ROPE_THETA = 10000.0
NEG = -1e30

VMEM_LIMIT = 56 * 1024 * 1024
ROW_CAP_MATMUL = 1152
ROW_CAP_FFN = 576
ROW_CAP_ELEMENTWISE = 256
FFN_COL_TILE = 256


def _row_tile(n_rows, cap):
    return max(d for d in range(8, cap + 1, 8) if n_rows % d == 0)


def _cp(*sem):
    return pltpu.CompilerParams(dimension_semantics=sem, vmem_limit_bytes=VMEM_LIMIT)


def _dot(a, b):
    return jnp.dot(a.astype(BF16), b.astype(BF16), preferred_element_type=F32)


def _dot_nt(a, b):
    return lax.dot_general(a.astype(BF16), b.astype(BF16), (((1,), (1,)), ((), ())), preferred_element_type=F32)


def _dot_tn(a, b):
    return lax.dot_general(a.astype(BF16), b.astype(BF16), (((0,), (0,)), ((), ())), preferred_element_type=F32)


def _split_dot(m, x, parts):
    acc = None
    rem = x
    for _ in range(parts):
        piece = rem.astype(BF16)
        rem = rem - piece.astype(F32)
        t = jnp.dot(m, piece, preferred_element_type=F32)
        acc = t if acc is None else acc + t
    return acc


def _split_dot_r(x, m, parts):
    acc = None
    rem = x
    for _ in range(parts):
        piece = rem.astype(BF16)
        rem = rem - piece.astype(F32)
        t = jnp.dot(piece, m, preferred_element_type=F32)
        acc = t if acc is None else acc + t
    return acc


def _softplus(y):
    return jnp.maximum(y, 0.0) + jnp.log1p(jnp.exp(-jnp.abs(y)))


def _rms_mod(x, g, sc_l, sh_l, sc_c, sh_c, row0, n_ctx):
    y = x * lax.rsqrt(jnp.mean(x * x, axis=-1, keepdims=True) + 1e-6) * g
    rows = row0 + lax.broadcasted_iota(jnp.int32, (x.shape[0], 1), 0)
    is_ctx = rows < n_ctx
    sc = jnp.where(is_ctx, sc_c, sc_l)
    sh = jnp.where(is_ctx, sh_c, sh_l)
    return y * (1.0 + sc) + sh


def _tri_masks(n, reverse):
    ri = lax.broadcasted_iota(jnp.int32, (n, n), 0)
    ci = lax.broadcasted_iota(jnp.int32, (n, n), 1)
    if reverse:
        return ci >= ri, ci > ri
    return ci <= ri, ci < ri


INV_BASE = 16


def _block_ids(n):
    ri = lax.broadcasted_iota(jnp.int32, (n, n), 0)
    ci = lax.broadcasted_iota(jnp.int32, (n, n), 1)
    return ri, ci


def _unit_tri_inverse(a, blk):
    ri, ci = blk
    n = a[0].shape[0]
    ids = range(len(a))
    same = lambda bs: (ri // bs) == (ci // bs)
    base = same(INV_BASE)
    eye = (ri == ci).astype(F32)
    p = [jnp.where(base, x, 0.0) for x in a]
    t = [eye + x for x in p]
    m = 1
    while 2 * m < INV_BASE:
        p = [_dot(x, x) for x in p]
        tp = [_dot(t[i], p[i]) for i in ids]
        t = [t[i] + tp[i] for i in ids]
        m *= 2
    bs = INV_BASE
    while bs < n:
        mask = same(2 * bs) & jnp.logical_not(same(bs))
        ta = [_dot(t[i], jnp.where(mask, a[i], 0.0)) for i in ids]
        tat = [_dot(ta[i], t[i]) for i in ids]
        t = [t[i] + tat[i] for i in ids]
        bs *= 2
    return t


def _mod_kernel(s_ref, w_ref, b_ref, o_ref):
    s = s_ref[...]
    s = s * jax.nn.sigmoid(s)
    o_ref[0] = _dot(s, w_ref[0]) + b_ref[0]


def _mod_vectors(c, c_ctx, w_mod, b_mod):
    B, D = c.shape
    bp = -(-(B + 1) // 8) * 8
    s_in = jnp.zeros((bp, D), F32).at[:B].set(c).at[B].set(c_ctx)
    n6 = w_mod.shape[-1]
    tn = 1536
    out = pl.pallas_call(
        _mod_kernel,
        grid=(DEPTH, n6 // tn),
        in_specs=[pl.BlockSpec((bp, D), lambda l, n: (0, 0)),
                  pl.BlockSpec((1, D, tn), lambda l, n: (l, 0, n)),
                  pl.BlockSpec((1, 1, tn), lambda l, n: (l, 0, n))],
        out_specs=pl.BlockSpec((1, bp, tn), lambda l, n: (l, 0, n)),
        out_shape=jax.ShapeDtypeStruct((DEPTH, bp, n6), F32),
        compiler_params=_cp("parallel", "parallel"),
        name="mod_vectors",
    )(s_in, w_mod, b_mod.reshape(DEPTH, 1, n6))
    return out.reshape(DEPTH, bp, 1, n6)


def _mod_specs(B, D, idx, grid_rank):
    if grid_rank == 2:
        return [pl.BlockSpec((1, 1, D), lambda b, j: (b, 0, idx)),
                pl.BlockSpec((1, 1, D), lambda b, j: (B, 0, idx))]
    return [pl.BlockSpec((1, 1, D), lambda b, j, n: (b, 0, idx)),
            pl.BlockSpec((1, 1, D), lambda b, j, n: (B, 0, idx))]


def _nmm_kernel(x_ref, g_ref, shl, shc, scl, scc, w_ref, o_ref, h_scr, *, tm, n_ctx):
    @pl.when(pl.program_id(2) == 0)
    def _():
        h = _rms_mod(x_ref[0], g_ref[...], scl[0], shl[0], scc[0], shc[0], pl.program_id(1) * tm, n_ctx)
        h_scr[...] = h.astype(BF16)

    o_ref[0] = jnp.dot(h_scr[...], w_ref[...], preferred_element_type=F32).astype(o_ref.dtype)


def _norm_mod_matmul(xs, g, mod, w, n_ctx, *, tm, tn, name):
    B, S, D = xs.shape
    N = w.shape[1]
    nB = B
    return pl.pallas_call(
        functools.partial(_nmm_kernel, tm=tm, n_ctx=n_ctx),
        grid=(B, S // tm, N // tn),
        in_specs=[pl.BlockSpec((1, tm, D), lambda b, j, n: (b, j, 0)),
                  pl.BlockSpec((1, D), lambda b, j, n: (0, 0)),
                  *_mod_specs(nB, D, 0, 3), *_mod_specs(nB, D, 1, 3),
                  pl.BlockSpec((D, tn), lambda b, j, n: (0, n))],
        out_specs=pl.BlockSpec((1, tm, tn), lambda b, j, n: (b, j, n)),
        out_shape=jax.ShapeDtypeStruct((B, S, N), F32),
        scratch_shapes=[pltpu.VMEM((tm, D), BF16)],
        compiler_params=_cp("parallel", "parallel", "arbitrary"),
        name=name,
    )(xs, g.reshape(1, D), mod, mod, mod, mod, w)


def _ffn_kernel(x_ref, g_ref, shl, shc, scl, scc, gl, gc, w13_ref, w2_ref, o_ref, *, tm, tf, n_ctx):
    x = x_ref[0]
    h = _rms_mod(x, g_ref[...], scl[0], shl[0], scc[0], shc[0], pl.program_id(1) * tm, n_ctx).astype(BF16)
    n_hidden = w2_ref.shape[0]
    acc = None
    for f in range(n_hidden // tf):
        a = jnp.dot(h, w13_ref[:, f * tf:(f + 1) * tf], preferred_element_type=F32)
        b = jnp.dot(h, w13_ref[:, n_hidden + f * tf:n_hidden + (f + 1) * tf], preferred_element_type=F32)
        u = (a * jax.nn.sigmoid(a) * b).astype(BF16)
        t = jnp.dot(u, w2_ref[f * tf:(f + 1) * tf, :], preferred_element_type=F32)
        acc = t if acc is None else acc + t
    rows = pl.program_id(1) * tm + lax.broadcasted_iota(jnp.int32, (tm, 1), 0)
    gate = jnp.where(rows < n_ctx, gc[0], gl[0])
    o_ref[0] = x + gate * acc


def _ffn(xs, g, mod, w13, w2, n_ctx, *, tm, tf):
    B, S, D = xs.shape
    resident = lambda shape: pl.BlockSpec(shape, lambda b, j: (0, 0), pipeline_mode=pl.Buffered(1))
    return pl.pallas_call(
        functools.partial(_ffn_kernel, tm=tm, tf=tf, n_ctx=n_ctx),
        grid=(B, S // tm),
        in_specs=[pl.BlockSpec((1, tm, D), lambda b, j: (b, j, 0)),
                  pl.BlockSpec((1, D), lambda b, j: (0, 0)),
                  *_mod_specs(B, D, 3, 2), *_mod_specs(B, D, 4, 2), *_mod_specs(B, D, 5, 2),
                  resident(w13.shape), resident(w2.shape)],
        out_specs=pl.BlockSpec((1, tm, D), lambda b, j: (b, j, 0)),
        out_shape=jax.ShapeDtypeStruct((B, S, D), F32),
        compiler_params=_cp("parallel", "parallel"),
        name="ffn",
    )(xs, g.reshape(1, D), mod, mod, mod, mod, mod, mod, w13, w2)


def _chunk_index(s, nc_ctx, nc, reverse):
    if not reverse:
        return s
    return jnp.where(s < nc_ctx, nc_ctx - 1 - s, nc + nc_ctx - 1 - s)


def _gla_kernel(qkv_f, ad_f, qkv_b, ad_b, aup_ref, bias_ref, of_ref, ob_ref, st_scr):
    @pl.when(pl.program_id(1) == 0)
    def _():
        st_scr[...] = jnp.zeros_like(st_scr)

    L = CHUNK
    qb, kb, kt, gt, v, incl = [], [], [], [], [], []
    for d, (qkv_ref, ad_ref) in enumerate(((qkv_f, ad_f), (qkv_b, ad_b))):
        m, _ = _tri_masks(L, d == 1)
        ad = ad_ref[0][:, d * LANE:(d + 1) * LANE]
        y = _dot(ad, aup_ref[d]) + bias_ref[d:d + 1, :]
        la = (jnp.minimum(y, 0.0) - jnp.log1p(jnp.exp(-jnp.abs(y)))) * (1.0 / GLA_TAU)
        cs = _split_dot(m.astype(BF16), la, 3)
        tot = cs[0:1] if d == 1 else cs[L - 1:L]
        qkv = qkv_ref[0]
        q = qkv[:, 0:GLA_QK] * GLA_DK ** -0.5
        k = qkv[:, GLA_QK:2 * GLA_QK]
        incl.append(m)
        v.append(qkv[:, 2 * GLA_QK:2 * GLA_QK + GLA_V].astype(BF16))
        qb.append((q * jnp.exp(cs)).astype(BF16))
        kb.append((k * jnp.exp(-cs)).astype(BF16))
        kt.append((k * jnp.exp(tot - cs)).astype(BF16))
        gt.append(jnp.exp(tot))
    es = [(d, h) for d in range(2) for h in range(GLA_HEADS)]
    ks = lambda h: slice(h * GLA_DK, (h + 1) * GLA_DK)
    vs = lambda h: slice(h * GLA_DV, (h + 1) * GLA_DV)
    st = [st_scr[d, h] for d, h in es]
    att = [jnp.where(incl[d], _dot_nt(qb[d][:, ks(h)], kb[d][:, ks(h)]), 0.0) for d, h in es]
    o_state = [_dot_nt(qb[d][:, ks(h)], st[i]) for i, (d, h) in enumerate(es)]
    o_att = [_dot(att[i], v[d][:, vs(h)]) for i, (d, h) in enumerate(es)]
    upd = [_dot_tn(v[d][:, vs(h)], kt[d][:, ks(h)]) for d, h in es]
    for i, (d, h) in enumerate(es):
        (of_ref, ob_ref)[d][0, :, vs(h)] = o_att[i] + o_state[i]
        st_scr[d, h] = st[i] * gt[d][:, ks(h)] + upd[i]


def _scan_index_maps(S, n_ctx):
    nc, nc_ctx = S // CHUNK, n_ctx // CHUNK
    fwd = functools.partial(_chunk_index, nc_ctx=nc_ctx, nc=nc, reverse=False)
    bwd = functools.partial(_chunk_index, nc_ctx=nc_ctx, nc=nc, reverse=True)
    return nc, fwd, bwd


def _gla_scan(z_gla, aup_pad, bias, n_ctx):
    B, S, _ = z_gla.shape
    nc, fwd, bwd = _scan_index_maps(S, n_ctx)
    ad_blk = (2 * GLA_QK + 2 * GLA_V) // (2 * LANE)
    qkv_spec = lambda cidx: pl.BlockSpec((1, CHUNK, 2 * GLA_QK + GLA_V), lambda b, s: (b, cidx(s), 0))
    ad_spec = lambda cidx: pl.BlockSpec((1, CHUNK, 2 * LANE), lambda b, s: (b, cidx(s), ad_blk))
    out_spec = lambda cidx: pl.BlockSpec((1, CHUNK, GLA_V), lambda b, s: (b, cidx(s), 0))
    return pl.pallas_call(
        _gla_kernel,
        grid=(B, nc),
        in_specs=[qkv_spec(fwd), ad_spec(fwd), qkv_spec(bwd), ad_spec(bwd),
                  pl.BlockSpec((2, LANE, GLA_QK), lambda b, s: (0, 0, 0)),
                  pl.BlockSpec((2, GLA_QK), lambda b, s: (0, 0))],
        out_specs=[out_spec(fwd), out_spec(bwd)],
        out_shape=[jax.ShapeDtypeStruct((B, S, GLA_V), F32)] * 2,
        scratch_shapes=[pltpu.VMEM((2, GLA_HEADS, GLA_DV, GLA_DK), F32)],
        compiler_params=_cp("parallel", "arbitrary"),
        name="gla_scan",
    )(z_gla, z_gla, z_gla, z_gla, aup_pad, bias)


def _rw_feat_kernel(z_ref, zp_ref, zn_ref, mu_ref, w0_ref, wup_ref, a0_ref, aup_ref, gup_ref, kk_ref, ka_ref,
                    rk_ref, ones_ref, r_o, lwf_o, lwb_o, k_o, v_o, kk_o, a_o, g_o, bonus_o, *, tt):
    j = pl.program_id(1)
    last = pl.num_programs(1) - 1
    z = z_ref[0]
    prev_row = jnp.where(j <= 1, 0.0, zp_ref[0, 7:8, :])
    next_row = jnp.where((j == 0) | (j == last), 0.0, zn_ref[0, 0:1, :])
    rid = lax.broadcasted_iota(jnp.int32, (tt, 1), 0)
    prev = jnp.where(rid == 0, prev_row, pltpu.roll(z, 1, 0))
    nxt = jnp.where(rid == tt - 1, next_row, pltpu.roll(z, tt - 1, 0))
    zs = z + mu_ref[0:1, :] * (prev - z) + mu_ref[1:2, :] * (nxt - z)

    W = RW_W
    r, k, v = zs[:, 0:W], zs[:, W:2 * W], zs[:, 2 * W:3 * W]
    base = 3 * W
    wd = (zs[:, base:base + LANE], zs[:, base + LANE:base + 2 * LANE])
    ad = zs[:, base + 2 * LANE:base + 3 * LANE]
    gd = zs[:, base + 3 * LANE:base + 4 * LANE]
    for d, lw_o in enumerate((lwf_o, lwb_o)):
        wl = -_softplus(-(w0_ref[d:d + 1, :] + _dot(jnp.tanh(wd[d]), wup_ref[d]))) - 0.5
        lw_o[0] = -jnp.exp(wl)
    a = jax.nn.sigmoid(a0_ref[...] + _dot(ad, aup_ref[...]))
    g = _dot(jax.nn.sigmoid(gd), gup_ref[...])
    ones = ones_ref[...]
    kk = k * kk_ref[...]
    ss = _split_dot_r(kk * kk, ones, 2)
    kk = kk / jnp.maximum(jnp.sqrt(ss), 1e-12)
    kmod = k * (1.0 + (a - 1.0) * ka_ref[...])
    bonus = _split_dot_r(r * kmod * rk_ref[...], ones, 2) * v
    r_o[0] = r
    k_o[0] = kmod
    v_o[0] = v
    kk_o[0] = kk
    a_o[0] = a
    g_o[0] = g
    bonus_o[0] = bonus


def _rw_features(z_rw, p, n_ctx):
    B, S, Z = z_rw.shape
    tt = n_ctx
    nt = S // tt
    n8 = S // 8
    W = RW_W
    full = lambda shape: pl.BlockSpec(shape, lambda b, j: (0,) * len(shape))
    out_spec = pl.BlockSpec((1, tt, W), lambda b, j: (b, j, 0))
    return pl.pallas_call(
        functools.partial(_rw_feat_kernel, tt=tt),
        grid=(B, nt),
        in_specs=[pl.BlockSpec((1, tt, Z), lambda b, j: (b, j, 0)),
                  pl.BlockSpec((1, 8, Z), lambda b, j: (b, jnp.maximum(j * (tt // 8) - 1, 0), 0)),
                  pl.BlockSpec((1, 8, Z), lambda b, j: (b, jnp.minimum((j + 1) * (tt // 8), n8 - 1), 0)),
                  full((2, Z)), full((2, W)), full((2, LANE, W)), full((1, W)), full((LANE, W)), full((LANE, W)),
                  full((1, W)), full((1, W)), full((1, W)), full((W, W))],
        out_specs=[out_spec] * 9,
        out_shape=[jax.ShapeDtypeStruct((B, S, W), F32)] * 9,
        compiler_params=_cp("parallel", "parallel"),
        name="rwkv_features",
    )(z_rw, z_rw, z_rw, p["mu"], p["w0"], p["wup"], p["a0"], p["aup"], p["gup"], p["k_k"], p["k_a"], p["r_k"],
      p["ones64"])


def _rw_scan_kernel(r_f, lw_f, k_f, v_f, kk_f, a_f, r_b, lw_b, k_b, v_b, kk_b, a_b, of_ref, ob_ref, s_scr):
    @pl.when(pl.program_id(1) == 0)
    def _():
        s_scr[...] = jnp.zeros_like(s_scr)

    L = CHUNK
    blk = _block_ids(L)
    incl, strict, at, rt, bh, kh, bl, kl, vb, g_tot = ([] for _ in range(10))
    for d, (r_ref, lw_ref, k_ref, v_ref, kk_ref, a_ref) in enumerate(((r_f, lw_f, k_f, v_f, kk_f, a_f),
                                                                     (r_b, lw_b, k_b, v_b, kk_b, a_b))):
        m_incl, m_strict = _tri_masks(L, d == 1)
        lw = lw_ref[0]
        cs = _split_dot(m_incl.astype(BF16), lw, 3)
        tot = cs[0:1] if d == 1 else cs[L - 1:L]
        kk, k = kk_ref[0], k_ref[0]
        bv = kk * a_ref[0]
        g_inv = jnp.exp(-cs)
        g_rem = jnp.exp(tot - cs)
        incl.append(m_incl)
        strict.append(m_strict)
        g_tot.append(jnp.exp(tot))
        vb.append(v_ref[0].astype(BF16))
        at.append((-(jnp.exp(cs - lw) * kk)).astype(BF16))
        rt.append((jnp.exp(cs) * r_ref[0]).astype(BF16))
        bh.append((g_inv * bv).astype(BF16))
        kh.append((g_inv * k).astype(BF16))
        bl.append((g_rem * bv).astype(BF16))
        kl.append((g_rem * k).astype(BF16))
    es = [(d, h) for d in range(2) for h in range(RW_HEADS)]
    ids = range(len(es))
    sl = lambda h: slice(h * RW_DH, (h + 1) * RW_DH)
    s0 = [s_scr[d, h] for d, h in es]
    lhs = [jnp.concatenate([at[d][:, sl(h)], rt[d][:, sl(h)]], axis=0) for d, h in es]
    rhs = [jnp.concatenate([bh[d][:, sl(h)], kh[d][:, sl(h)]], axis=0) for d, h in es]
    vh = [vb[d][:, sl(h)] for d, h in es]
    sc = [_dot_nt(lhs[i], rhs[i]) for i in ids]
    zs = [_dot_nt(lhs[i], s0[i]) for i in ids]
    mab = [jnp.where(strict[d], sc[i][:L, :L], 0.0) for i, (d, h) in enumerate(es)]
    mak = [jnp.where(strict[d], sc[i][:L, L:], 0.0) for i, (d, h) in enumerate(es)]
    nrb = [jnp.where(incl[d], sc[i][L:, :L], 0.0) for i, (d, h) in enumerate(es)]
    nrk = [jnp.where(incl[d], sc[i][L:, L:], 0.0) for i, (d, h) in enumerate(es)]
    z0 = [zs[i][:L] + _dot(mak[i], vh[i]) for i in ids]
    tinv = _unit_tri_inverse(mab, blk)
    u = [_dot(tinv[i], z0[i]) for i in ids]
    o1 = [_dot(nrk[i], vh[i]) for i in ids]
    o2 = [_dot(nrb[i], u[i]) for i in ids]
    upd = [_dot_tn(jnp.concatenate([u[i].astype(BF16), vh[i]], axis=0),
                   jnp.concatenate([bl[d][:, sl(h)], kl[d][:, sl(h)]], axis=0)) for i, (d, h) in enumerate(es)]
    for i, (d, h) in enumerate(es):
        (of_ref, ob_ref)[d][0, :, sl(h)] = zs[i][L:] + o1[i] + o2[i]
        s_scr[d, h] = s0[i] * g_tot[d][:, sl(h)] + upd[i]


def _rw_scan(r, lwf, lwb, k, v, kk, a, n_ctx):
    B, S, W = r.shape
    nc, fwd, bwd = _scan_index_maps(S, n_ctx)
    spec = lambda cidx: pl.BlockSpec((1, CHUNK, W), lambda b, s: (b, cidx(s), 0))
    return pl.pallas_call(
        _rw_scan_kernel,
        grid=(B, nc),
        in_specs=[spec(fwd)] * 6 + [spec(bwd)] * 6,
        out_specs=[spec(fwd), spec(bwd)],
        out_shape=[jax.ShapeDtypeStruct((B, S, W), F32)] * 2,
        scratch_shapes=[pltpu.VMEM((2, RW_HEADS, RW_DH, RW_DH), F32)],
        compiler_params=_cp("parallel", "arbitrary"),
        name="rwkv_scan",
    )(r, lwf, k, v, kk, a, r, lwb, k, v, kk, a)


def _even_out_kernel(x_ref, ogf, ogb, gate_ref, orf, orb, bonus_ref, grw_ref, glag, lng, lnb, ones128, ones64,
                     w_ref, gl, gc, o_ref, *, tm, n_ctx):
    og = ogf[0] + ogb[0]
    ms = _split_dot_r(og * og, ones128[...], 2) * (1.0 / GLA_DV)
    gt = gate_ref[0]
    y_gla = og * lax.rsqrt(ms + 1e-6) * glag[...] * (gt * jax.nn.sigmoid(gt))
    of = orf[0] + orb[0]
    o64 = ones64[...]
    mean = _split_dot_r(of, o64, 3) * (1.0 / RW_DH)
    cen = of - mean
    var = _split_dot_r(cen * cen, o64, 2) * (1.0 / RW_DH)
    y_rw = (cen * lax.rsqrt(var + RW_GN_EPS) * lng[...] + lnb[...] + bonus_ref[0]) * grw_ref[0]
    y = jnp.concatenate([y_gla, y_rw], axis=1).astype(BF16)
    rows = pl.program_id(1) * tm + lax.broadcasted_iota(jnp.int32, (tm, 1), 0)
    gate = jnp.where(rows < n_ctx, gc[0], gl[0])
    o_ref[0] = x_ref[0] + gate * jnp.dot(y, w_ref[...], preferred_element_type=F32)


def _even_out(xs, og_f, og_b, z_gla, or_f, or_b, bonus, g_rw, p, mod, n_ctx, *, tm):
    B, S, D = xs.shape
    W = RW_W
    row = lambda width: pl.BlockSpec((1, tm, width), lambda b, j: (b, j, 0))
    full = lambda shape: pl.BlockSpec(shape, lambda b, j: (0,) * len(shape))
    return pl.pallas_call(
        functools.partial(_even_out_kernel, tm=tm, n_ctx=n_ctx),
        grid=(B, S // tm),
        in_specs=[row(D), row(GLA_V), row(GLA_V),
                  pl.BlockSpec((1, tm, GLA_V), lambda b, j: (b, j, (2 * GLA_QK + GLA_V) // GLA_V)),
                  row(W), row(W), row(W), row(W),
                  full((1, GLA_V)), full((1, W)), full((1, W)), full((GLA_V, GLA_V)), full((W, W)),
                  full((GLA_V + W, D)), *_mod_specs(B, D, 2, 2)],
        out_specs=row(D),
        out_shape=jax.ShapeDtypeStruct((B, S, D), F32),
        compiler_params=_cp("parallel", "parallel"),
        name="even_out",
    )(xs, og_f, og_b, z_gla, or_f, or_b, bonus, g_rw, p["gla_g"], p["ln_g"], p["ln_b"], p["ones128"], p["ones64"],
      p["w_out"], mod, mod)


def _odd_prep_kernel(z_ref, cos_ref, sin_ref, qg_ref, kg_ref, ones64, qc_o, kc_o, vc_o, qd_o, kd_o, vd_o):
    z = z_ref[0]
    cos, sin = cos_ref[...], sin_ref[...]
    o64 = ones64[...]
    nq = C_HEADS * HEAD_DIM
    nk = C_KV * HEAD_DIM

    def norm_rope(x, g, width):
        ms = _split_dot_r(x * x, o64[:width, :width], 2) * (1.0 / HEAD_DIM)
        y = x * lax.rsqrt(ms + 1e-6) * g
        lane = lax.broadcasted_iota(jnp.int32, (1, width), 1)
        first = (lane % (HEAD_DIM // 2)) < (HEAD_DIM // 4)
        swapped = jnp.where(first, pltpu.roll(y, width - HEAD_DIM // 4, 1), pltpu.roll(y, HEAD_DIM // 4, 1))
        return y * cos[:, :width] + swapped * sin[:, :width]

    qc_o[0] = (norm_rope(z[:, 0:nq], qg_ref[...], nq) * HEAD_DIM ** -0.5).astype(BF16)
    kc_o[0] = norm_rope(z[:, nq:nq + nk], kg_ref[:, :nk], nk).astype(BF16)
    vc_o[0] = z[:, nq + nk:nq + 2 * nk].astype(BF16)
    base = nq + 2 * nk
    nd = NA_HEADS * HEAD_DIM
    qd_o[0] = (z[:, base:base + nd] * HEAD_DIM ** -0.5).astype(BF16)
    kd_o[0] = z[:, base + nd:base + 2 * nd].astype(BF16)
    vd_o[0] = z[:, base + 2 * nd:base + 3 * nd].astype(BF16)


def _odd_prep(z, cos_t, sin_t, qg, kg, ones64, *, tt):
    B, S, Z = z.shape
    nq, nk, nd = C_HEADS * HEAD_DIM, C_KV * HEAD_DIM, NA_HEADS * HEAD_DIM
    full = lambda shape: pl.BlockSpec(shape, lambda b, j: (0,) * len(shape))
    row = lambda width: pl.BlockSpec((1, tt, width), lambda b, j: (b, j, 0))
    widths = (nq, nk, nk, nd, nd, nd)
    return pl.pallas_call(
        _odd_prep_kernel,
        grid=(B, S // tt),
        in_specs=[row(Z), pl.BlockSpec((tt, nq), lambda b, j: (j, 0)), pl.BlockSpec((tt, nq), lambda b, j: (j, 0)),
                  full((1, nq)), full((1, nq)), full((nq, nq))],
        out_specs=[row(w) for w in widths],
        out_shape=[jax.ShapeDtypeStruct((B, S, w), BF16) for w in widths],
        compiler_params=_cp("parallel", "parallel"),
        name="odd_prep",
    )(z, cos_t, sin_t, qg, kg, ones64)


def _pair_attn_kernel(q_ref, k_ref, v_ref, o_ref, *, shared_kv):
    q = q_ref[0]
    tq = q.shape[0]
    ngroups = q.shape[1] // LANE
    lane = lax.broadcasted_iota(jnp.int32, (1, LANE), 1)
    masks = (lane < HEAD_DIM, lane >= HEAD_DIM)
    zero = jnp.zeros((), q.dtype)
    if shared_kv:
        kp, vp = k_ref[0], v_ref[0]
        lhs = [jnp.concatenate([jnp.where(masks[hm], q[:, g * LANE:(g + 1) * LANE], zero) for g in range(ngroups)],
                               axis=0) for hm in range(2)]
        s = [_dot_nt(lhs[hm], kp) for hm in range(2)]
        p = [jnp.exp(s[hm] - jnp.max(s[hm], axis=-1, keepdims=True)) for hm in range(2)]
        den = [jnp.sum(p[hm], axis=-1, keepdims=True) for hm in range(2)]
        o = [_dot(p[hm], vp) / den[hm] for hm in range(2)]
        for g in range(ngroups):
            rows = slice(g * tq, (g + 1) * tq)
            o_ref[0, :, g * LANE:(g + 1) * LANE] = jnp.where(masks[0], o[0][rows], o[1][rows]).astype(o_ref.dtype)
    else:
        for g in range(ngroups):
            qp = q[:, g * LANE:(g + 1) * LANE]
            kp = k_ref[0, :, g * LANE:(g + 1) * LANE]
            vp = v_ref[0, :, g * LANE:(g + 1) * LANE]
            lhs = jnp.concatenate([jnp.where(masks[0], qp, zero), jnp.where(masks[1], qp, zero)], axis=0)
            s = _dot_nt(lhs, kp)
            p = jnp.exp(s - jnp.max(s, axis=-1, keepdims=True))
            o = _dot(p, vp) / jnp.sum(p, axis=-1, keepdims=True)
            o_ref[0, :, g * LANE:(g + 1) * LANE] = jnp.where(masks[0], o[:tq], o[tq:]).astype(o_ref.dtype)


def _pair_attention(q, k, v, *, q_row0, n_q, n_k, tq, shared_kv, name):
    B, S, QW = q.shape
    KW = k.shape[2]
    return pl.pallas_call(
        functools.partial(_pair_attn_kernel, shared_kv=shared_kv),
        grid=(B, n_q // tq),
        in_specs=[pl.BlockSpec((1, tq, QW), lambda b, i: (b, q_row0 // tq + i, 0)),
                  pl.BlockSpec((1, n_k, KW), lambda b, i: (b, 0, 0)),
                  pl.BlockSpec((1, n_k, KW), lambda b, i: (b, 0, 0))],
        out_specs=pl.BlockSpec((1, tq, QW), lambda b, i: (b, i, 0)),
        out_shape=jax.ShapeDtypeStruct((B, n_q, QW), BF16),
        compiler_params=_cp("parallel", "parallel"),
        name=name,
    )(q, k, v)


def _na_kernel(q_ref, k_ref, v_ref, bias_ref, o_ref, *, n_ctx, n_rows, kh):
    r = pl.program_id(1)
    rs = jnp.clip(r - kh // 2, 0, n_rows - kh)
    start = pl.multiple_of(n_ctx + rs * GRID_W, GRID_W)
    nkeys = kh * GRID_W
    q = q_ref[0]
    lane = lax.broadcasted_iota(jnp.int32, (1, LANE), 1)
    m0 = lane < HEAD_DIM
    zero = jnp.zeros((), q.dtype)
    gs = [slice(g * LANE, (g + 1) * LANE) for g in range(NA_HEADS // 2)]
    ids = range(len(gs))
    lhs = [jnp.concatenate([jnp.where(m0, q[:, s], zero), jnp.where(m0, zero, q[:, s])], axis=0) for s in gs]
    s_nb = [_dot_nt(lhs[g], k_ref[0, pl.ds(start, nkeys), gs[g]]) for g in ids]
    s_cx = [_dot_nt(lhs[g], k_ref[0, 0:n_ctx, gs[g]]) for g in ids]
    s_nb = [s_nb[g] + jnp.concatenate([bias_ref[0, 2 * g], bias_ref[0, 2 * g + 1]], axis=0) for g in ids]
    m = [jnp.maximum(jnp.max(s_nb[g], axis=-1, keepdims=True), jnp.max(s_cx[g], axis=-1, keepdims=True))
         for g in ids]
    p_nb = [jnp.exp(s_nb[g] - m[g]) for g in ids]
    p_cx = [jnp.exp(s_cx[g] - m[g]) for g in ids]
    den = [jnp.sum(p_nb[g], axis=-1, keepdims=True) + jnp.sum(p_cx[g], axis=-1, keepdims=True) for g in ids]
    o_nb = [_dot(p_nb[g], v_ref[0, pl.ds(start, nkeys), gs[g]]) for g in ids]
    o_cx = [_dot(p_cx[g], v_ref[0, 0:n_ctx, gs[g]]) for g in ids]
    for g in ids:
        o = (o_nb[g] + o_cx[g]) / den[g]
        o_ref[0, :, gs[g]] = jnp.where(m0, o[:GRID_W], o[GRID_W:]).astype(o_ref.dtype)


def _na_attention(qd, kd, vd, bias_tab, n_ctx):
    B, S, W = qd.shape
    T = S - n_ctx
    n_rows = T // GRID_W
    kh = min(NA_KH, n_rows)
    nkeys = kh * GRID_W

    def bias_idx(b, r):
        return (jnp.clip(r - kh // 2, 0, n_rows - kh) - r + kh - 1, 0, 0, 0)

    return pl.pallas_call(
        functools.partial(_na_kernel, n_ctx=n_ctx, n_rows=n_rows, kh=kh),
        grid=(B, n_rows),
        in_specs=[pl.BlockSpec((1, GRID_W, W), lambda b, r: (b, n_ctx // GRID_W + r, 0)),
                  pl.BlockSpec((1, S, W), lambda b, r: (b, 0, 0)),
                  pl.BlockSpec((1, S, W), lambda b, r: (b, 0, 0)),
                  pl.BlockSpec((1, NA_HEADS, GRID_W, nkeys), bias_idx)],
        out_specs=pl.BlockSpec((1, GRID_W, W), lambda b, r: (b, r, 0)),
        out_shape=jax.ShapeDtypeStruct((B, T, W), BF16),
        compiler_params=_cp("parallel", "arbitrary"),
        name="na_attention",
    )(qd, kd, vd, bias_tab)


def _na_bias_table(rpb, n_rows):
    kh = min(NA_KH, n_rows)
    kw = min(NA_KW, GRID_W)
    col = np.arange(GRID_W)
    start = np.clip(col - kw // 2, 0, GRID_W - kw)
    in_win = (col[None, :] >= start[:, None]) & (col[None, :] < start[:, None] + kw)
    dc = np.clip(col[None, :] - col[:, None], -(NA_KW - 1), NA_KW - 1) + NA_KW - 1
    bias_cols = rpb[:, :, dc]
    slabs = []
    for st in range(kh):
        sl = bias_cols[:, st + NA_KH - kh:st + NA_KH - kh + kh]
        sl = jnp.where(in_win[None, None], sl, NEG).transpose(0, 2, 1, 3)
        slabs.append(sl.reshape(NA_HEADS, GRID_W, kh * GRID_W))
    return jnp.stack(slabs).astype(F32)


def _proj_res_kernel(x_ref, y_ref, w_ref, gl, gc, o_ref, *, tm, n_ctx):
    rows = pl.program_id(1) * tm + lax.broadcasted_iota(jnp.int32, (tm, 1), 0)
    gate = jnp.where(rows < n_ctx, gc[0], gl[0])
    o_ref[0] = x_ref[0] + gate * jnp.dot(y_ref[0], w_ref[...], preferred_element_type=F32)


def _proj_res(xs, y, w, mod, n_ctx, *, tm):
    B, S, D = xs.shape
    K = y.shape[2]
    row = lambda width: pl.BlockSpec((1, tm, width), lambda b, j: (b, j, 0))
    return pl.pallas_call(
        functools.partial(_proj_res_kernel, tm=tm, n_ctx=n_ctx),
        grid=(B, S // tm),
        in_specs=[row(D), row(K), pl.BlockSpec((K, D), lambda b, j: (0, 0)), *_mod_specs(B, D, 2, 2)],
        out_specs=row(D),
        out_shape=jax.ShapeDtypeStruct((B, S, D), F32),
        compiler_params=_cp("parallel", "parallel"),
        name="proj_res",
    )(xs, y, w, mod, mod)


def _final_norm_kernel(x_ref, g_ref, o_ref):
    x = x_ref[0]
    o_ref[0] = x * lax.rsqrt(jnp.mean(x * x, axis=-1, keepdims=True) + 1e-6) * g_ref[...]


def _final_norm(xs, g, n_ctx, *, tm):
    B, S, D = xs.shape
    T = S - n_ctx
    return pl.pallas_call(
        _final_norm_kernel,
        grid=(B, T // tm),
        in_specs=[pl.BlockSpec((1, tm, D), lambda b, j: (b, n_ctx // tm + j, 0)),
                  pl.BlockSpec((1, D), lambda b, j: (0, 0))],
        out_specs=pl.BlockSpec((1, tm, D), lambda b, j: (b, j, 0)),
        out_shape=jax.ShapeDtypeStruct((B, T, D), F32),
        compiler_params=_cp("parallel", "parallel"),
        name="final_norm",
    )(xs, g.reshape(1, D))


def _block_ones(width, block):
    idx = np.arange(width) // block
    return jnp.asarray(idx[:, None] == idx[None, :], BF16)


def _pad_cols(w, width):
    return jnp.pad(w, ((0, 0), (0, width - w.shape[1])))


def _pad_rows(w, height):
    return jnp.pad(w, ((0, height - w.shape[0]), (0, 0)))


def _even_params(j, ev_w_in, ev_w_out, gla_a_up, gla_a_bias, gla_norm_g, rw_mu, rw_w0, rw_w_up, rw_a0, rw_a_up,
                 rw_g_up, rw_k_k, rw_k_a, rw_r_k, rw_ln_g, rw_ln_b):
    w_in = ev_w_in[j]
    D = w_in.shape[0]
    gq = 2 * GLA_QK + 2 * GLA_V
    w_gla = jnp.concatenate([w_in[:, :gq], _pad_cols(w_in[:, gq:gq + GLA_RANK], LANE),
                             _pad_cols(w_in[:, gq + GLA_RANK:gq + 2 * GLA_RANK], LANE)], axis=1)
    o = gq + 2 * GLA_RANK
    sizes = (3 * RW_W, RW_DECAY_RANK, RW_DECAY_RANK, RW_A_RANK, RW_G_RANK)
    offs = np.cumsum((0,) + sizes)
    pieces = [w_in[:, o:o + 3 * RW_W]] + [_pad_cols(w_in[:, o + offs[i]:o + offs[i + 1]], LANE) for i in range(1, 5)]
    w_rw = jnp.concatenate(pieces, axis=1)
    mu = rw_mu[j]
    mu_pieces = [mu[:, :3 * RW_W]] + [_pad_cols(mu[:, offs[i]:offs[i + 1]], LANE) for i in range(1, 5)]
    return {
        "w_gla": w_gla.astype(BF16), "w_rw": w_rw.astype(BF16), "w_out": ev_w_out[j].astype(BF16),
        "gla_aup": jnp.stack([_pad_rows(gla_a_up[j, d], LANE) for d in range(2)]).astype(BF16),
        "gla_bias": gla_a_bias[j], "gla_g": jnp.tile(gla_norm_g[j], GLA_HEADS).reshape(1, GLA_V),
        "mu": jnp.concatenate(mu_pieces, axis=1), "w0": rw_w0[j],
        "wup": jnp.stack([_pad_rows(rw_w_up[j, d], LANE) for d in range(2)]).astype(BF16),
        "a0": rw_a0[j].reshape(1, RW_W), "aup": _pad_rows(rw_a_up[j], LANE).astype(BF16),
        "gup": _pad_rows(rw_g_up[j], LANE).astype(BF16),
        "k_k": rw_k_k[j].reshape(1, RW_W), "k_a": rw_k_a[j].reshape(1, RW_W), "r_k": rw_r_k[j].reshape(1, RW_W),
        "ln_g": rw_ln_g[j].reshape(1, RW_W), "ln_b": rw_ln_b[j].reshape(1, RW_W),
        "ones64": _block_ones(RW_W, RW_DH), "ones128": _block_ones(GLA_V, GLA_DV),
    }


def _gqa_head_order():
    per_kv = C_HEADS // C_KV
    order = []
    for jj in range(per_kv):
        for kv in range(C_KV):
            h = kv * per_kv + jj
            order.extend(range(h * HEAD_DIM, (h + 1) * HEAD_DIM))
    return np.asarray(order)


def _rope_tables(n_ctx, T, width):
    t = jnp.arange(T)
    pos = jnp.stack([t // GRID_W, t % GRID_W], axis=-1).astype(F32)
    half = HEAD_DIM // 2
    inv = ROPE_THETA ** (-jnp.arange(0, half, 2, dtype=F32) / half)
    ang = pos[:, :, None] * inv
    cos, sin = jnp.cos(ang), jnp.sin(ang)
    cos_h = jnp.concatenate([cos, cos], axis=-1).reshape(T, HEAD_DIM)
    sin_h = jnp.concatenate([-sin, sin], axis=-1).reshape(T, HEAD_DIM)
    cos_h = jnp.concatenate([jnp.ones((n_ctx, HEAD_DIM), F32), cos_h], axis=0)
    sin_h = jnp.concatenate([jnp.zeros((n_ctx, HEAD_DIM), F32), sin_h], axis=0)
    reps = width // HEAD_DIM
    return jnp.tile(cos_h, (1, reps)), jnp.tile(sin_h, (1, reps))


def _even_layer(xs, mod, n_ctx, norm1_g, p):
    tm = _row_tile(xs.shape[1], ROW_CAP_MATMUL)
    z_gla = _norm_mod_matmul(xs, norm1_g, mod, p["w_gla"], n_ctx, tm=tm, tn=GLA_Z // 2, name="even_in_gla")
    z_rw = _norm_mod_matmul(xs, norm1_g, mod, p["w_rw"], n_ctx, tm=tm, tn=RW_Z // 2, name="even_in_rw")
    og_f, og_b = _gla_scan(z_gla, p["gla_aup"], p["gla_bias"], n_ctx)
    r, lwf, lwb, k, v, kk, a, g_rw, bonus = _rw_features(z_rw, p, n_ctx)
    or_f, or_b = _rw_scan(r, lwf, lwb, k, v, kk, a, n_ctx)
    return _even_out(xs, og_f, og_b, z_gla, or_f, or_b, bonus, g_rw, p, mod, n_ctx,
                     tm=_row_tile(xs.shape[1], ROW_CAP_ELEMENTWISE))


def _odd_layer(xs, mod, n_ctx, norm1_g, p, need_ctx):
    B, S, D = xs.shape
    T = S - n_ctx
    z = _norm_mod_matmul(xs, norm1_g, mod, p["w_in"], n_ctx, tm=_row_tile(S, ROW_CAP_MATMUL),
                         tn=p["w_in"].shape[1] // 2, name="odd_in")
    qc, kc, vc, qd, kd, vd = _odd_prep(z, p["cos"], p["sin"], p["qg"], p["kg"], p["ones64"],
                                       tt=_row_tile(S, ROW_CAP_ELEMENTWISE))
    y_gqa = _pair_attention(qc, kc, vc, q_row0=n_ctx, n_q=T, n_k=S, tq=128, shared_kv=True, name="gqa")
    y_na = _na_attention(qd, kd, vd, p["bias_tab"], n_ctx)
    parts_g, parts_n = [y_gqa], [y_na]
    if need_ctx:
        parts_g.insert(0, _pair_attention(qc, kc, vc, q_row0=0, n_q=n_ctx, n_k=n_ctx, tq=n_ctx, shared_kv=True,
                                          name="gqa_ctx"))
        parts_n.insert(0, _pair_attention(qd, kd, vd, q_row0=0, n_q=n_ctx, n_k=n_ctx, tq=n_ctx, shared_kv=False,
                                          name="na_ctx"))
    else:
        zeros = jnp.zeros((B, n_ctx, y_gqa.shape[2]), BF16)
        parts_g.insert(0, zeros)
        parts_n.insert(0, zeros)
    y = jnp.concatenate([jnp.concatenate(parts_g, axis=1), jnp.concatenate(parts_n, axis=1)], axis=2)
    return _proj_res(xs, y, p["w_out"], mod, n_ctx, tm=_row_tile(S, ROW_CAP_MATMUL))


def kernel(x, c, ctx, c_ctx, w_mod, b_mod, norm1_g, norm2_g, ffn_w13, ffn_w2, ev_w_in, ev_w_out, gla_a_up,
           gla_a_bias, gla_norm_g, rw_mu, rw_w0, rw_w_up, rw_a0, rw_a_up, rw_g_up, rw_k_k, rw_k_a, rw_r_k, rw_ln_g,
           rw_ln_b, od_w_in, od_w_out, cq_norm_g, ck_norm_g, na_rpb, final_g):
    B, T, D = x.shape
    n_ctx = ctx.shape[1]
    S = n_ctx + T
    xs = jnp.concatenate([ctx, x], axis=1)
    mods = _mod_vectors(c, c_ctx, w_mod, b_mod)
    nq, nk = C_HEADS * HEAD_DIM, C_KV * HEAD_DIM
    order = _gqa_head_order()
    cos_t, sin_t = _rope_tables(n_ctx, T, nq)
    ones64 = _block_ones(nq, HEAD_DIM)
    for i in range(DEPTH):
        j = i // 2
        mod = mods[i]
        if i % 2 == 0:
            p = _even_params(j, ev_w_in, ev_w_out, gla_a_up, gla_a_bias, gla_norm_g, rw_mu, rw_w0, rw_w_up, rw_a0,
                             rw_a_up, rw_g_up, rw_k_k, rw_k_a, rw_r_k, rw_ln_g, rw_ln_b)
            xs = _even_layer(xs, mod, n_ctx, norm1_g[i], p)
        else:
            w_in = od_w_in[j]
            w_out = od_w_out[j]
            p = {
                "w_in": jnp.concatenate([w_in[:, :nq][:, order], w_in[:, nq:]], axis=1).astype(BF16),
                "w_out": jnp.concatenate([w_out[:nq][order], w_out[nq:]], axis=0).astype(BF16),
                "cos": cos_t, "sin": sin_t, "ones64": ones64,
                "qg": jnp.tile(cq_norm_g[j], C_HEADS).reshape(1, nq),
                "kg": jnp.tile(ck_norm_g[j], C_HEADS).reshape(1, nq),
                "bias_tab": _na_bias_table(na_rpb[j], T // GRID_W),
            }
            xs = _odd_layer(xs, mod, n_ctx, norm1_g[i], p, need_ctx=i < DEPTH - 1)
        xs = _ffn(xs, norm2_g[i], mod, ffn_w13[i].astype(BF16), ffn_w2[i].astype(BF16), n_ctx,
                  tm=_row_tile(S, ROW_CAP_FFN), tf=FFN_COL_TILE)
    return _final_norm(xs, final_g, n_ctx, tm=_row_tile(n_ctx, ROW_CAP_ELEMENTWISE))
```

```python
import functools

import jax
import jax.numpy as jnp
import numpy as np
from jax import lax
from jax.experimental import pallas as pl
from jax.experimental.pallas import tpu as pltpu

F32 = jnp.float32
BF16 = jnp.bfloat16

DEPTH = 4
GRID_W = 64
CHUNK = 64
LANE = 128

GLA_HEADS, GLA_DK, GLA_DV, GLA_RANK, GLA_TAU = 4, 64, 128, 16, 16.0
GLA_QK, GLA_V = GLA_HEADS * GLA_DK, GLA_HEADS * GLA_DV
GLA_Z = 2 * GLA_QK + 2 * GLA_V + 2 * LANE

RW_HEADS, RW_DH, RW_W = 8, 64, 512
RW_DECAY_RANK, RW_A_RANK, RW_G_RANK = 32, 32, 96
RW_GN_EPS = 64e-5
RW_Z = 3 * RW_W + 4 * LANE

HEAD_DIM, C_HEADS, C_KV, NA_HEADS, NA_KH, NA_KW = 64, 8, 2, 8, 8, 16
ROPE_THETA = 10000.0
NEG = -1e30

VMEM_LIMIT = 56 * 1024 * 1024
ROW_CAP_MATMUL = 1152
ROW_CAP_FFN = 576
ROW_CAP_ELEMENTWISE = 256
FFN_COL_TILE = 256


def _row_tile(n_rows, cap):
    return max(d for d in range(8, cap + 1, 8) if n_rows % d == 0)


def _cp(*sem):
    return pltpu.CompilerParams(dimension_semantics=sem, vmem_limit_bytes=VMEM_LIMIT)


def _dot(a, b):
    return jnp.dot(a.astype(BF16), b.astype(BF16), preferred_element_type=F32)


def _dot_nt(a, b):
    return lax.dot_general(a.astype(BF16), b.astype(BF16), (((1,), (1,)), ((), ())), preferred_element_type=F32)


def _dot_tn(a, b):
    return lax.dot_general(a.astype(BF16), b.astype(BF16), (((0,), (0,)), ((), ())), preferred_element_type=F32)


def _split_dot(m, x, parts):
    acc = None
    rem = x
    for _ in range(parts):
        piece = rem.astype(BF16)
        rem = rem - piece.astype(F32)
        t = jnp.dot(m, piece, preferred_element_type=F32)
        acc = t if acc is None else acc + t
    return acc


def _split_dot_r(x, m, parts):
    acc = None
    rem = x
    for _ in range(parts):
        piece = rem.astype(BF16)
        rem = rem - piece.astype(F32)
        t = jnp.dot(piece, m, preferred_element_type=F32)
        acc = t if acc is None else acc + t
    return acc


def _softplus(y):
    return jnp.maximum(y, 0.0) + jnp.log1p(jnp.exp(-jnp.abs(y)))


def _rms_mod(x, g, sc_l, sh_l, sc_c, sh_c, row0, n_ctx):
    y = x * lax.rsqrt(jnp.mean(x * x, axis=-1, keepdims=True) + 1e-6) * g
    rows = row0 + lax.broadcasted_iota(jnp.int32, (x.shape[0], 1), 0)
    is_ctx = rows < n_ctx
    sc = jnp.where(is_ctx, sc_c, sc_l)
    sh = jnp.where(is_ctx, sh_c, sh_l)
    return y * (1.0 + sc) + sh


def _tri_masks(n, reverse):
    ri = lax.broadcasted_iota(jnp.int32, (n, n), 0)
    ci = lax.broadcasted_iota(jnp.int32, (n, n), 1)
    if reverse:
        return ci >= ri, ci > ri
    return ci <= ri, ci < ri


INV_BASE = 16


def _block_ids(n):
    ri = lax.broadcasted_iota(jnp.int32, (n, n), 0)
    ci = lax.broadcasted_iota(jnp.int32, (n, n), 1)
    return ri, ci


def _unit_tri_inverse(a, blk):
    ri, ci = blk
    n = a[0].shape[0]
    ids = range(len(a))
    same = lambda bs: (ri // bs) == (ci // bs)
    base = same(INV_BASE)
    eye = (ri == ci).astype(F32)
    p = [jnp.where(base, x, 0.0) for x in a]
    t = [eye + x for x in p]
    m = 1
    while 2 * m < INV_BASE:
        p = [_dot(x, x) for x in p]
        tp = [_dot(t[i], p[i]) for i in ids]
        t = [t[i] + tp[i] for i in ids]
        m *= 2
    bs = INV_BASE
    while bs < n:
        mask = same(2 * bs) & jnp.logical_not(same(bs))
        ta = [_dot(t[i], jnp.where(mask, a[i], 0.0)) for i in ids]
        tat = [_dot(ta[i], t[i]) for i in ids]
        t = [t[i] + tat[i] for i in ids]
        bs *= 2
    return t


def _mod_kernel(s_ref, w_ref, b_ref, o_ref):
    s = s_ref[...]
    s = s * jax.nn.sigmoid(s)
    o_ref[0] = _dot(s, w_ref[0]) + b_ref[0]


def _mod_vectors(c, c_ctx, w_mod, b_mod):
    B, D = c.shape
    bp = -(-(B + 1) // 8) * 8
    s_in = jnp.zeros((bp, D), F32).at[:B].set(c).at[B].set(c_ctx)
    n6 = w_mod.shape[-1]
    tn = 1536
    out = pl.pallas_call(
        _mod_kernel,
        grid=(DEPTH, n6 // tn),
        in_specs=[pl.BlockSpec((bp, D), lambda l, n: (0, 0)),
                  pl.BlockSpec((1, D, tn), lambda l, n: (l, 0, n)),
                  pl.BlockSpec((1, 1, tn), lambda l, n: (l, 0, n))],
        out_specs=pl.BlockSpec((1, bp, tn), lambda l, n: (l, 0, n)),
        out_shape=jax.ShapeDtypeStruct((DEPTH, bp, n6), F32),
        compiler_params=_cp("parallel", "parallel"),
        name="mod_vectors",
    )(s_in, w_mod, b_mod.reshape(DEPTH, 1, n6))
    return out.reshape(DEPTH, bp, 1, n6)


def _mod_specs(B, D, idx, grid_rank):
    if grid_rank == 2:
        return [pl.BlockSpec((1, 1, D), lambda b, j: (b, 0, idx)),
                pl.BlockSpec((1, 1, D), lambda b, j: (B, 0, idx))]
    return [pl.BlockSpec((1, 1, D), lambda b, j, n: (b, 0, idx)),
            pl.BlockSpec((1, 1, D), lambda b, j, n: (B, 0, idx))]


def _norm_mod_kernel(x_ref, g_ref, shl, shc, scl, scc, o_ref, *, tm, n_ctx):
    h = _rms_mod(x_ref[0], g_ref[...], scl[0], shl[0], scc[0], shc[0], pl.program_id(1) * tm, n_ctx)
    o_ref[0] = h.astype(BF16)


def _norm_mod(xs, g, mod, n_ctx, *, tm):
    B, S, D = xs.shape
    row = pl.BlockSpec((1, tm, D), lambda b, j: (b, j, 0))
    return pl.pallas_call(
        functools.partial(_norm_mod_kernel, tm=tm, n_ctx=n_ctx),
        grid=(B, S // tm),
        in_specs=[row, pl.BlockSpec((1, D), lambda b, j: (0, 0)), *_mod_specs(B, D, 0, 2), *_mod_specs(B, D, 1, 2)],
        out_specs=row,
        out_shape=jax.ShapeDtypeStruct((B, S, D), BF16),
        compiler_params=_cp("parallel", "parallel"),
        name="norm_mod",
    )(xs, g.reshape(1, D), mod, mod, mod, mod)


def _proj_kernel(h_ref, *refs):
    n = len(refs) // 2
    h = h_ref[0]
    for w_ref, o_ref in zip(refs[:n], refs[n:]):
        o_ref[0] = jnp.dot(h, w_ref[...], preferred_element_type=F32)


def _project(h, ws, *, tm, name):
    B, S, D = h.shape
    row = lambda width: pl.BlockSpec((1, tm, width), lambda b, j: (b, j, 0))
    return pl.pallas_call(
        _proj_kernel,
        grid=(B, S // tm),
        in_specs=[row(D)] + [_resident(w.shape) for w in ws],
        out_specs=[row(w.shape[1]) for w in ws],
        out_shape=[jax.ShapeDtypeStruct((B, S, w.shape[1]), F32) for w in ws],
        compiler_params=_cp("parallel", "parallel"),
        name=name,
    )(h, *ws)


def _ffn_kernel(x_ref, g_ref, shl, shc, scl, scc, gl, gc, w13_ref, w2_ref, *rest, tm, tf, n_ctx, emit_next):
    if emit_next:
        gn_ref, nshl, nshc, nscl, nscc, o_ref, hn_ref = rest
    else:
        (o_ref,) = rest
    x = x_ref[0]
    row0 = pl.program_id(1) * tm
    h = _rms_mod(x, g_ref[...], scl[0], shl[0], scc[0], shc[0], row0, n_ctx).astype(BF16)
    n_hidden = w2_ref.shape[0]
    acc = None
    for f in range(n_hidden // tf):
        a = jnp.dot(h, w13_ref[:, f * tf:(f + 1) * tf], preferred_element_type=F32)
        b = jnp.dot(h, w13_ref[:, n_hidden + f * tf:n_hidden + (f + 1) * tf], preferred_element_type=F32)
        u = (a * jax.nn.sigmoid(a) * b).astype(BF16)
        t = jnp.dot(u, w2_ref[f * tf:(f + 1) * tf, :], preferred_element_type=F32)
        acc = t if acc is None else acc + t
    rows = row0 + lax.broadcasted_iota(jnp.int32, (tm, 1), 0)
    gate = jnp.where(rows < n_ctx, gc[0], gl[0])
    y = x + gate * acc
    o_ref[0] = y
    if emit_next:
        hn_ref[0] = _rms_mod(y, gn_ref[...], nscl[0], nshl[0], nscc[0], nshc[0], row0, n_ctx).astype(BF16)


def _resident(shape):
    return pl.BlockSpec(shape, lambda b, j: (0,) * len(shape), pipeline_mode=pl.Buffered(1))


def _ffn(xs, g, mod, w13, w2, n_ctx, *, tm, tf, next_norm=None):
    B, S, D = xs.shape
    row = pl.BlockSpec((1, tm, D), lambda b, j: (b, j, 0))
    vec = pl.BlockSpec((1, D), lambda b, j: (0, 0))
    in_specs = [row, vec, *_mod_specs(B, D, 3, 2), *_mod_specs(B, D, 4, 2), *_mod_specs(B, D, 5, 2),
                _resident(w13.shape), _resident(w2.shape)]
    args = [xs, g.reshape(1, D), mod, mod, mod, mod, mod, mod, w13, w2]
    out_specs, out_shape = [row], [jax.ShapeDtypeStruct((B, S, D), F32)]
    if next_norm is not None:
        g_next, mod_next = next_norm
        in_specs += [vec, *_mod_specs(B, D, 0, 2), *_mod_specs(B, D, 1, 2)]
        args += [g_next.reshape(1, D), mod_next, mod_next, mod_next, mod_next]
        out_specs.append(row)
        out_shape.append(jax.ShapeDtypeStruct((B, S, D), BF16))
    out = pl.pallas_call(
        functools.partial(_ffn_kernel, tm=tm, tf=tf, n_ctx=n_ctx, emit_next=next_norm is not None),
        grid=(B, S // tm),
        in_specs=in_specs,
        out_specs=out_specs,
        out_shape=out_shape,
        compiler_params=_cp("parallel", "parallel"),
        name="ffn",
    )(*args)
    return (out[0], out[1]) if next_norm is not None else (out[0], None)


def _chunk_index(s, nc_ctx, nc, reverse):
    if not reverse:
        return s
    return jnp.where(s < nc_ctx, nc_ctx - 1 - s, nc + nc_ctx - 1 - s)


def _gla_kernel(qkv_f, ad_f, qkv_b, ad_b, aup_ref, bias_ref, of_ref, ob_ref, st_scr):
    @pl.when(pl.program_id(1) == 0)
    def _():
        st_scr[...] = jnp.zeros_like(st_scr)

    L = CHUNK
    nb = qkv_f.shape[0]
    streams = [(bb, d) for bb in range(nb) for d in range(2)]
    qb, kb, kt, gt, v, incl = [], [], [], [], [], []
    for bb, d in streams:
        qkv_ref, ad_ref = ((qkv_f, ad_f), (qkv_b, ad_b))[d]
        m, _ = _tri_masks(L, d == 1)
        ad = ad_ref[bb][:, d * LANE:(d + 1) * LANE]
        y = _dot(ad, aup_ref[d]) + bias_ref[d:d + 1, :]
        la = (jnp.minimum(y, 0.0) - jnp.log1p(jnp.exp(-jnp.abs(y)))) * (1.0 / GLA_TAU)
        cs = _split_dot(m.astype(BF16), la, 3)
        tot = cs[0:1] if d == 1 else cs[L - 1:L]
        qkv = qkv_ref[bb]
        q = qkv[:, 0:GLA_QK] * GLA_DK ** -0.5
        k = qkv[:, GLA_QK:2 * GLA_QK]
        incl.append(m)
        v.append(qkv[:, 2 * GLA_QK:2 * GLA_QK + GLA_V].astype(BF16))
        qb.append((q * jnp.exp(cs)).astype(BF16))
        kb.append((k * jnp.exp(-cs)).astype(BF16))
        kt.append((k * jnp.exp(tot - cs)).astype(BF16))
        gt.append(jnp.exp(tot))
    es = [(c, h) for c in range(len(streams)) for h in range(GLA_HEADS)]
    ks = lambda h: slice(h * GLA_DK, (h + 1) * GLA_DK)
    vs = lambda h: slice(h * GLA_DV, (h + 1) * GLA_DV)
    st = [st_scr[c, h] for c, h in es]
    att = [jnp.where(incl[c], _dot_nt(qb[c][:, ks(h)], kb[c][:, ks(h)]), 0.0) for c, h in es]
    o_state = [_dot_nt(qb[c][:, ks(h)], st[i]) for i, (c, h) in enumerate(es)]
    o_att = [_dot(att[i], v[c][:, vs(h)]) for i, (c, h) in enumerate(es)]
    upd = [_dot_tn(v[c][:, vs(h)], kt[c][:, ks(h)]) for c, h in es]
    for i, (c, h) in enumerate(es):
        bb, d = streams[c]
        (of_ref, ob_ref)[d][bb, :, vs(h)] = o_att[i] + o_state[i]
        st_scr[c, h] = st[i] * gt[c][:, ks(h)] + upd[i]


GLA_SCAN_BATCH = 4
RW_SCAN_BATCH = 2


def _scan_batch(B, want):
    return want if B % want == 0 else 1


def _scan_index_maps(S, n_ctx):
    nc, nc_ctx = S // CHUNK, n_ctx // CHUNK
    fwd = functools.partial(_chunk_index, nc_ctx=nc_ctx, nc=nc, reverse=False)
    bwd = functools.partial(_chunk_index, nc_ctx=nc_ctx, nc=nc, reverse=True)
    return nc, fwd, bwd


def _gla_scan(z_gla, aup_pad, bias, n_ctx):
    B, S, _ = z_gla.shape
    nc, fwd, bwd = _scan_index_maps(S, n_ctx)
    ad_blk = (2 * GLA_QK + 2 * GLA_V) // (2 * LANE)
    nb = _scan_batch(B, GLA_SCAN_BATCH)
    qkv_spec =lambda cidx: pl.BlockSpec((nb, CHUNK, 2 * GLA_QK + GLA_V), lambda b, s: (b, cidx(s), 0))
    ad_spec = lambda cidx: pl.BlockSpec((nb, CHUNK, 2 * LANE), lambda b, s: (b, cidx(s), ad_blk))
    out_spec = lambda cidx: pl.BlockSpec((nb, CHUNK, GLA_V), lambda b, s: (b, cidx(s), 0))
    return pl.pallas_call(
        _gla_kernel,
        grid=(B // nb, nc),
        in_specs=[qkv_spec(fwd), ad_spec(fwd), qkv_spec(bwd), ad_spec(bwd),
                  pl.BlockSpec((2, LANE, GLA_QK), lambda b, s: (0, 0, 0)),
                  pl.BlockSpec((2, GLA_QK), lambda b, s: (0, 0))],
        out_specs=[out_spec(fwd), out_spec(bwd)],
        out_shape=[jax.ShapeDtypeStruct((B, S, GLA_V), F32)] * 2,
        scratch_shapes=[pltpu.VMEM((2 * nb, GLA_HEADS, GLA_DV, GLA_DK), F32)],
        compiler_params=_cp("parallel", "arbitrary"),
        name="gla_scan",
    )(z_gla, z_gla, z_gla, z_gla, aup_pad, bias)


def _rw_feat_kernel(z_ref, zp_ref, zn_ref, mu_ref, w0_ref, wup_ref, a0_ref, aup_ref, gup_ref, kk_ref, ka_ref,
                    rk_ref, ones_ref, r_o, lwf_o, lwb_o, k_o, v_o, kk_o, a_o, g_o, bonus_o, *, tt):
    j = pl.program_id(1)
    last = pl.num_programs(1) - 1
    z = z_ref[0]
    prev_row = jnp.where(j <= 1, 0.0, zp_ref[0, 7:8, :])
    next_row = jnp.where((j == 0) | (j == last), 0.0, zn_ref[0, 0:1, :])
    rid = lax.broadcasted_iota(jnp.int32, (tt, 1), 0)
    prev = jnp.where(rid == 0, prev_row, pltpu.roll(z, 1, 0))
    nxt = jnp.where(rid == tt - 1, next_row, pltpu.roll(z, tt - 1, 0))
    zs = z + mu_ref[0:1, :] * (prev - z) + mu_ref[1:2, :] * (nxt - z)

    W = RW_W
    r, k, v = zs[:, 0:W], zs[:, W:2 * W], zs[:, 2 * W:3 * W]
    base = 3 * W
    wd = (zs[:, base:base + LANE], zs[:, base + LANE:base + 2 * LANE])
    ad = zs[:, base + 2 * LANE:base + 3 * LANE]
    gd = zs[:, base + 3 * LANE:base + 4 * LANE]
    for d, lw_o in enumerate((lwf_o, lwb_o)):
        wl = -_softplus(-(w0_ref[d:d + 1, :] + _dot(jnp.tanh(wd[d]), wup_ref[d]))) - 0.5
        lw_o[0] = -jnp.exp(wl)
    a = jax.nn.sigmoid(a0_ref[...] + _dot(ad, aup_ref[...]))
    g = _dot(jax.nn.sigmoid(gd), gup_ref[...])
    ones = ones_ref[...]
    kk = k * kk_ref[...]
    ss = _split_dot_r(kk * kk, ones, 2)
    kk = kk / jnp.maximum(jnp.sqrt(ss), 1e-12)
    kmod = k * (1.0 + (a - 1.0) * ka_ref[...])
    bonus = _split_dot_r(r * kmod * rk_ref[...], ones, 2) * v
    r_o[0] = r
    k_o[0] = kmod
    v_o[0] = v
    kk_o[0] = kk
    a_o[0] = a
    g_o[0] = g
    bonus_o[0] = bonus


def _rw_features(z_rw, p, n_ctx):
    B, S, Z = z_rw.shape
    tt = n_ctx
    nt = S // tt
    n8 = S // 8
    W = RW_W
    full = lambda shape: pl.BlockSpec(shape, lambda b, j: (0,) * len(shape))
    out_spec = pl.BlockSpec((1, tt, W), lambda b, j: (b, j, 0))
    return pl.pallas_call(
        functools.partial(_rw_feat_kernel, tt=tt),
        grid=(B, nt),
        in_specs=[pl.BlockSpec((1, tt, Z), lambda b, j: (b, j, 0)),
                  pl.BlockSpec((1, 8, Z), lambda b, j: (b, jnp.maximum(j * (tt // 8) - 1, 0), 0)),
                  pl.BlockSpec((1, 8, Z), lambda b, j: (b, jnp.minimum((j + 1) * (tt // 8), n8 - 1), 0)),
                  full((2, Z)), full((2, W)), full((2, LANE, W)), full((1, W)), full((LANE, W)), full((LANE, W)),
                  full((1, W)), full((1, W)), full((1, W)), full((W, W))],
        out_specs=[out_spec] * 9,
        out_shape=[jax.ShapeDtypeStruct((B, S, W), F32)] * 9,
        compiler_params=_cp("parallel", "parallel"),
        name="rwkv_features",
    )(z_rw, z_rw, z_rw, p["mu"], p["w0"], p["wup"], p["a0"], p["aup"], p["gup"], p["k_k"], p["k_a"], p["r_k"],
      p["ones64"])


def _rw_scan_kernel(r_f, lw_f, k_f, v_f, kk_f, a_f, r_b, lw_b, k_b, v_b, kk_b, a_b, of_ref, ob_ref, s_scr):
    @pl.when(pl.program_id(1) == 0)
    def _():
        s_scr[...] = jnp.zeros_like(s_scr)

    L = CHUNK
    blk = _block_ids(L)
    nb = r_f.shape[0]
    streams = [(bb, dr) for bb in range(nb) for dr in range(2)]
    incl, strict, at, rt, bh, kh, bl, kl, vb, g_tot = ([] for _ in range(10))
    for bb, dr in streams:
        r_ref, lw_ref, k_ref, v_ref, kk_ref, a_ref = ((r_f, lw_f, k_f, v_f, kk_f, a_f),
                                                      (r_b, lw_b, k_b, v_b, kk_b, a_b))[dr]
        m_incl, m_strict = _tri_masks(L, dr == 1)
        lw = lw_ref[bb]
        cs = _split_dot(m_incl.astype(BF16), lw, 3)
        tot = cs[0:1] if dr == 1 else cs[L - 1:L]
        kk, k = kk_ref[bb], k_ref[bb]
        bv = kk * a_ref[bb]
        g_inv = jnp.exp(-cs)
        g_rem = jnp.exp(tot - cs)
        incl.append(m_incl)
        strict.append(m_strict)
        g_tot.append(jnp.exp(tot))
        vb.append(v_ref[bb].astype(BF16))
        at.append((-(jnp.exp(cs - lw) * kk)).astype(BF16))
        rt.append((jnp.exp(cs) * r_ref[bb]).astype(BF16))
        bh.append((g_inv * bv).astype(BF16))
        kh.append((g_inv * k).astype(BF16))
        bl.append((g_rem * bv).astype(BF16))
        kl.append((g_rem * k).astype(BF16))
    es = [(d, h) for d in range(len(streams)) for h in range(RW_HEADS)]
    ids = range(len(es))
    sl = lambda h: slice(h * RW_DH, (h + 1) * RW_DH)
    s0 = [s_scr[d, h] for d, h in es]
    lhs = [jnp.concatenate([at[d][:, sl(h)], rt[d][:, sl(h)]], axis=0) for d, h in es]
    rhs = [jnp.concatenate([bh[d][:, sl(h)], kh[d][:, sl(h)]], axis=0) for d, h in es]
    vh = [vb[d][:, sl(h)] for d, h in es]
    sc = [_dot_nt(lhs[i], rhs[i]) for i in ids]
    zs = [_dot_nt(lhs[i], s0[i]) for i in ids]
    mab = [jnp.where(strict[d], sc[i][:L, :L], 0.0) for i, (d, h) in enumerate(es)]
    mak = [jnp.where(strict[d], sc[i][:L, L:], 0.0) for i, (d, h) in enumerate(es)]
    nrb = [jnp.where(incl[d], sc[i][L:, :L], 0.0) for i, (d, h) in enumerate(es)]
    nrk = [jnp.where(incl[d], sc[i][L:, L:], 0.0) for i, (d, h) in enumerate(es)]
    z0 = [zs[i][:L] + _dot(mak[i], vh[i]) for i in ids]
    tinv = _unit_tri_inverse(mab, blk)
    u = [_dot(tinv[i], z0[i]) for i in ids]
    o1 = [_dot(nrk[i], vh[i]) for i in ids]
    o2 = [_dot(nrb[i], u[i]) for i in ids]
    upd = [_dot_tn(jnp.concatenate([u[i].astype(BF16), vh[i]], axis=0),
                   jnp.concatenate([bl[d][:, sl(h)], kl[d][:, sl(h)]], axis=0)) for i, (d, h) in enumerate(es)]
    for i, (d, h) in enumerate(es):
        bb, dr = streams[d]
        (of_ref, ob_ref)[dr][bb, :, sl(h)] = zs[i][L:] + o1[i] + o2[i]
        s_scr[d, h] = s0[i] * g_tot[d][:, sl(h)] + upd[i]


def _rw_scan(r, lwf, lwb, k, v, kk, a, n_ctx):
    B, S, W = r.shape
    nc, fwd, bwd = _scan_index_maps(S, n_ctx)
    nb = _scan_batch(B, RW_SCAN_BATCH)
    spec = lambda cidx: pl.BlockSpec((nb, CHUNK, W), lambda b, s: (b, cidx(s), 0))
    return pl.pallas_call(
        _rw_scan_kernel,
        grid=(B // nb, nc),
        in_specs=[spec(fwd)] * 6 + [spec(bwd)] * 6,
        out_specs=[spec(fwd), spec(bwd)],
        out_shape=[jax.ShapeDtypeStruct((B, S, W), F32)] * 2,
        scratch_shapes=[pltpu.VMEM((2 * nb, RW_HEADS, RW_DH, RW_DH), F32)],
        compiler_params=_cp("parallel", "arbitrary"),
        name="rwkv_scan",
    )(r, lwf, k, v, kk, a, r, lwb, k, v, kk, a)


def _even_out_kernel(x_ref, ogf, ogb, gate_ref, orf, orb, bonus_ref, grw_ref, glag, lng, lnb, ones128, ones64,
                     w_ref, gl, gc, o_ref, *, tm, n_ctx):
    og = ogf[0] + ogb[0]
    ms = _split_dot_r(og * og, ones128[...], 2) * (1.0 / GLA_DV)
    gt = gate_ref[0]
    y_gla = og * lax.rsqrt(ms + 1e-6) * glag[...] * (gt * jax.nn.sigmoid(gt))
    of = orf[0] + orb[0]
    o64 = ones64[...]
    mean = _split_dot_r(of, o64, 3) * (1.0 / RW_DH)
    cen = of - mean
    var = _split_dot_r(cen * cen, o64, 2) * (1.0 / RW_DH)
    y_rw = (cen * lax.rsqrt(var + RW_GN_EPS) * lng[...] + lnb[...] + bonus_ref[0]) * grw_ref[0]
    y = jnp.concatenate([y_gla, y_rw], axis=1).astype(BF16)
    rows = pl.program_id(1) * tm + lax.broadcasted_iota(jnp.int32, (tm, 1), 0)
    gate = jnp.where(rows < n_ctx, gc[0], gl[0])
    o_ref[0] = x_ref[0] + gate * jnp.dot(y, w_ref[...], preferred_element_type=F32)


def _even_out(xs, og_f, og_b, z_gla, or_f, or_b, bonus, g_rw, p, mod, n_ctx, *, tm):
    B, S, D = xs.shape
    W = RW_W
    row = lambda width: pl.BlockSpec((1, tm, width), lambda b, j: (b, j, 0))
    full = lambda shape: pl.BlockSpec(shape, lambda b, j: (0,) * len(shape))
    return pl.pallas_call(
        functools.partial(_even_out_kernel, tm=tm, n_ctx=n_ctx),
        grid=(B, S // tm),
        in_specs=[row(D), row(GLA_V), row(GLA_V),
                  pl.BlockSpec((1, tm, GLA_V), lambda b, j: (b, j, (2 * GLA_QK + GLA_V) // GLA_V)),
                  row(W), row(W), row(W), row(W),
                  full((1, GLA_V)), full((1, W)), full((1, W)), full((GLA_V, GLA_V)), full((W, W)),
                  full((GLA_V + W, D)), *_mod_specs(B, D, 2, 2)],
        out_specs=row(D),
        out_shape=jax.ShapeDtypeStruct((B, S, D), F32),
        compiler_params=_cp("parallel", "parallel"),
        name="even_out",
    )(xs, og_f, og_b, z_gla, or_f, or_b, bonus, g_rw, p["gla_g"], p["ln_g"], p["ln_b"], p["ones128"], p["ones64"],
      p["w_out"], mod, mod)


def _odd_prep_kernel(z_ref, cos_ref, sin_ref, qg_ref, kg_ref, ones64, qc_o, kc_o, vc_o, qd_o, kd_o, vd_o):
    z = z_ref[0]
    cos, sin = cos_ref[...], sin_ref[...]
    o64 = ones64[...]
    nq = C_HEADS * HEAD_DIM
    nk = C_KV * HEAD_DIM

    def norm_rope(x, g, width):
        ms = _split_dot_r(x * x, o64[:width, :width], 2) * (1.0 / HEAD_DIM)
        y = x * lax.rsqrt(ms + 1e-6) * g
        lane = lax.broadcasted_iota(jnp.int32, (1, width), 1)
        first = (lane % (HEAD_DIM // 2)) < (HEAD_DIM // 4)
        swapped = jnp.where(first, pltpu.roll(y, width - HEAD_DIM // 4, 1), pltpu.roll(y, HEAD_DIM // 4, 1))
        return y * cos[:, :width] + swapped * sin[:, :width]

    qc_o[0] = (norm_rope(z[:, 0:nq], qg_ref[...], nq) * HEAD_DIM ** -0.5).astype(BF16)
    kc_o[0] = norm_rope(z[:, nq:nq + nk], kg_ref[:, :nk], nk).astype(BF16)
    vc_o[0] = z[:, nq + nk:nq + 2 * nk].astype(BF16)
    base = nq + 2 * nk
    nd = NA_HEADS * HEAD_DIM
    qd_o[0] = (z[:, base:base + nd] * HEAD_DIM ** -0.5).astype(BF16)
    kd_o[0] = z[:, base + nd:base + 2 * nd].astype(BF16)
    vd_o[0] = z[:, base + 2 * nd:base + 3 * nd].astype(BF16)


def _odd_prep(z, cos_t, sin_t, qg, kg, ones64, *, tt):
    B, S, Z = z.shape
    nq, nk, nd = C_HEADS * HEAD_DIM, C_KV * HEAD_DIM, NA_HEADS * HEAD_DIM
    full = lambda shape: pl.BlockSpec(shape, lambda b, j: (0,) * len(shape))
    row = lambda width: pl.BlockSpec((1, tt, width), lambda b, j: (b, j, 0))
    widths = (nq, nk, nk, nd, nd, nd)
    return pl.pallas_call(
        _odd_prep_kernel,
        grid=(B, S // tt),
        in_specs=[row(Z), pl.BlockSpec((tt, nq), lambda b, j: (j, 0)), pl.BlockSpec((tt, nq), lambda b, j: (j, 0)),
                  full((1, nq)), full((1, nq)), full((nq, nq))],
        out_specs=[row(w) for w in widths],
        out_shape=[jax.ShapeDtypeStruct((B, S, w), BF16) for w in widths],
        compiler_params=_cp("parallel", "parallel"),
        name="odd_prep",
    )(z, cos_t, sin_t, qg, kg, ones64)


def _pair_attn_kernel(q_ref, k_ref, v_ref, o_ref, *, shared_kv):
    q = q_ref[0]
    tq = q.shape[0]
    ngroups = q.shape[1] // LANE
    lane = lax.broadcasted_iota(jnp.int32, (1, LANE), 1)
    masks = (lane < HEAD_DIM, lane >= HEAD_DIM)
    zero = jnp.zeros((), q.dtype)
    if shared_kv:
        kp, vp = k_ref[0], v_ref[0]
        lhs = [jnp.concatenate([jnp.where(masks[hm], q[:, g * LANE:(g + 1) * LANE], zero) for g in range(ngroups)],
                               axis=0) for hm in range(2)]
        s = [_dot_nt(lhs[hm], kp) for hm in range(2)]
        p = [jnp.exp(s[hm] - jnp.max(s[hm], axis=-1, keepdims=True)) for hm in range(2)]
        den = [jnp.sum(p[hm], axis=-1, keepdims=True) for hm in range(2)]
        o = [_dot(p[hm], vp) / den[hm] for hm in range(2)]
        for g in range(ngroups):
            rows = slice(g * tq, (g + 1) * tq)
            o_ref[0, :, g * LANE:(g + 1) * LANE] = jnp.where(masks[0], o[0][rows], o[1][rows]).astype(o_ref.dtype)
    else:
        for g in range(ngroups):
            qp = q[:, g * LANE:(g + 1) * LANE]
            kp = k_ref[0, :, g * LANE:(g + 1) * LANE]
            vp = v_ref[0, :, g * LANE:(g + 1) * LANE]
            lhs = jnp.concatenate([jnp.where(masks[0], qp, zero), jnp.where(masks[1], qp, zero)], axis=0)
            s = _dot_nt(lhs, kp)
            p = jnp.exp(s - jnp.max(s, axis=-1, keepdims=True))
            o = _dot(p, vp) / jnp.sum(p, axis=-1, keepdims=True)
            o_ref[0, :, g * LANE:(g + 1) * LANE] = jnp.where(masks[0], o[:tq], o[tq:]).astype(o_ref.dtype)


def _pair_attention(q, k, v, *, q_row0, n_q, n_k, tq, shared_kv, name):
    B, S, QW = q.shape
    KW = k.shape[2]
    return pl.pallas_call(
        functools.partial(_pair_attn_kernel, shared_kv=shared_kv),
        grid=(B, n_q // tq),
        in_specs=[pl.BlockSpec((1, tq, QW), lambda b, i: (b, q_row0 // tq + i, 0)),
                  pl.BlockSpec((1, n_k, KW), lambda b, i: (b, 0, 0)),
                  pl.BlockSpec((1, n_k, KW), lambda b, i: (b, 0, 0))],
        out_specs=pl.BlockSpec((1, tq, QW), lambda b, i: (b, i, 0)),
        out_shape=jax.ShapeDtypeStruct((B, n_q, QW), BF16),
        compiler_params=_cp("parallel", "parallel"),
        name=name,
    )(q, k, v)


NA_ROWS_PER_STEP = 2


def _na_kernel(q_ref, k_ref, v_ref, *rest, n_ctx, n_rows, kh, rps):
    bias_refs, o_ref = rest[:rps], rest[rps]
    nkeys = kh * GRID_W
    lane = lax.broadcasted_iota(jnp.int32, (1, LANE), 1)
    m0 = lane < HEAD_DIM
    zero = jnp.zeros((), q_ref.dtype)
    starts = []
    for j in range(rps):
        r = pl.program_id(1) * rps + j
        rs = jnp.clip(r - kh // 2, 0, n_rows - kh)
        starts.append(pl.multiple_of(n_ctx + rs * GRID_W, GRID_W))
    es = [(j, g) for j in range(rps) for g in range(NA_HEADS // 2)]
    ids = range(len(es))
    gs = lambda g: slice(g * LANE, (g + 1) * LANE)
    qrow = lambda j, g: q_ref[0, j * GRID_W:(j + 1) * GRID_W, gs(g)]
    lhs = [jnp.concatenate([jnp.where(m0, qrow(j, g), zero), jnp.where(m0, zero, qrow(j, g))], axis=0)
           for j, g in es]
    s_nb = [_dot_nt(lhs[i], k_ref[0, pl.ds(starts[j], nkeys), gs(g)]) for i, (j, g) in enumerate(es)]
    s_cx = [_dot_nt(lhs[i], k_ref[0, 0:n_ctx, gs(g)]) for i, (j, g) in enumerate(es)]
    s_nb = [s_nb[i] + jnp.concatenate([bias_refs[j][0, 2 * g], bias_refs[j][0, 2 * g + 1]], axis=0)
            for i, (j, g) in enumerate(es)]
    m = [jnp.maximum(jnp.max(s_nb[i], axis=-1, keepdims=True), jnp.max(s_cx[i], axis=-1, keepdims=True))
         for i in ids]
    p_nb = [jnp.exp(s_nb[i] - m[i]) for i in ids]
    p_cx = [jnp.exp(s_cx[i] - m[i]) for i in ids]
    den = [jnp.sum(p_nb[i], axis=-1, keepdims=True) + jnp.sum(p_cx[i], axis=-1, keepdims=True) for i in ids]
    o_nb = [_dot(p_nb[i], v_ref[0, pl.ds(starts[j], nkeys), gs(g)]) for i, (j, g) in enumerate(es)]
    o_cx = [_dot(p_cx[i], v_ref[0, 0:n_ctx, gs(g)]) for i, (j, g) in enumerate(es)]
    for i, (j, g) in enumerate(es):
        o = (o_nb[i] + o_cx[i]) / den[i]
        o_ref[0, j * GRID_W:(j + 1) * GRID_W, gs(g)] = jnp.where(m0, o[:GRID_W], o[GRID_W:]).astype(o_ref.dtype)


def _na_attention(qd, kd, vd, bias_tab, n_ctx):
    B, S, W = qd.shape
    T = S - n_ctx
    n_rows = T // GRID_W
    kh = min(NA_KH, n_rows)
    nkeys = kh * GRID_W
    rps = NA_ROWS_PER_STEP if n_rows % NA_ROWS_PER_STEP == 0 and n_ctx % (NA_ROWS_PER_STEP * GRID_W) == 0 else 1
    tq = rps * GRID_W

    def bias_spec(j):
        def idx(b, i):
            r = i * rps + j
            return (jnp.clip(r - kh // 2, 0, n_rows - kh) - r + kh - 1, 0, 0, 0)
        return pl.BlockSpec((1, NA_HEADS, GRID_W, nkeys), idx)

    return pl.pallas_call(
        functools.partial(_na_kernel, n_ctx=n_ctx, n_rows=n_rows, kh=kh, rps=rps),
        grid=(B, n_rows // rps),
        in_specs=[pl.BlockSpec((1, tq, W), lambda b, i: (b, n_ctx // tq + i, 0)),
                  pl.BlockSpec((1, S, W), lambda b, i: (b, 0, 0)),
                  pl.BlockSpec((1, S, W), lambda b, i: (b, 0, 0)),
                  *[bias_spec(j) for j in range(rps)]],
        out_specs=pl.BlockSpec((1, tq, W), lambda b, i: (b, i, 0)),
        out_shape=jax.ShapeDtypeStruct((B, T, W), BF16),
        compiler_params=_cp("parallel", "arbitrary"),
        name="na_attention",
    )(qd, kd, vd, *([bias_tab] * rps))


def _na_bias_table(rpb, n_rows):
    kh = min(NA_KH, n_rows)
    kw = min(NA_KW, GRID_W)
    col = np.arange(GRID_W)
    start = np.clip(col - kw // 2, 0, GRID_W - kw)
    in_win = (col[None, :] >= start[:, None]) & (col[None, :] < start[:, None] + kw)
    dc = np.clip(col[None, :] - col[:, None], -(NA_KW - 1), NA_KW - 1) + NA_KW - 1
    bias_cols = rpb[:, :, dc]
    slabs = []
    for st in range(kh):
        sl = bias_cols[:, st + NA_KH - kh:st + NA_KH - kh + kh]
        sl = jnp.where(in_win[None, None], sl, NEG).transpose(0, 2, 1, 3)
        slabs.append(sl.reshape(NA_HEADS, GRID_W, kh * GRID_W))
    return jnp.stack(slabs).astype(F32)


def _proj_res_kernel(x_ref, y_ref, w_ref, gl, gc, o_ref, *, tm, n_ctx):
    rows = pl.program_id(1) * tm + lax.broadcasted_iota(jnp.int32, (tm, 1), 0)
    gate = jnp.where(rows < n_ctx, gc[0], gl[0])
    o_ref[0] = x_ref[0] + gate * jnp.dot(y_ref[0], w_ref[...], preferred_element_type=F32)


def _proj_res(xs, y, w, mod, n_ctx, *, tm):
    B, S, D = xs.shape
    K = y.shape[2]
    row = lambda width: pl.BlockSpec((1, tm, width), lambda b, j: (b, j, 0))
    return pl.pallas_call(
        functools.partial(_proj_res_kernel, tm=tm, n_ctx=n_ctx),
        grid=(B, S // tm),
        in_specs=[row(D), row(K), pl.BlockSpec((K, D), lambda b, j: (0, 0)), *_mod_specs(B, D, 2, 2)],
        out_specs=row(D),
        out_shape=jax.ShapeDtypeStruct((B, S, D), F32),
        compiler_params=_cp("parallel", "parallel"),
        name="proj_res",
    )(xs, y, w, mod, mod)


def _final_norm_kernel(x_ref, g_ref, o_ref):
    x = x_ref[0]
    o_ref[0] = x * lax.rsqrt(jnp.mean(x * x, axis=-1, keepdims=True) + 1e-6) * g_ref[...]


def _final_norm(xs, g, n_ctx, *, tm):
    B, S, D = xs.shape
    T = S - n_ctx
    return pl.pallas_call(
        _final_norm_kernel,
        grid=(B, T // tm),
        in_specs=[pl.BlockSpec((1, tm, D), lambda b, j: (b, n_ctx // tm + j, 0)),
                  pl.BlockSpec((1, D), lambda b, j: (0, 0))],
        out_specs=pl.BlockSpec((1, tm, D), lambda b, j: (b, j, 0)),
        out_shape=jax.ShapeDtypeStruct((B, T, D), F32),
        compiler_params=_cp("parallel", "parallel"),
        name="final_norm",
    )(xs, g.reshape(1, D))


def _block_ones(width, block):
    idx = np.arange(width) // block
    return jnp.asarray(idx[:, None] == idx[None, :], BF16)


def _pad_cols(w, width):
    return jnp.pad(w, ((0, 0), (0, width - w.shape[1])))


def _pad_rows(w, height):
    return jnp.pad(w, ((0, height - w.shape[0]), (0, 0)))


def _even_params(j, ev_w_in, ev_w_out, gla_a_up, gla_a_bias, gla_norm_g, rw_mu, rw_w0, rw_w_up, rw_a0, rw_a_up,
                 rw_g_up, rw_k_k, rw_k_a, rw_r_k, rw_ln_g, rw_ln_b):
    w_in = ev_w_in[j]
    D = w_in.shape[0]
    gq = 2 * GLA_QK + 2 * GLA_V
    w_gla = jnp.concatenate([w_in[:, :gq], _pad_cols(w_in[:, gq:gq + GLA_RANK], LANE),
                             _pad_cols(w_in[:, gq + GLA_RANK:gq + 2 * GLA_RANK], LANE)], axis=1)
    o = gq + 2 * GLA_RANK
    sizes = (3 * RW_W, RW_DECAY_RANK, RW_DECAY_RANK, RW_A_RANK, RW_G_RANK)
    offs = np.cumsum((0,) + sizes)
    pieces = [w_in[:, o:o + 3 * RW_W]] + [_pad_cols(w_in[:, o + offs[i]:o + offs[i + 1]], LANE) for i in range(1, 5)]
    w_rw = jnp.concatenate(pieces, axis=1)
    mu = rw_mu[j]
    mu_pieces = [mu[:, :3 * RW_W]] + [_pad_cols(mu[:, offs[i]:offs[i + 1]], LANE) for i in range(1, 5)]
    return {
        "w_gla": w_gla.astype(BF16), "w_rw": w_rw.astype(BF16), "w_out": ev_w_out[j].astype(BF16),
        "gla_aup": jnp.stack([_pad_rows(gla_a_up[j, d], LANE) for d in range(2)]).astype(BF16),
        "gla_bias": gla_a_bias[j], "gla_g": jnp.tile(gla_norm_g[j], GLA_HEADS).reshape(1, GLA_V),
        "mu": jnp.concatenate(mu_pieces, axis=1), "w0": rw_w0[j],
        "wup": jnp.stack([_pad_rows(rw_w_up[j, d], LANE) for d in range(2)]).astype(BF16),
        "a0": rw_a0[j].reshape(1, RW_W), "aup": _pad_rows(rw_a_up[j], LANE).astype(BF16),
        "gup": _pad_rows(rw_g_up[j], LANE).astype(BF16),
        "k_k": rw_k_k[j].reshape(1, RW_W), "k_a": rw_k_a[j].reshape(1, RW_W), "r_k": rw_r_k[j].reshape(1, RW_W),
        "ln_g": rw_ln_g[j].reshape(1, RW_W), "ln_b": rw_ln_b[j].reshape(1, RW_W),
        "ones64": _block_ones(RW_W, RW_DH), "ones128": _block_ones(GLA_V, GLA_DV),
    }


def _gqa_head_order():
    per_kv = C_HEADS // C_KV
    order = []
    for jj in range(per_kv):
        for kv in range(C_KV):
            h = kv * per_kv + jj
            order.extend(range(h * HEAD_DIM, (h + 1) * HEAD_DIM))
    return np.asarray(order)


def _rope_tables(n_ctx, T, width):
    t = jnp.arange(T)
    pos = jnp.stack([t // GRID_W, t % GRID_W], axis=-1).astype(F32)
    half = HEAD_DIM // 2
    inv = ROPE_THETA ** (-jnp.arange(0, half, 2, dtype=F32) / half)
    ang = pos[:, :, None] * inv
    cos, sin = jnp.cos(ang), jnp.sin(ang)
    cos_h = jnp.concatenate([cos, cos], axis=-1).reshape(T, HEAD_DIM)
    sin_h = jnp.concatenate([-sin, sin], axis=-1).reshape(T, HEAD_DIM)
    cos_h = jnp.concatenate([jnp.ones((n_ctx, HEAD_DIM), F32), cos_h], axis=0)
    sin_h = jnp.concatenate([jnp.zeros((n_ctx, HEAD_DIM), F32), sin_h], axis=0)
    reps = width // HEAD_DIM
    return jnp.tile(cos_h, (1, reps)), jnp.tile(sin_h, (1, reps))


def _even_layer(xs, h, mod, n_ctx, p):
    z_gla, z_rw = _project(h, [p["w_gla"], p["w_rw"]], tm=_row_tile(xs.shape[1], ROW_CAP_FFN), name="even_in")
    og_f, og_b = _gla_scan(z_gla, p["gla_aup"], p["gla_bias"], n_ctx)
    r, lwf, lwb, k, v, kk, a, g_rw, bonus = _rw_features(z_rw, p, n_ctx)
    or_f, or_b = _rw_scan(r, lwf, lwb, k, v, kk, a, n_ctx)
    return _even_out(xs, og_f, og_b, z_gla, or_f, or_b, bonus, g_rw, p, mod, n_ctx,
                     tm=_row_tile(xs.shape[1], ROW_CAP_ELEMENTWISE))


def _odd_layer(xs, h, mod, n_ctx, p, need_ctx):
    B, S, D = xs.shape
    T = S - n_ctx
    (z,) = _project(h, [p["w_in"]], tm=_row_tile(S, ROW_CAP_FFN), name="odd_in")
    qc, kc, vc, qd, kd, vd = _odd_prep(z, p["cos"], p["sin"], p["qg"], p["kg"], p["ones64"],
                                       tt=_row_tile(S, ROW_CAP_ELEMENTWISE))
    y_gqa = _pair_attention(qc, kc, vc, q_row0=n_ctx, n_q=T, n_k=S, tq=128, shared_kv=True, name="gqa")
    y_na = _na_attention(qd, kd, vd, p["bias_tab"], n_ctx)
    parts_g, parts_n = [y_gqa], [y_na]
    if need_ctx:
        parts_g.insert(0, _pair_attention(qc, kc, vc, q_row0=0, n_q=n_ctx, n_k=n_ctx, tq=n_ctx, shared_kv=True,
                                          name="gqa_ctx"))
        parts_n.insert(0, _pair_attention(qd, kd, vd, q_row0=0, n_q=n_ctx, n_k=n_ctx, tq=n_ctx, shared_kv=False,
                                          name="na_ctx"))
    else:
        zeros = jnp.zeros((B, n_ctx, y_gqa.shape[2]), BF16)
        parts_g.insert(0, zeros)
        parts_n.insert(0, zeros)
    y = jnp.concatenate([jnp.concatenate(parts_g, axis=1), jnp.concatenate(parts_n, axis=1)], axis=2)
    return _proj_res(xs, y, p["w_out"], mod, n_ctx, tm=_row_tile(S, ROW_CAP_MATMUL))


def kernel(x, c, ctx, c_ctx, w_mod, b_mod, norm1_g, norm2_g, ffn_w13, ffn_w2, ev_w_in, ev_w_out, gla_a_up,
           gla_a_bias, gla_norm_g, rw_mu, rw_w0, rw_w_up, rw_a0, rw_a_up, rw_g_up, rw_k_k, rw_k_a, rw_r_k, rw_ln_g,
           rw_ln_b, od_w_in, od_w_out, cq_norm_g, ck_norm_g, na_rpb, final_g):
    B, T, D = x.shape
    n_ctx = ctx.shape[1]
    S = n_ctx + T
    xs = jnp.concatenate([ctx, x], axis=1)
    mods = _mod_vectors(c, c_ctx, w_mod, b_mod)
    nq, nk = C_HEADS * HEAD_DIM, C_KV * HEAD_DIM
    order = _gqa_head_order()
    cos_t, sin_t = _rope_tables(n_ctx, T, nq)
    ones64 = _block_ones(nq, HEAD_DIM)
    h = _norm_mod(xs, norm1_g[0], mods[0], n_ctx, tm=_row_tile(S, ROW_CAP_MATMUL))
    for i in range(DEPTH):
        j = i // 2
        mod = mods[i]
        if i % 2 == 0:
            p = _even_params(j, ev_w_in, ev_w_out, gla_a_up, gla_a_bias, gla_norm_g, rw_mu, rw_w0, rw_w_up, rw_a0,
                             rw_a_up, rw_g_up, rw_k_k, rw_k_a, rw_r_k, rw_ln_g, rw_ln_b)
            xs = _even_layer(xs, h, mod, n_ctx, p)
        else:
            w_in = od_w_in[j]
            w_out = od_w_out[j]
            p = {
                "w_in": jnp.concatenate([w_in[:, :nq][:, order], w_in[:, nq:]], axis=1).astype(BF16),
                "w_out": jnp.concatenate([w_out[:nq][order], w_out[nq:]], axis=0).astype(BF16),
                "cos": cos_t, "sin": sin_t, "ones64": ones64,
                "qg": jnp.tile(cq_norm_g[j], C_HEADS).reshape(1, nq),
                "kg": jnp.tile(ck_norm_g[j], C_HEADS).reshape(1, nq),
                "bias_tab": _na_bias_table(na_rpb[j], T // GRID_W),
            }
            xs = _odd_layer(xs, h, mod, n_ctx, p, need_ctx=i < DEPTH - 1)
        next_norm = (norm1_g[i + 1], mods[i + 1]) if i + 1 < DEPTH else None
        xs, h = _ffn(xs, norm2_g[i], mod, ffn_w13[i].astype(BF16), ffn_w2[i].astype(BF16), n_ctx,
                     tm=_row_tile(S, ROW_CAP_FFN), tf=FFN_COL_TILE, next_norm=next_norm)
    return _final_norm(xs, final_g, n_ctx, tm=_row_tile(n_ctx, ROW_CAP_ELEMENTWISE))
```

```python
import functools
import math

import jax
import jax.numpy as jnp
import numpy as np
from jax import lax
from jax.experimental import pallas as pl
from jax.experimental.pallas import tpu as pltpu

F32 = jnp.float32
BF16 = jnp.bfloat16

DEPTH = 4
GRID_W = 64
CHUNK = 64
LANE = 128

GLA_HEADS, GLA_DK, GLA_DV, GLA_RANK, GLA_TAU = 4, 64, 128, 16, 16.0
GLA_QK, GLA_V = GLA_HEADS * GLA_DK, GLA_HEADS * GLA_DV
GLA_Z = 2 * GLA_QK + 2 * GLA_V + 2 * LANE

RW_HEADS, RW_DH, RW_W = 8, 64, 512
RW_DECAY_RANK, RW_A_RANK, RW_G_RANK = 32, 32, 96
RW_GN_EPS = 64e-5
RW_Z = 3 * RW_W + 4 * LANE

HEAD_DIM, C_HEADS, C_KV, NA_HEADS, NA_KH, NA_KW = 64, 8, 2, 8, 8, 16
ROPE_THETA = 10000.0
NEG = -1e30

VMEM_LIMIT = 56 * 1024 * 1024
ROW_CAP_MATMUL = 1152
ROW_CAP_FFN = 576
ROW_CAP_ELEMENTWISE = 256
FFN_COL_TILE = 256


def _row_tile(n_rows, cap):
    return max(d for d in range(8, cap + 1, 8) if n_rows % d == 0)


def _cp(*sem):
    return pltpu.CompilerParams(dimension_semantics=sem, vmem_limit_bytes=VMEM_LIMIT)


def _dot(a, b):
    return jnp.dot(a.astype(BF16), b.astype(BF16), preferred_element_type=F32)


def _dot_nt(a, b):
    return lax.dot_general(a.astype(BF16), b.astype(BF16), (((1,), (1,)), ((), ())), preferred_element_type=F32)


def _dot_tn(a, b):
    return lax.dot_general(a.astype(BF16), b.astype(BF16), (((0,), (0,)), ((), ())), preferred_element_type=F32)


def _split_dot(m, x, parts):
    acc = None
    rem = x
    for _ in range(parts):
        piece = rem.astype(BF16)
        rem = rem - piece.astype(F32)
        t = jnp.dot(m, piece, preferred_element_type=F32)
        acc = t if acc is None else acc + t
    return acc


def _split_dot_r(x, m, parts):
    acc = None
    rem = x
    for _ in range(parts):
        piece = rem.astype(BF16)
        rem = rem - piece.astype(F32)
        t = jnp.dot(piece, m, preferred_element_type=F32)
        acc = t if acc is None else acc + t
    return acc


def _rms_mod(x, g, sc_l, sh_l, sc_c, sh_c, row0, n_ctx):
    y = x * lax.rsqrt(jnp.mean(x * x, axis=-1, keepdims=True) + 1e-6) * g
    rows = row0 + lax.broadcasted_iota(jnp.int32, (x.shape[0], 1), 0)
    is_ctx = rows < n_ctx
    sc = jnp.where(is_ctx, sc_c, sc_l)
    sh = jnp.where(is_ctx, sh_c, sh_l)
    return y * (1.0 + sc) + sh


def _tri_masks(n, reverse):
    ri = lax.broadcasted_iota(jnp.int32, (n, n), 0)
    ci = lax.broadcasted_iota(jnp.int32, (n, n), 1)
    if reverse:
        return ci >= ri, ci > ri
    return ci <= ri, ci < ri


INV_BASE = 16


def _block_ids(n):
    ri = lax.broadcasted_iota(jnp.int32, (n, n), 0)
    ci = lax.broadcasted_iota(jnp.int32, (n, n), 1)
    return ri, ci


def _unit_tri_inverse(a, blk):
    ri, ci = blk
    n = a[0].shape[0]
    ids = range(len(a))
    same = lambda bs: (ri // bs) == (ci // bs)
    base = same(INV_BASE)
    eye = (ri == ci).astype(F32)
    p = [jnp.where(base, x, 0.0) for x in a]
    t = [eye + x for x in p]
    m = 1
    while 2 * m < INV_BASE:
        p = [_dot(x, x) for x in p]
        tp = [_dot(t[i], p[i]) for i in ids]
        t = [t[i] + tp[i] for i in ids]
        m *= 2
    bs = INV_BASE
    while bs < n:
        mask = same(2 * bs) & jnp.logical_not(same(bs))
        ta = [_dot(t[i], jnp.where(mask, a[i], 0.0)) for i in ids]
        tat = [_dot(ta[i], t[i]) for i in ids]
        t = [t[i] + tat[i] for i in ids]
        bs *= 2
    return t


def _mod_kernel(s_ref, w_ref, b_ref, o_ref):
    s = s_ref[...]
    s = s * jax.nn.sigmoid(s)
    o_ref[0] = _dot(s, w_ref[0]) + b_ref[0]


def _mod_vectors(c, c_ctx, w_mod, b_mod):
    B, D = c.shape
    bp = -(-(B + 1) // 8) * 8
    s_in = jnp.zeros((bp, D), F32).at[:B].set(c).at[B].set(c_ctx)
    n6 = w_mod.shape[-1]
    tn = 1536
    out = pl.pallas_call(
        _mod_kernel,
        grid=(DEPTH, n6 // tn),
        in_specs=[pl.BlockSpec((bp, D), lambda l, n: (0, 0)),
                  pl.BlockSpec((1, D, tn), lambda l, n: (l, 0, n)),
                  pl.BlockSpec((1, 1, tn), lambda l, n: (l, 0, n))],
        out_specs=pl.BlockSpec((1, bp, tn), lambda l, n: (l, 0, n)),
        out_shape=jax.ShapeDtypeStruct((DEPTH, bp, n6), F32),
        compiler_params=_cp("parallel", "parallel"),
        name="mod_vectors",
    )(s_in, w_mod, b_mod.reshape(DEPTH, 1, n6))
    return out.reshape(DEPTH, bp, 1, n6)


def _mod_specs(B, D, idx, grid_rank):
    if grid_rank == 2:
        return [pl.BlockSpec((1, 1, D), lambda b, j: (b, 0, idx)),
                pl.BlockSpec((1, 1, D), lambda b, j: (B, 0, idx))]
    return [pl.BlockSpec((1, 1, D), lambda b, j, n: (b, 0, idx)),
            pl.BlockSpec((1, 1, D), lambda b, j, n: (B, 0, idx))]


def _norm_mod_kernel(x_ref, g_ref, shl, shc, scl, scc, o_ref, *, tm, n_ctx):
    h = _rms_mod(x_ref[0], g_ref[...], scl[0], shl[0], scc[0], shc[0], pl.program_id(1) * tm, n_ctx)
    o_ref[0] = h.astype(BF16)


def _norm_mod(xs, g, mod, n_ctx, *, tm):
    B, S, D = xs.shape
    row = pl.BlockSpec((1, tm, D), lambda b, j: (b, j, 0))
    return pl.pallas_call(
        functools.partial(_norm_mod_kernel, tm=tm, n_ctx=n_ctx),
        grid=(B, S // tm),
        in_specs=[row, pl.BlockSpec((1, D), lambda b, j: (0, 0)), *_mod_specs(B, D, 0, 2), *_mod_specs(B, D, 1, 2)],
        out_specs=row,
        out_shape=jax.ShapeDtypeStruct((B, S, D), BF16),
        compiler_params=_cp("parallel", "parallel"),
        name="norm_mod",
    )(xs, g.reshape(1, D), mod, mod, mod, mod)


def _proj_kernel(h_ref, *refs):
    n = len(refs) // 2
    h = h_ref[0]
    for w_ref, o_ref in zip(refs[:n], refs[n:]):
        o_ref[0] = jnp.dot(h, w_ref[...], preferred_element_type=F32)


def _project(h, ws, *, tm, name):
    B, S, D = h.shape
    row = lambda width: pl.BlockSpec((1, tm, width), lambda b, j: (b, j, 0))
    return pl.pallas_call(
        _proj_kernel,
        grid=(B, S // tm),
        in_specs=[row(D)] + [_resident(w.shape) for w in ws],
        out_specs=[row(w.shape[1]) for w in ws],
        out_shape=[jax.ShapeDtypeStruct((B, S, w.shape[1]), F32) for w in ws],
        compiler_params=_cp("parallel", "parallel"),
        name=name,
    )(h, *ws)


def _ffn_kernel(x_ref, g_ref, shl, shc, scl, scc, gl, gc, w13_ref, w2_ref, *rest, tm, tf, n_ctx, emit_next):
    if emit_next:
        gn_ref, nshl, nshc, nscl, nscc, o_ref, hn_ref = rest
    else:
        (o_ref,) = rest
    x = x_ref[0]
    row0 = pl.program_id(1) * tm
    h = _rms_mod(x, g_ref[...], scl[0], shl[0], scc[0], shc[0], row0, n_ctx).astype(BF16)
    n_hidden = w2_ref.shape[0]
    acc = None
    for f in range(n_hidden // tf):
        a = jnp.dot(h, w13_ref[:, f * tf:(f + 1) * tf], preferred_element_type=F32)
        b = jnp.dot(h, w13_ref[:, n_hidden + f * tf:n_hidden + (f + 1) * tf], preferred_element_type=F32)
        u = (a * jax.nn.sigmoid(a) * b).astype(BF16)
        t = jnp.dot(u, w2_ref[f * tf:(f + 1) * tf, :], preferred_element_type=F32)
        acc = t if acc is None else acc + t
    rows = row0 + lax.broadcasted_iota(jnp.int32, (tm, 1), 0)
    gate = jnp.where(rows < n_ctx, gc[0], gl[0])
    y = x + gate * acc
    o_ref[0] = y
    if emit_next:
        hn_ref[0] = _rms_mod(y, gn_ref[...], nscl[0], nshl[0], nscc[0], nshc[0], row0, n_ctx).astype(BF16)


def _resident(shape):
    return pl.BlockSpec(shape, lambda b, j: (0,) * len(shape), pipeline_mode=pl.Buffered(1))


def _ffn(xs, g, mod, w13, w2, n_ctx, *, tm, tf, next_norm=None):
    B, S, D = xs.shape
    row = pl.BlockSpec((1, tm, D), lambda b, j: (b, j, 0))
    vec = pl.BlockSpec((1, D), lambda b, j: (0, 0))
    in_specs = [row, vec, *_mod_specs(B, D, 3, 2), *_mod_specs(B, D, 4, 2), *_mod_specs(B, D, 5, 2),
                _resident(w13.shape), _resident(w2.shape)]
    args = [xs, g.reshape(1, D), mod, mod, mod, mod, mod, mod, w13, w2]
    out_specs, out_shape = [row], [jax.ShapeDtypeStruct((B, S, D), F32)]
    if next_norm is not None:
        g_next, mod_next = next_norm
        in_specs += [vec, *_mod_specs(B, D, 0, 2), *_mod_specs(B, D, 1, 2)]
        args += [g_next.reshape(1, D), mod_next, mod_next, mod_next, mod_next]
        out_specs.append(row)
        out_shape.append(jax.ShapeDtypeStruct((B, S, D), BF16))
    out = pl.pallas_call(
        functools.partial(_ffn_kernel, tm=tm, tf=tf, n_ctx=n_ctx, emit_next=next_norm is not None),
        grid=(B, S // tm),
        in_specs=in_specs,
        out_specs=out_specs,
        out_shape=out_shape,
        compiler_params=_cp("parallel", "parallel"),
        name="ffn",
    )(*args)
    return (out[0], out[1]) if next_norm is not None else (out[0], None)


def _chunk_index(s, nc_ctx, nc, reverse):
    if not reverse:
        return s
    return jnp.where(s < nc_ctx, nc_ctx - 1 - s, nc + nc_ctx - 1 - s)


def _gla_kernel(qkv_f, ad_f, qkv_b, ad_b, aup_ref, bias_ref, of_ref, ob_ref, st_scr):
    @pl.when(pl.program_id(1) == 0)
    def _():
        st_scr[...] = jnp.zeros_like(st_scr)

    L = CHUNK
    nb = qkv_f.shape[0]
    streams = [(bb, d) for bb in range(nb) for d in range(2)]
    qb, kb, kt, gt, v, incl = [], [], [], [], [], []
    for bb, d in streams:
        qkv_ref, ad_ref = ((qkv_f, ad_f), (qkv_b, ad_b))[d]
        m, _ = _tri_masks(L, d == 1)
        ad = ad_ref[bb][:, d * LANE:(d + 1) * LANE]
        y = _dot(ad, aup_ref[d]) + bias_ref[d:d + 1, :]
        la = (jnp.minimum(y, 0.0) - jnp.log1p(jnp.exp(-jnp.abs(y)))) * (1.0 / GLA_TAU)
        cs = _split_dot(m.astype(BF16), la, 3)
        tot = cs[0:1] if d == 1 else cs[L - 1:L]
        qkv = qkv_ref[bb]
        q = qkv[:, 0:GLA_QK] * GLA_DK ** -0.5
        k = qkv[:, GLA_QK:2 * GLA_QK]
        incl.append(m)
        v.append(qkv[:, 2 * GLA_QK:2 * GLA_QK + GLA_V].astype(BF16))
        qb.append((q * jnp.exp(cs)).astype(BF16))
        kb.append((k * jnp.exp(-cs)).astype(BF16))
        kt.append((k * jnp.exp(tot - cs)).astype(BF16))
        gt.append(jnp.exp(tot))
    es = [(c, h) for c in range(len(streams)) for h in range(GLA_HEADS)]
    ks = lambda h: slice(h * GLA_DK, (h + 1) * GLA_DK)
    vs = lambda h: slice(h * GLA_DV, (h + 1) * GLA_DV)
    st = [st_scr[c, h] for c, h in es]
    att = [jnp.where(incl[c], _dot_nt(qb[c][:, ks(h)], kb[c][:, ks(h)]), 0.0) for c, h in es]
    o_state = [_dot_nt(qb[c][:, ks(h)], st[i]) for i, (c, h) in enumerate(es)]
    o_att = [_dot(att[i], v[c][:, vs(h)]) for i, (c, h) in enumerate(es)]
    upd = [_dot_tn(v[c][:, vs(h)], kt[c][:, ks(h)]) for c, h in es]
    for i, (c, h) in enumerate(es):
        bb, d = streams[c]
        (of_ref, ob_ref)[d][bb, :, vs(h)] = o_att[i] + o_state[i]
        st_scr[c, h] = st[i] * gt[c][:, ks(h)] + upd[i]


GLA_SCAN_BATCH = 4
RW_SCAN_BATCH = 2


def _scan_batch(B, want):
    return want if B % want == 0 else 1


def _scan_index_maps(S, n_ctx):
    nc, nc_ctx = S // CHUNK, n_ctx // CHUNK
    fwd = functools.partial(_chunk_index, nc_ctx=nc_ctx, nc=nc, reverse=False)
    bwd = functools.partial(_chunk_index, nc_ctx=nc_ctx, nc=nc, reverse=True)
    return nc, fwd, bwd


def _gla_scan(z_gla, aup_pad, bias, n_ctx):
    B, S, _ = z_gla.shape
    nc, fwd, bwd = _scan_index_maps(S, n_ctx)
    ad_blk = (2 * GLA_QK + 2 * GLA_V) // (2 * LANE)
    nb = _scan_batch(B, GLA_SCAN_BATCH)
    qkv_spec =lambda cidx: pl.BlockSpec((nb, CHUNK, 2 * GLA_QK + GLA_V), lambda b, s: (b, cidx(s), 0))
    ad_spec = lambda cidx: pl.BlockSpec((nb, CHUNK, 2 * LANE), lambda b, s: (b, cidx(s), ad_blk))
    out_spec = lambda cidx: pl.BlockSpec((nb, CHUNK, GLA_V), lambda b, s: (b, cidx(s), 0))
    return pl.pallas_call(
        _gla_kernel,
        grid=(B // nb, nc),
        in_specs=[qkv_spec(fwd), ad_spec(fwd), qkv_spec(bwd), ad_spec(bwd),
                  pl.BlockSpec((2, LANE, GLA_QK), lambda b, s: (0, 0, 0)),
                  pl.BlockSpec((2, GLA_QK), lambda b, s: (0, 0))],
        out_specs=[out_spec(fwd), out_spec(bwd)],
        out_shape=[jax.ShapeDtypeStruct((B, S, GLA_V), F32)] * 2,
        scratch_shapes=[pltpu.VMEM((2 * nb, GLA_HEADS, GLA_DV, GLA_DK), F32)],
        compiler_params=_cp("parallel", "arbitrary"),
        name="gla_scan",
    )(z_gla, z_gla, z_gla, z_gla, aup_pad, bias)


def _rw_feat_kernel(z_ref, zp_ref, zn_ref, mu_ref, w0_ref, wup_ref, a0_ref, aup_ref, gup_ref, kk_ref, ka_ref,
                    rk_ref, ones_ref, r_o, lwf_o, lwb_o, k_o, v_o, kk_o, a_o, g_o, bonus_o, *, tt):
    j = pl.program_id(1)
    last = pl.num_programs(1) - 1
    z = z_ref[0]
    prev_row = jnp.where(j <= 1, 0.0, zp_ref[0, 7:8, :])
    next_row = jnp.where((j == 0) | (j == last), 0.0, zn_ref[0, 0:1, :])
    mu_p, mu_n = mu_ref[0:1, :], mu_ref[1:2, :]
    zs = z + mu_p * (pltpu.roll(z, 1, 0) - z) + mu_n * (pltpu.roll(z, tt - 1, 0) - z)
    r8 = lax.broadcasted_iota(jnp.int32, (8, 1), 0)
    head = zs[:8] + jnp.where(r8 == 0, mu_p * (prev_row - z[tt - 1:tt]), 0.0)
    tail = zs[tt - 8:] + jnp.where(r8 == 7, mu_n * (next_row - z[0:1]), 0.0)
    zs = jnp.concatenate([head, zs[8:tt - 8], tail], axis=0)

    W = RW_W
    r, k, v = zs[:, 0:W], zs[:, W:2 * W], zs[:, 2 * W:3 * W]
    base = 3 * W
    wd = (zs[:, base:base + LANE], zs[:, base + LANE:base + 2 * LANE])
    ad = zs[:, base + 2 * LANE:base + 3 * LANE]
    gd = zs[:, base + 3 * LANE:base + 4 * LANE]
    for d, lw_o in enumerate((lwf_o, lwb_o)):
        y = w0_ref[d:d + 1, :] + _dot(jnp.tanh(wd[d]), wup_ref[d])
        lw_o[0] = -math.exp(-0.5) * jax.nn.sigmoid(y)
    a = jax.nn.sigmoid(a0_ref[...] + _dot(ad, aup_ref[...]))
    g = _dot(jax.nn.sigmoid(gd), gup_ref[...])
    ones = ones_ref[...]
    kk = k * kk_ref[...]
    ss = _split_dot_r(kk * kk, ones, 2)
    kk = kk / jnp.maximum(jnp.sqrt(ss), 1e-12)
    kmod = k * (1.0 + (a - 1.0) * ka_ref[...])
    bonus = _split_dot_r(r * kmod * rk_ref[...], ones, 2) * v
    r_o[0] = r
    k_o[0] = kmod
    v_o[0] = v
    kk_o[0] = kk
    a_o[0] = a
    g_o[0] = g
    bonus_o[0] = bonus


def _rw_features(z_rw, p, n_ctx):
    B, S, Z = z_rw.shape
    tt = n_ctx
    nt = S // tt
    n8 = S // 8
    W = RW_W
    full = lambda shape: pl.BlockSpec(shape, lambda b, j: (0,) * len(shape))
    out_spec = pl.BlockSpec((1, tt, W), lambda b, j: (b, j, 0))
    return pl.pallas_call(
        functools.partial(_rw_feat_kernel, tt=tt),
        grid=(B, nt),
        in_specs=[pl.BlockSpec((1, tt, Z), lambda b, j: (b, j, 0)),
                  pl.BlockSpec((1, 8, Z), lambda b, j: (b, jnp.maximum(j * (tt // 8) - 1, 0), 0)),
                  pl.BlockSpec((1, 8, Z), lambda b, j: (b, jnp.minimum((j + 1) * (tt // 8), n8 - 1), 0)),
                  full((2, Z)), full((2, W)), full((2, LANE, W)), full((1, W)), full((LANE, W)), full((LANE, W)),
                  full((1, W)), full((1, W)), full((1, W)), full((W, W))],
        out_specs=[out_spec] * 9,
        out_shape=[jax.ShapeDtypeStruct((B, S, W), F32)] * 9,
        compiler_params=_cp("parallel", "parallel"),
        name="rwkv_features",
    )(z_rw, z_rw, z_rw, p["mu"], p["w0"], p["wup"], p["a0"], p["aup"], p["gup"], p["k_k"], p["k_a"], p["r_k"],
      p["ones64"])


def _rw_scan_kernel(r_f, lw_f, k_f, v_f, kk_f, a_f, r_b, lw_b, k_b, v_b, kk_b, a_b, of_ref, ob_ref, s_scr):
    @pl.when(pl.program_id(1) == 0)
    def _():
        s_scr[...] = jnp.zeros_like(s_scr)

    L = CHUNK
    blk = _block_ids(L)
    nb = r_f.shape[0]
    streams = [(bb, dr) for bb in range(nb) for dr in range(2)]
    incl, strict, at, rt, bh, kh, bl, kl, vb, g_tot = ([] for _ in range(10))
    for bb, dr in streams:
        r_ref, lw_ref, k_ref, v_ref, kk_ref, a_ref = ((r_f, lw_f, k_f, v_f, kk_f, a_f),
                                                      (r_b, lw_b, k_b, v_b, kk_b, a_b))[dr]
        m_incl, m_strict = _tri_masks(L, dr == 1)
        lw = lw_ref[bb]
        cs = _split_dot(m_incl.astype(BF16), lw, 3)
        tot = cs[0:1] if dr == 1 else cs[L - 1:L]
        kk, k = kk_ref[bb], k_ref[bb]
        bv = kk * a_ref[bb]
        g_inv = jnp.exp(-cs)
        g_rem = jnp.exp(tot - cs)
        incl.append(m_incl)
        strict.append(m_strict)
        g_tot.append(jnp.exp(tot))
        vb.append(v_ref[bb].astype(BF16))
        at.append((-(jnp.exp(cs - lw) * kk)).astype(BF16))
        rt.append((jnp.exp(cs) * r_ref[bb]).astype(BF16))
        bh.append((g_inv * bv).astype(BF16))
        kh.append((g_inv * k).astype(BF16))
        bl.append((g_rem * bv).astype(BF16))
        kl.append((g_rem * k).astype(BF16))
    es = [(d, h) for d in range(len(streams)) for h in range(RW_HEADS)]
    ids = range(len(es))
    sl = lambda h: slice(h * RW_DH, (h + 1) * RW_DH)
    s0 = [s_scr[d, h] for d, h in es]
    lhs = [jnp.concatenate([at[d][:, sl(h)], rt[d][:, sl(h)]], axis=0) for d, h in es]
    rhs = [jnp.concatenate([bh[d][:, sl(h)], kh[d][:, sl(h)]], axis=0) for d, h in es]
    vh = [vb[d][:, sl(h)] for d, h in es]
    sc = [_dot_nt(lhs[i], rhs[i]) for i in ids]
    zs = [_dot_nt(lhs[i], s0[i]) for i in ids]
    mab = [jnp.where(strict[d], sc[i][:L, :L], 0.0) for i, (d, h) in enumerate(es)]
    mak = [jnp.where(strict[d], sc[i][:L, L:], 0.0) for i, (d, h) in enumerate(es)]
    nrb = [jnp.where(incl[d], sc[i][L:, :L], 0.0) for i, (d, h) in enumerate(es)]
    nrk = [jnp.where(incl[d], sc[i][L:, L:], 0.0) for i, (d, h) in enumerate(es)]
    z0 = [zs[i][:L] + _dot(mak[i], vh[i]) for i in ids]
    tinv = _unit_tri_inverse(mab, blk)
    u = [_dot(tinv[i], z0[i]) for i in ids]
    o1 = [_dot(nrk[i], vh[i]) for i in ids]
    o2 = [_dot(nrb[i], u[i]) for i in ids]
    upd = [_dot_tn(jnp.concatenate([u[i].astype(BF16), vh[i]], axis=0),
                   jnp.concatenate([bl[d][:, sl(h)], kl[d][:, sl(h)]], axis=0)) for i, (d, h) in enumerate(es)]
    for i, (d, h) in enumerate(es):
        bb, dr = streams[d]
        (of_ref, ob_ref)[dr][bb, :, sl(h)] = zs[i][L:] + o1[i] + o2[i]
        s_scr[d, h] = s0[i] * g_tot[d][:, sl(h)] + upd[i]


def _rw_scan(r, lwf, lwb, k, v, kk, a, n_ctx):
    B, S, W = r.shape
    nc, fwd, bwd = _scan_index_maps(S, n_ctx)
    nb = _scan_batch(B, RW_SCAN_BATCH)
    spec = lambda cidx: pl.BlockSpec((nb, CHUNK, W), lambda b, s: (b, cidx(s), 0))
    return pl.pallas_call(
        _rw_scan_kernel,
        grid=(B // nb, nc),
        in_specs=[spec(fwd)] * 6 + [spec(bwd)] * 6,
        out_specs=[spec(fwd), spec(bwd)],
        out_shape=[jax.ShapeDtypeStruct((B, S, W), F32)] * 2,
        scratch_shapes=[pltpu.VMEM((2 * nb, RW_HEADS, RW_DH, RW_DH), F32)],
        compiler_params=_cp("parallel", "arbitrary"),
        name="rwkv_scan",
    )(r, lwf, k, v, kk, a, r, lwb, k, v, kk, a)


def _even_out_kernel(x_ref, ogf, ogb, gate_ref, orf, orb, bonus_ref, grw_ref, glag, lng, lnb, ones128, ones64,
                     w_ref, gl, gc, o_ref, *, tm, n_ctx):
    og = ogf[0] + ogb[0]
    ms = _split_dot_r(og * og, ones128[...], 2) * (1.0 / GLA_DV)
    gt = gate_ref[0]
    y_gla = og * lax.rsqrt(ms + 1e-6) * glag[...] * (gt * jax.nn.sigmoid(gt))
    of = orf[0] + orb[0]
    o64 = ones64[...]
    mean = _split_dot_r(of, o64, 3) * (1.0 / RW_DH)
    cen = of - mean
    var = _split_dot_r(cen * cen, o64, 2) * (1.0 / RW_DH)
    y_rw = (cen * lax.rsqrt(var + RW_GN_EPS) * lng[...] + lnb[...] + bonus_ref[0]) * grw_ref[0]
    y = jnp.concatenate([y_gla, y_rw], axis=1).astype(BF16)
    rows = pl.program_id(1) * tm + lax.broadcasted_iota(jnp.int32, (tm, 1), 0)
    gate = jnp.where(rows < n_ctx, gc[0], gl[0])
    o_ref[0] = x_ref[0] + gate * jnp.dot(y, w_ref[...], preferred_element_type=F32)


def _even_out(xs, og_f, og_b, z_gla, or_f, or_b, bonus, g_rw, p, mod, n_ctx, *, tm):
    B, S, D = xs.shape
    W = RW_W
    row = lambda width: pl.BlockSpec((1, tm, width), lambda b, j: (b, j, 0))
    full = lambda shape: pl.BlockSpec(shape, lambda b, j: (0,) * len(shape))
    return pl.pallas_call(
        functools.partial(_even_out_kernel, tm=tm, n_ctx=n_ctx),
        grid=(B, S // tm),
        in_specs=[row(D), row(GLA_V), row(GLA_V),
                  pl.BlockSpec((1, tm, GLA_V), lambda b, j: (b, j, (2 * GLA_QK + GLA_V) // GLA_V)),
                  row(W), row(W), row(W), row(W),
                  full((1, GLA_V)), full((1, W)), full((1, W)), full((GLA_V, GLA_V)), full((W, W)),
                  full((GLA_V + W, D)), *_mod_specs(B, D, 2, 2)],
        out_specs=row(D),
        out_shape=jax.ShapeDtypeStruct((B, S, D), F32),
        compiler_params=_cp("parallel", "parallel"),
        name="even_out",
    )(xs, og_f, og_b, z_gla, or_f, or_b, bonus, g_rw, p["gla_g"], p["ln_g"], p["ln_b"], p["ones128"], p["ones64"],
      p["w_out"], mod, mod)


def _odd_in_kernel(h_ref, w_ref, cos_ref, sin_ref, qg_ref, kg_ref, ones64, qc_o, kc_o, vc_o, qd_o, kd_o, vd_o):
    h = h_ref[0]
    cos, sin = cos_ref[...], sin_ref[...]
    o64 = ones64[...]
    nq, nk, nd = C_HEADS * HEAD_DIM, C_KV * HEAD_DIM, NA_HEADS * HEAD_DIM

    def piece(lo, width):
        return jnp.dot(h, w_ref[:, lo:lo + width], preferred_element_type=F32)

    def norm_rope(x, g, width):
        ms = _split_dot_r(x * x, o64[:width, :width], 2) * (1.0 / HEAD_DIM)
        y = x * lax.rsqrt(ms + 1e-6) * g
        lane = lax.broadcasted_iota(jnp.int32, (1, width), 1)
        first = (lane % (HEAD_DIM // 2)) < (HEAD_DIM // 4)
        swapped = jnp.where(first, pltpu.roll(y, width - HEAD_DIM // 4, 1), pltpu.roll(y, HEAD_DIM // 4, 1))
        return y * cos[:, :width] + swapped * sin[:, :width]

    qc_o[0] = (norm_rope(piece(0, nq), qg_ref[...], nq) * HEAD_DIM ** -0.5).astype(BF16)
    kc_o[0] = norm_rope(piece(nq, nk), kg_ref[:, :nk], nk).astype(BF16)
    vc_o[0] = piece(nq + nk, nk).astype(BF16)
    base = nq + 2 * nk
    qd_o[0] = (piece(base, nd) * HEAD_DIM ** -0.5).astype(BF16)
    kd_o[0] = piece(base + nd, nd).astype(BF16)
    vd_o[0] = piece(base + 2 * nd, nd).astype(BF16)


def _odd_in(h, w, cos_t, sin_t, qg, kg, ones64, *, tm):
    B, S, D = h.shape
    nq, nk, nd = C_HEADS * HEAD_DIM, C_KV * HEAD_DIM, NA_HEADS * HEAD_DIM
    full = lambda shape: pl.BlockSpec(shape, lambda b, j: (0,) * len(shape))
    row = lambda width: pl.BlockSpec((1, tm, width), lambda b, j: (b, j, 0))
    widths = (nq, nk, nk, nd, nd, nd)
    return pl.pallas_call(
        _odd_in_kernel,
        grid=(B, S // tm),
        in_specs=[row(D), _resident(w.shape),
                  pl.BlockSpec((tm, nq), lambda b, j: (j, 0)), pl.BlockSpec((tm, nq), lambda b, j: (j, 0)),
                  full((1, nq)), full((1, nq)), full((nq, nq))],
        out_specs=[row(wd) for wd in widths],
        out_shape=[jax.ShapeDtypeStruct((B, S, wd), BF16) for wd in widths],
        compiler_params=_cp("parallel", "parallel"),
        name="odd_in",
    )(h, w, cos_t, sin_t, qg, kg, ones64)


def _pair_attn_kernel(q_ref, k_ref, v_ref, o_ref, *, shared_kv):
    q = q_ref[0]
    tq = q.shape[0]
    ngroups = q.shape[1] // LANE
    lane = lax.broadcasted_iota(jnp.int32, (1, LANE), 1)
    masks = (lane < HEAD_DIM, lane >= HEAD_DIM)
    zero = jnp.zeros((), q.dtype)
    if shared_kv:
        kp, vp = k_ref[0], v_ref[0]
        lhs = [jnp.concatenate([jnp.where(masks[hm], q[:, g * LANE:(g + 1) * LANE], zero) for g in range(ngroups)],
                               axis=0) for hm in range(2)]
        s = [_dot_nt(lhs[hm], kp) for hm in range(2)]
        p = [jnp.exp(s[hm] - jnp.max(s[hm], axis=-1, keepdims=True)) for hm in range(2)]
        den = [jnp.sum(p[hm], axis=-1, keepdims=True) for hm in range(2)]
        o = [_dot(p[hm], vp) / den[hm] for hm in range(2)]
        for g in range(ngroups):
            rows = slice(g * tq, (g + 1) * tq)
            o_ref[0, :, g * LANE:(g + 1) * LANE] = jnp.where(masks[0], o[0][rows], o[1][rows]).astype(o_ref.dtype)
    else:
        for g in range(ngroups):
            qp = q[:, g * LANE:(g + 1) * LANE]
            kp = k_ref[0, :, g * LANE:(g + 1) * LANE]
            vp = v_ref[0, :, g * LANE:(g + 1) * LANE]
            lhs = jnp.concatenate([jnp.where(masks[0], qp, zero), jnp.where(masks[1], qp, zero)], axis=0)
            s = _dot_nt(lhs, kp)
            p = jnp.exp(s - jnp.max(s, axis=-1, keepdims=True))
            o = _dot(p, vp) / jnp.sum(p, axis=-1, keepdims=True)
            o_ref[0, :, g * LANE:(g + 1) * LANE] = jnp.where(masks[0], o[:tq], o[tq:]).astype(o_ref.dtype)


def _pair_attention(q, k, v, *, q_row0, n_q, n_k, tq, shared_kv, name):
    B, S, QW = q.shape
    KW = k.shape[2]
    return pl.pallas_call(
        functools.partial(_pair_attn_kernel, shared_kv=shared_kv),
        grid=(B, n_q // tq),
        in_specs=[pl.BlockSpec((1, tq, QW), lambda b, i: (b, q_row0 // tq + i, 0)),
                  pl.BlockSpec((1, n_k, KW), lambda b, i: (b, 0, 0)),
                  pl.BlockSpec((1, n_k, KW), lambda b, i: (b, 0, 0))],
        out_specs=pl.BlockSpec((1, tq, QW), lambda b, i: (b, i, 0)),
        out_shape=jax.ShapeDtypeStruct((B, n_q, QW), BF16),
        compiler_params=_cp("parallel", "parallel"),
        name=name,
    )(q, k, v)


NA_ROWS_PER_STEP = 2


def _na_kernel(q_ref, k_ref, v_ref, *rest, n_ctx, n_rows, kh, rps):
    bias_refs, o_ref = rest[:rps], rest[rps]
    nkeys = kh * GRID_W
    lane = lax.broadcasted_iota(jnp.int32, (1, LANE), 1)
    m0 = lane < HEAD_DIM
    zero = jnp.zeros((), q_ref.dtype)
    starts = []
    for j in range(rps):
        r = pl.program_id(1) * rps + j
        rs = jnp.clip(r - kh // 2, 0, n_rows - kh)
        starts.append(pl.multiple_of(n_ctx + rs * GRID_W, GRID_W))
    es = [(j, g) for j in range(rps) for g in range(NA_HEADS // 2)]
    ids = range(len(es))
    gs = lambda g: slice(g * LANE, (g + 1) * LANE)
    qrow = lambda j, g: q_ref[0, j * GRID_W:(j + 1) * GRID_W, gs(g)]
    lhs = [jnp.concatenate([jnp.where(m0, qrow(j, g), zero), jnp.where(m0, zero, qrow(j, g))], axis=0)
           for j, g in es]
    s_nb = [_dot_nt(lhs[i], k_ref[0, pl.ds(starts[j], nkeys), gs(g)]) for i, (j, g) in enumerate(es)]
    s_cx = [_dot_nt(lhs[i], k_ref[0, 0:n_ctx, gs(g)]) for i, (j, g) in enumerate(es)]
    s_nb = [s_nb[i] + jnp.concatenate([bias_refs[j][0, 2 * g], bias_refs[j][0, 2 * g + 1]], axis=0)
            for i, (j, g) in enumerate(es)]
    m = [jnp.maximum(jnp.max(s_nb[i], axis=-1, keepdims=True), jnp.max(s_cx[i], axis=-1, keepdims=True))
         for i in ids]
    p_nb = [jnp.exp(s_nb[i] - m[i]) for i in ids]
    p_cx = [jnp.exp(s_cx[i] - m[i]) for i in ids]
    den = [jnp.sum(p_nb[i], axis=-1, keepdims=True) + jnp.sum(p_cx[i], axis=-1, keepdims=True) for i in ids]
    o_nb = [_dot(p_nb[i], v_ref[0, pl.ds(starts[j], nkeys), gs(g)]) for i, (j, g) in enumerate(es)]
    o_cx = [_dot(p_cx[i], v_ref[0, 0:n_ctx, gs(g)]) for i, (j, g) in enumerate(es)]
    for i, (j, g) in enumerate(es):
        o = (o_nb[i] + o_cx[i]) / den[i]
        o_ref[0, j * GRID_W:(j + 1) * GRID_W, gs(g)] = jnp.where(m0, o[:GRID_W], o[GRID_W:]).astype(o_ref.dtype)


def _na_attention(qd, kd, vd, bias_tab, n_ctx):
    B, S, W = qd.shape
    T = S - n_ctx
    n_rows = T // GRID_W
    kh = min(NA_KH, n_rows)
    nkeys = kh * GRID_W
    rps = NA_ROWS_PER_STEP if n_rows % NA_ROWS_PER_STEP == 0 and n_ctx % (NA_ROWS_PER_STEP * GRID_W) == 0 else 1
    tq = rps * GRID_W

    def bias_spec(j):
        def idx(b, i):
            r = i * rps + j
            return (jnp.clip(r - kh // 2, 0, n_rows - kh) - r + kh - 1, 0, 0, 0)
        return pl.BlockSpec((1, NA_HEADS, GRID_W, nkeys), idx)

    return pl.pallas_call(
        functools.partial(_na_kernel, n_ctx=n_ctx, n_rows=n_rows, kh=kh, rps=rps),
        grid=(B, n_rows // rps),
        in_specs=[pl.BlockSpec((1, tq, W), lambda b, i: (b, n_ctx // tq + i, 0)),
                  pl.BlockSpec((1, S, W), lambda b, i: (b, 0, 0)),
                  pl.BlockSpec((1, S, W), lambda b, i: (b, 0, 0)),
                  *[bias_spec(j) for j in range(rps)]],
        out_specs=pl.BlockSpec((1, tq, W), lambda b, i: (b, i, 0)),
        out_shape=jax.ShapeDtypeStruct((B, T, W), BF16),
        compiler_params=_cp("parallel", "arbitrary"),
        name="na_attention",
    )(qd, kd, vd, *([bias_tab] * rps))


def _na_bias_table(rpb, n_rows):
    kh = min(NA_KH, n_rows)
    kw = min(NA_KW, GRID_W)
    col = np.arange(GRID_W)
    start = np.clip(col - kw // 2, 0, GRID_W - kw)
    in_win = (col[None, :] >= start[:, None]) & (col[None, :] < start[:, None] + kw)
    dc = np.clip(col[None, :] - col[:, None], -(NA_KW - 1), NA_KW - 1) + NA_KW - 1
    bias_cols = rpb[:, :, dc]
    slabs = []
    for st in range(kh):
        sl = bias_cols[:, st + NA_KH - kh:st + NA_KH - kh + kh]
        sl = jnp.where(in_win[None, None], sl, NEG).transpose(0, 2, 1, 3)
        slabs.append(sl.reshape(NA_HEADS, GRID_W, kh * GRID_W))
    return jnp.stack(slabs).astype(F32)


def _proj_res_kernel(x_ref, y_ref, w_ref, gl, gc, o_ref, *, tm, n_ctx):
    rows = pl.program_id(1) * tm + lax.broadcasted_iota(jnp.int32, (tm, 1), 0)
    gate = jnp.where(rows < n_ctx, gc[0], gl[0])
    o_ref[0] = x_ref[0] + gate * jnp.dot(y_ref[0], w_ref[...], preferred_element_type=F32)


def _proj_res(xs, y, w, mod, n_ctx, *, tm):
    B, S, D = xs.shape
    K = y.shape[2]
    row = lambda width: pl.BlockSpec((1, tm, width), lambda b, j: (b, j, 0))
    return pl.pallas_call(
        functools.partial(_proj_res_kernel, tm=tm, n_ctx=n_ctx),
        grid=(B, S // tm),
        in_specs=[row(D), row(K), pl.BlockSpec((K, D), lambda b, j: (0, 0)), *_mod_specs(B, D, 2, 2)],
        out_specs=row(D),
        out_shape=jax.ShapeDtypeStruct((B, S, D), F32),
        compiler_params=_cp("parallel", "parallel"),
        name="proj_res",
    )(xs, y, w, mod, mod)


def _final_norm_kernel(x_ref, g_ref, o_ref):
    x = x_ref[0]
    o_ref[0] = x * lax.rsqrt(jnp.mean(x * x, axis=-1, keepdims=True) + 1e-6) * g_ref[...]


def _final_norm(xs, g, n_ctx, *, tm):
    B, S, D = xs.shape
    T = S - n_ctx
    return pl.pallas_call(
        _final_norm_kernel,
        grid=(B, T // tm),
        in_specs=[pl.BlockSpec((1, tm, D), lambda b, j: (b, n_ctx // tm + j, 0)),
                  pl.BlockSpec((1, D), lambda b, j: (0, 0))],
        out_specs=pl.BlockSpec((1, tm, D), lambda b, j: (b, j, 0)),
        out_shape=jax.ShapeDtypeStruct((B, T, D), F32),
        compiler_params=_cp("parallel", "parallel"),
        name="final_norm",
    )(xs, g.reshape(1, D))


def _block_ones(width, block):
    idx = np.arange(width) // block
    return jnp.asarray(idx[:, None] == idx[None, :], BF16)


def _pad_cols(w, width):
    return jnp.pad(w, ((0, 0), (0, width - w.shape[1])))


def _pad_rows(w, height):
    return jnp.pad(w, ((0, height - w.shape[0]), (0, 0)))


def _even_params(j, ev_w_in, ev_w_out, gla_a_up, gla_a_bias, gla_norm_g, rw_mu, rw_w0, rw_w_up, rw_a0, rw_a_up,
                 rw_g_up, rw_k_k, rw_k_a, rw_r_k, rw_ln_g, rw_ln_b):
    w_in = ev_w_in[j]
    D = w_in.shape[0]
    gq = 2 * GLA_QK + 2 * GLA_V
    w_gla = jnp.concatenate([w_in[:, :gq], _pad_cols(w_in[:, gq:gq + GLA_RANK], LANE),
                             _pad_cols(w_in[:, gq + GLA_RANK:gq + 2 * GLA_RANK], LANE)], axis=1)
    o = gq + 2 * GLA_RANK
    sizes = (3 * RW_W, RW_DECAY_RANK, RW_DECAY_RANK, RW_A_RANK, RW_G_RANK)
    offs = np.cumsum((0,) + sizes)
    pieces = [w_in[:, o:o + 3 * RW_W]] + [_pad_cols(w_in[:, o + offs[i]:o + offs[i + 1]], LANE) for i in range(1, 5)]
    w_rw = jnp.concatenate(pieces, axis=1)
    mu = rw_mu[j]
    mu_pieces = [mu[:, :3 * RW_W]] + [_pad_cols(mu[:, offs[i]:offs[i + 1]], LANE) for i in range(1, 5)]
    return {
        "w_gla": w_gla.astype(BF16), "w_rw": w_rw.astype(BF16), "w_out": ev_w_out[j].astype(BF16),
        "gla_aup": jnp.stack([_pad_rows(gla_a_up[j, d], LANE) for d in range(2)]).astype(BF16),
        "gla_bias": gla_a_bias[j], "gla_g": jnp.tile(gla_norm_g[j], GLA_HEADS).reshape(1, GLA_V),
        "mu": jnp.concatenate(mu_pieces, axis=1), "w0": rw_w0[j],
        "wup": jnp.stack([_pad_rows(rw_w_up[j, d], LANE) for d in range(2)]).astype(BF16),
        "a0": rw_a0[j].reshape(1, RW_W), "aup": _pad_rows(rw_a_up[j], LANE).astype(BF16),
        "gup": _pad_rows(rw_g_up[j], LANE).astype(BF16),
        "k_k": rw_k_k[j].reshape(1, RW_W), "k_a": rw_k_a[j].reshape(1, RW_W), "r_k": rw_r_k[j].reshape(1, RW_W),
        "ln_g": rw_ln_g[j].reshape(1, RW_W), "ln_b": rw_ln_b[j].reshape(1, RW_W),
        "ones64": _block_ones(RW_W, RW_DH), "ones128": _block_ones(GLA_V, GLA_DV),
    }


def _gqa_head_order():
    per_kv = C_HEADS // C_KV
    order = []
    for jj in range(per_kv):
        for kv in range(C_KV):
            h = kv * per_kv + jj
            order.extend(range(h * HEAD_DIM, (h + 1) * HEAD_DIM))
    return np.asarray(order)


def _rope_tables(n_ctx, T, width):
    t = jnp.arange(T)
    pos = jnp.stack([t // GRID_W, t % GRID_W], axis=-1).astype(F32)
    half = HEAD_DIM // 2
    inv = ROPE_THETA ** (-jnp.arange(0, half, 2, dtype=F32) / half)
    ang = pos[:, :, None] * inv
    cos, sin = jnp.cos(ang), jnp.sin(ang)
    cos_h = jnp.concatenate([cos, cos], axis=-1).reshape(T, HEAD_DIM)
    sin_h = jnp.concatenate([-sin, sin], axis=-1).reshape(T, HEAD_DIM)
    cos_h = jnp.concatenate([jnp.ones((n_ctx, HEAD_DIM), F32), cos_h], axis=0)
    sin_h = jnp.concatenate([jnp.zeros((n_ctx, HEAD_DIM), F32), sin_h], axis=0)
    reps = width // HEAD_DIM
    return jnp.tile(cos_h, (1, reps)), jnp.tile(sin_h, (1, reps))


def _even_layer(xs, h, mod, n_ctx, p):
    z_gla, z_rw = _project(h, [p["w_gla"], p["w_rw"]], tm=_row_tile(xs.shape[1], ROW_CAP_FFN), name="even_in")
    og_f, og_b = _gla_scan(z_gla, p["gla_aup"], p["gla_bias"], n_ctx)
    r, lwf, lwb, k, v, kk, a, g_rw, bonus = _rw_features(z_rw, p, n_ctx)
    or_f, or_b = _rw_scan(r, lwf, lwb, k, v, kk, a, n_ctx)
    return _even_out(xs, og_f, og_b, z_gla, or_f, or_b, bonus, g_rw, p, mod, n_ctx,
                     tm=_row_tile(xs.shape[1], ROW_CAP_ELEMENTWISE))


def _odd_layer(xs, h, mod, n_ctx, p, need_ctx):
    B, S, D = xs.shape
    T = S - n_ctx
    qc, kc, vc, qd, kd, vd = _odd_in(h, p["w_in"], p["cos"], p["sin"], p["qg"], p["kg"], p["ones64"],
                                     tm=_row_tile(S, ROW_CAP_FFN))
    y_gqa = _pair_attention(qc, kc, vc, q_row0=n_ctx, n_q=T, n_k=S, tq=128, shared_kv=True, name="gqa")
    y_na = _na_attention(qd, kd, vd, p["bias_tab"], n_ctx)
    parts_g, parts_n = [y_gqa], [y_na]
    if need_ctx:
        parts_g.insert(0, _pair_attention(qc, kc, vc, q_row0=0, n_q=n_ctx, n_k=n_ctx, tq=n_ctx, shared_kv=True,
                                          name="gqa_ctx"))
        parts_n.insert(0, _pair_attention(qd, kd, vd, q_row0=0, n_q=n_ctx, n_k=n_ctx, tq=n_ctx, shared_kv=False,
                                          name="na_ctx"))
    else:
        zeros = jnp.zeros((B, n_ctx, y_gqa.shape[2]), BF16)
        parts_g.insert(0, zeros)
        parts_n.insert(0, zeros)
    y = jnp.concatenate([jnp.concatenate(parts_g, axis=1), jnp.concatenate(parts_n, axis=1)], axis=2)
    return _proj_res(xs, y, p["w_out"], mod, n_ctx, tm=_row_tile(S, ROW_CAP_MATMUL))


def kernel(x, c, ctx, c_ctx, w_mod, b_mod, norm1_g, norm2_g, ffn_w13, ffn_w2, ev_w_in, ev_w_out, gla_a_up,
           gla_a_bias, gla_norm_g, rw_mu, rw_w0, rw_w_up, rw_a0, rw_a_up, rw_g_up, rw_k_k, rw_k_a, rw_r_k, rw_ln_g,
           rw_ln_b, od_w_in, od_w_out, cq_norm_g, ck_norm_g, na_rpb, final_g):
    B, T, D = x.shape
    n_ctx = ctx.shape[1]
    S = n_ctx + T
    xs = jnp.concatenate([ctx, x], axis=1)
    mods = _mod_vectors(c, c_ctx, w_mod, b_mod)
    nq, nk = C_HEADS * HEAD_DIM, C_KV * HEAD_DIM
    order = _gqa_head_order()
    cos_t, sin_t = _rope_tables(n_ctx, T, nq)
    ones64 = _block_ones(nq, HEAD_DIM)
    h = _norm_mod(xs, norm1_g[0], mods[0], n_ctx, tm=_row_tile(S, ROW_CAP_MATMUL))
    for i in range(DEPTH):
        j = i // 2
        mod = mods[i]
        if i % 2 == 0:
            p = _even_params(j, ev_w_in, ev_w_out, gla_a_up, gla_a_bias, gla_norm_g, rw_mu, rw_w0, rw_w_up, rw_a0,
                             rw_a_up, rw_g_up, rw_k_k, rw_k_a, rw_r_k, rw_ln_g, rw_ln_b)
            xs = _even_layer(xs, h, mod, n_ctx, p)
        else:
            w_in = od_w_in[j]
            w_out = od_w_out[j]
            p = {
                "w_in": jnp.concatenate([w_in[:, :nq][:, order], w_in[:, nq:]], axis=1).astype(BF16),
                "w_out": jnp.concatenate([w_out[:nq][order], w_out[nq:]], axis=0).astype(BF16),
                "cos": cos_t, "sin": sin_t, "ones64": ones64,
                "qg": jnp.tile(cq_norm_g[j], C_HEADS).reshape(1, nq),
                "kg": jnp.tile(ck_norm_g[j], C_HEADS).reshape(1, nq),
                "bias_tab": _na_bias_table(na_rpb[j], T // GRID_W),
            }
            xs = _odd_layer(xs, h, mod, n_ctx, p, need_ctx=i < DEPTH - 1)
        next_norm = (norm1_g[i + 1], mods[i + 1]) if i + 1 < DEPTH else None
        xs, h = _ffn(xs, norm2_g[i], mod, ffn_w13[i].astype(BF16), ffn_w2[i].astype(BF16), n_ctx,
                     tm=_row_tile(S, ROW_CAP_FFN), tf=FFN_COL_TILE, next_norm=next_norm)
    return _final_norm(xs, final_g, n_ctx, tm=_row_tile(n_ctx, ROW_CAP_ELEMENTWISE))
```

```python
import functools
import math

import jax
import jax.numpy as jnp
import numpy as np
from jax import lax
from jax.experimental import pallas as pl
from jax.experimental.pallas import tpu as pltpu

F32 = jnp.float32
BF16 = jnp.bfloat16

DEPTH = 4
GRID_W = 64
CHUNK = 64
LANE = 128

GLA_HEADS, GLA_DK, GLA_DV, GLA_RANK, GLA_TAU = 4, 64, 128, 16, 16.0
GLA_QK, GLA_V = GLA_HEADS * GLA_DK, GLA_HEADS * GLA_DV
GLA_Z = 2 * GLA_QK + 2 * GLA_V + 2 * LANE

RW_HEADS, RW_DH, RW_W = 8, 64, 512
RW_DECAY_RANK, RW_A_RANK, RW_G_RANK = 32, 32, 96
RW_GN_EPS = 64e-5
RW_Z = 3 * RW_W + 4 * LANE

HEAD_DIM, C_HEADS, C_KV, NA_HEADS, NA_KH, NA_KW = 64, 8, 2, 8, 8, 16
ROPE_THETA = 10000.0
NEG = -1e30

VMEM_LIMIT = 56 * 1024 * 1024
ROW_CAP_MATMUL = 1152
ROW_CAP_FFN = 576
ROW_CAP_ELEMENTWISE = 256
FFN_COL_TILE = 256


def _row_tile(n_rows, cap):
    return max(d for d in range(8, cap + 1, 8) if n_rows % d == 0)


def _cp(*sem):
    return pltpu.CompilerParams(dimension_semantics=sem, vmem_limit_bytes=VMEM_LIMIT)


def _dot(a, b):
    return jnp.dot(a.astype(BF16), b.astype(BF16), preferred_element_type=F32)


def _dot_nt(a, b):
    return lax.dot_general(a.astype(BF16), b.astype(BF16), (((1,), (1,)), ((), ())), preferred_element_type=F32)


def _dot_tn(a, b):
    return lax.dot_general(a.astype(BF16), b.astype(BF16), (((0,), (0,)), ((), ())), preferred_element_type=F32)


def _split_dot(m, x, parts):
    acc = None
    rem = x
    for _ in range(parts):
        piece = rem.astype(BF16)
        rem = rem - piece.astype(F32)
        t = jnp.dot(m, piece, preferred_element_type=F32)
        acc = t if acc is None else acc + t
    return acc


def _split_dot_r(x, m, parts):
    acc = None
    rem = x
    for _ in range(parts):
        piece = rem.astype(BF16)
        rem = rem - piece.astype(F32)
        t = jnp.dot(piece, m, preferred_element_type=F32)
        acc = t if acc is None else acc + t
    return acc


def _rms_mod(x, g, sc_l, sh_l, sc_c, sh_c, row0, n_ctx):
    y = x * lax.rsqrt(jnp.mean(x * x, axis=-1, keepdims=True) + 1e-6) * g
    rows = row0 + lax.broadcasted_iota(jnp.int32, (x.shape[0], 1), 0)
    is_ctx = rows < n_ctx
    sc = jnp.where(is_ctx, sc_c, sc_l)
    sh = jnp.where(is_ctx, sh_c, sh_l)
    return y * (1.0 + sc) + sh


def _tri_masks(n, reverse):
    ri = lax.broadcasted_iota(jnp.int32, (n, n), 0)
    ci = lax.broadcasted_iota(jnp.int32, (n, n), 1)
    if reverse:
        return ci >= ri, ci > ri
    return ci <= ri, ci < ri


INV_BASE = 16


def _pair_ids(n):
    ti = lax.broadcasted_iota(jnp.int32, (n, 2 * n), 0)
    ji = lax.broadcasted_iota(jnp.int32, (n, 2 * n), 1) % n
    return ti, ji


def _pair_blockdiag(x, lane_is_head0):
    zero = jnp.zeros((), x.dtype)
    return jnp.concatenate([jnp.where(lane_is_head0, x, zero), jnp.where(lane_is_head0, zero, x)], axis=0)


def _unit_tri_inverse(a, ids_tj, lane_is_head0):
    ti, ji = ids_tj
    n = a[0].shape[0]
    ids = range(len(a))
    bd = lambda x: _pair_blockdiag(x.astype(BF16), lane_is_head0)
    same = lambda bs: (ti // bs) == (ji // bs)
    base = same(INV_BASE)
    eye = (ti == ji).astype(F32)
    p = [jnp.where(base, x, 0.0) for x in a]
    t = [eye + x for x in p]
    m = 1
    while 2 * m < INV_BASE:
        p = [_dot(x, bd(x)) for x in p]
        tp = [_dot(t[i], bd(p[i])) for i in ids]
        t = [t[i] + tp[i] for i in ids]
        m *= 2
    bs = INV_BASE
    while bs < n:
        mask = same(2 * bs) & jnp.logical_not(same(bs))
        ta = [_dot(t[i], bd(jnp.where(mask, a[i], 0.0))) for i in ids]
        tat = [_dot(ta[i], bd(t[i])) for i in ids]
        t = [t[i] + tat[i] for i in ids]
        bs *= 2
    return t


def _mod_kernel(s_ref, w_ref, b_ref, o_ref):
    s = s_ref[...]
    s = s * jax.nn.sigmoid(s)
    o_ref[0] = _dot(s, w_ref[0]) + b_ref[0]


def _mod_vectors(c, c_ctx, w_mod, b_mod):
    B, D = c.shape
    bp = -(-(B + 1) // 8) * 8
    s_in = jnp.zeros((bp, D), F32).at[:B].set(c).at[B].set(c_ctx)
    n6 = w_mod.shape[-1]
    tn = 1536
    out = pl.pallas_call(
        _mod_kernel,
        grid=(DEPTH, n6 // tn),
        in_specs=[pl.BlockSpec((bp, D), lambda l, n: (0, 0)),
                  pl.BlockSpec((1, D, tn), lambda l, n: (l, 0, n)),
                  pl.BlockSpec((1, 1, tn), lambda l, n: (l, 0, n))],
        out_specs=pl.BlockSpec((1, bp, tn), lambda l, n: (l, 0, n)),
        out_shape=jax.ShapeDtypeStruct((DEPTH, bp, n6), F32),
        compiler_params=_cp("parallel", "parallel"),
        name="mod_vectors",
    )(s_in, w_mod, b_mod.reshape(DEPTH, 1, n6))
    return out.reshape(DEPTH, bp, 1, n6)


def _mod_specs(B, D, idx, grid_rank):
    if grid_rank == 2:
        return [pl.BlockSpec((1, 1, D), lambda b, j: (b, 0, idx)),
                pl.BlockSpec((1, 1, D), lambda b, j: (B, 0, idx))]
    return [pl.BlockSpec((1, 1, D), lambda b, j, n: (b, 0, idx)),
            pl.BlockSpec((1, 1, D), lambda b, j, n: (B, 0, idx))]


def _norm_mod_kernel(x_ref, g_ref, shl, shc, scl, scc, o_ref, *, tm, n_ctx):
    h = _rms_mod(x_ref[0], g_ref[...], scl[0], shl[0], scc[0], shc[0], pl.program_id(1) * tm, n_ctx)
    o_ref[0] = h.astype(BF16)


def _norm_mod(xs, g, mod, n_ctx, *, tm):
    B, S, D = xs.shape
    row = pl.BlockSpec((1, tm, D), lambda b, j: (b, j, 0))
    return pl.pallas_call(
        functools.partial(_norm_mod_kernel, tm=tm, n_ctx=n_ctx),
        grid=(B, S // tm),
        in_specs=[row, pl.BlockSpec((1, D), lambda b, j: (0, 0)), *_mod_specs(B, D, 0, 2), *_mod_specs(B, D, 1, 2)],
        out_specs=row,
        out_shape=jax.ShapeDtypeStruct((B, S, D), BF16),
        compiler_params=_cp("parallel", "parallel"),
        name="norm_mod",
    )(xs, g.reshape(1, D), mod, mod, mod, mod)


def _proj_kernel(h_ref, *refs):
    n = len(refs) // 2
    h = h_ref[0]
    for w_ref, o_ref in zip(refs[:n], refs[n:]):
        o_ref[0] = jnp.dot(h, w_ref[...], preferred_element_type=F32)


def _project(h, ws, *, tm, name):
    B, S, D = h.shape
    row = lambda width: pl.BlockSpec((1, tm, width), lambda b, j: (b, j, 0))
    return pl.pallas_call(
        _proj_kernel,
        grid=(B, S // tm),
        in_specs=[row(D)] + [_resident(w.shape) for w in ws],
        out_specs=[row(w.shape[1]) for w in ws],
        out_shape=[jax.ShapeDtypeStruct((B, S, w.shape[1]), F32) for w in ws],
        compiler_params=_cp("parallel", "parallel"),
        name=name,
    )(h, *ws)


def _ffn_kernel(x_ref, g_ref, shl, shc, scl, scc, gl, gc, w13_ref, w2_ref, *rest, tm, tf, n_ctx, emit_next):
    if emit_next:
        gn_ref, nshl, nshc, nscl, nscc, o_ref, hn_ref = rest
    else:
        (o_ref,) = rest
    x = x_ref[0]
    row0 = pl.program_id(1) * tm
    h = _rms_mod(x, g_ref[...], scl[0], shl[0], scc[0], shc[0], row0, n_ctx).astype(BF16)
    n_hidden = w2_ref.shape[0]
    acc = None
    for f in range(n_hidden // tf):
        a = jnp.dot(h, w13_ref[:, f * tf:(f + 1) * tf], preferred_element_type=F32)
        b = jnp.dot(h, w13_ref[:, n_hidden + f * tf:n_hidden + (f + 1) * tf], preferred_element_type=F32)
        u = (a * jax.nn.sigmoid(a) * b).astype(BF16)
        t = jnp.dot(u, w2_ref[f * tf:(f + 1) * tf, :], preferred_element_type=F32)
        acc = t if acc is None else acc + t
    rows = row0 + lax.broadcasted_iota(jnp.int32, (tm, 1), 0)
    gate = jnp.where(rows < n_ctx, gc[0], gl[0])
    y = x + gate * acc
    o_ref[0] = y
    if emit_next:
        hn_ref[0] = _rms_mod(y, gn_ref[...], nscl[0], nshl[0], nscc[0], nshc[0], row0, n_ctx).astype(BF16)


def _resident(shape):
    return pl.BlockSpec(shape, lambda b, j: (0,) * len(shape), pipeline_mode=pl.Buffered(1))


def _ffn(xs, g, mod, w13, w2, n_ctx, *, tm, tf, next_norm=None):
    B, S, D = xs.shape
    row = pl.BlockSpec((1, tm, D), lambda b, j: (b, j, 0))
    vec = pl.BlockSpec((1, D), lambda b, j: (0, 0))
    in_specs = [row, vec, *_mod_specs(B, D, 3, 2), *_mod_specs(B, D, 4, 2), *_mod_specs(B, D, 5, 2),
                _resident(w13.shape), _resident(w2.shape)]
    args = [xs, g.reshape(1, D), mod, mod, mod, mod, mod, mod, w13, w2]
    out_specs, out_shape = [row], [jax.ShapeDtypeStruct((B, S, D), F32)]
    if next_norm is not None:
        g_next, mod_next = next_norm
        in_specs += [vec, *_mod_specs(B, D, 0, 2), *_mod_specs(B, D, 1, 2)]
        args += [g_next.reshape(1, D), mod_next, mod_next, mod_next, mod_next]
        out_specs.append(row)
        out_shape.append(jax.ShapeDtypeStruct((B, S, D), BF16))
    out = pl.pallas_call(
        functools.partial(_ffn_kernel, tm=tm, tf=tf, n_ctx=n_ctx, emit_next=next_norm is not None),
        grid=(B, S // tm),
        in_specs=in_specs,
        out_specs=out_specs,
        out_shape=out_shape,
        compiler_params=_cp("parallel", "parallel"),
        name="ffn",
    )(*args)
    return (out[0], out[1]) if next_norm is not None else (out[0], None)


def _chunk_index(s, nc_ctx, nc, reverse):
    if not reverse:
        return s
    return jnp.where(s < nc_ctx, nc_ctx - 1 - s, nc + nc_ctx - 1 - s)


def _gla_kernel(qkv_f, ad_f, qkv_b, ad_b, aup_ref, bias_ref, of_ref, ob_ref, st_scr):
    @pl.when(pl.program_id(1) == 0)
    def _():
        st_scr[...] = jnp.zeros_like(st_scr)

    L = CHUNK
    nb = qkv_f.shape[0]
    streams = [(bb, d) for bb in range(nb) for d in range(2)]
    qb, kb, kt, gt, v, incl = [], [], [], [], [], []
    for bb, d in streams:
        qkv_ref, ad_ref = ((qkv_f, ad_f), (qkv_b, ad_b))[d]
        m, _ = _tri_masks(L, d == 1)
        ad = ad_ref[bb][:, d * LANE:(d + 1) * LANE]
        y = _dot(ad, aup_ref[d]) + bias_ref[d:d + 1, :]
        la = (jnp.minimum(y, 0.0) - jnp.log1p(jnp.exp(-jnp.abs(y)))) * (1.0 / GLA_TAU)
        cs = _split_dot(m.astype(BF16), la, 3)
        tot = cs[0:1] if d == 1 else cs[L - 1:L]
        qkv = qkv_ref[bb]
        q = qkv[:, 0:GLA_QK] * GLA_DK ** -0.5
        k = qkv[:, GLA_QK:2 * GLA_QK]
        incl.append(m)
        v.append(qkv[:, 2 * GLA_QK:2 * GLA_QK + GLA_V].astype(BF16))
        qb.append((q * jnp.exp(cs)).astype(BF16))
        kb.append((k * jnp.exp(-cs)).astype(BF16))
        kt.append((k * jnp.exp(tot - cs)).astype(BF16))
        gt.append(jnp.exp(tot))
    es = [(c, h) for c in range(len(streams)) for h in range(GLA_HEADS)]
    ks = lambda h: slice(h * GLA_DK, (h + 1) * GLA_DK)
    vs = lambda h: slice(h * GLA_DV, (h + 1) * GLA_DV)
    st = [st_scr[c, h] for c, h in es]
    att = [jnp.where(incl[c], _dot_nt(qb[c][:, ks(h)], kb[c][:, ks(h)]), 0.0) for c, h in es]
    o_state = [_dot_nt(qb[c][:, ks(h)], st[i]) for i, (c, h) in enumerate(es)]
    o_att = [_dot(att[i], v[c][:, vs(h)]) for i, (c, h) in enumerate(es)]
    upd = [_dot_tn(v[c][:, vs(h)], kt[c][:, ks(h)]) for c, h in es]
    for i, (c, h) in enumerate(es):
        bb, d = streams[c]
        (of_ref, ob_ref)[d][bb, :, vs(h)] = o_att[i] + o_state[i]
        st_scr[c, h] = st[i] * gt[c][:, ks(h)] + upd[i]


GLA_SCAN_BATCH = 4
RW_SCAN_BATCH = 4


def _scan_batch(B, want):
    return want if B % want == 0 else 1


def _scan_index_maps(S, n_ctx):
    nc, nc_ctx = S // CHUNK, n_ctx // CHUNK
    fwd = functools.partial(_chunk_index, nc_ctx=nc_ctx, nc=nc, reverse=False)
    bwd = functools.partial(_chunk_index, nc_ctx=nc_ctx, nc=nc, reverse=True)
    return nc, fwd, bwd


def _gla_scan(z_gla, aup_pad, bias, n_ctx):
    B, S, _ = z_gla.shape
    nc, fwd, bwd = _scan_index_maps(S, n_ctx)
    ad_blk = (2 * GLA_QK + 2 * GLA_V) // (2 * LANE)
    nb = _scan_batch(B, GLA_SCAN_BATCH)
    qkv_spec =lambda cidx: pl.BlockSpec((nb, CHUNK, 2 * GLA_QK + GLA_V), lambda b, s: (b, cidx(s), 0))
    ad_spec = lambda cidx: pl.BlockSpec((nb, CHUNK, 2 * LANE), lambda b, s: (b, cidx(s), ad_blk))
    out_spec = lambda cidx: pl.BlockSpec((nb, CHUNK, GLA_V), lambda b, s: (b, cidx(s), 0))
    return pl.pallas_call(
        _gla_kernel,
        grid=(B // nb, nc),
        in_specs=[qkv_spec(fwd), ad_spec(fwd), qkv_spec(bwd), ad_spec(bwd),
                  pl.BlockSpec((2, LANE, GLA_QK), lambda b, s: (0, 0, 0)),
                  pl.BlockSpec((2, GLA_QK), lambda b, s: (0, 0))],
        out_specs=[out_spec(fwd), out_spec(bwd)],
        out_shape=[jax.ShapeDtypeStruct((B, S, GLA_V), F32)] * 2,
        scratch_shapes=[pltpu.VMEM((2 * nb, GLA_HEADS, GLA_DV, GLA_DK), F32)],
        compiler_params=_cp("parallel", "arbitrary"),
        name="gla_scan",
    )(z_gla, z_gla, z_gla, z_gla, aup_pad, bias)


def _rw_feat_kernel(z_ref, zp_ref, zn_ref, mu_ref, w0_ref, wup_ref, a0_ref, aup_ref, gup_ref, kk_ref, ka_ref,
                    rk_ref, ones_ref, r_o, lwf_o, lwb_o, k_o, v_o, kk_o, a_o, g_o, bonus_o, *, tt):
    j = pl.program_id(1)
    last = pl.num_programs(1) - 1
    z = z_ref[0]
    prev_row = jnp.where(j <= 1, 0.0, zp_ref[0, 7:8, :])
    next_row = jnp.where((j == 0) | (j == last), 0.0, zn_ref[0, 0:1, :])
    mu_p, mu_n = mu_ref[0:1, :], mu_ref[1:2, :]
    zs = z + mu_p * (pltpu.roll(z, 1, 0) - z) + mu_n * (pltpu.roll(z, tt - 1, 0) - z)
    r8 = lax.broadcasted_iota(jnp.int32, (8, 1), 0)
    head = zs[:8] + jnp.where(r8 == 0, mu_p * (prev_row - z[tt - 1:tt]), 0.0)
    tail = zs[tt - 8:] + jnp.where(r8 == 7, mu_n * (next_row - z[0:1]), 0.0)
    zs = jnp.concatenate([head, zs[8:tt - 8], tail], axis=0)

    W = RW_W
    r, k, v = zs[:, 0:W], zs[:, W:2 * W], zs[:, 2 * W:3 * W]
    base = 3 * W
    wd = (zs[:, base:base + LANE], zs[:, base + LANE:base + 2 * LANE])
    ad = zs[:, base + 2 * LANE:base + 3 * LANE]
    gd = zs[:, base + 3 * LANE:base + 4 * LANE]
    for d, lw_o in enumerate((lwf_o, lwb_o)):
        y = w0_ref[d:d + 1, :] + _dot(jnp.tanh(wd[d]), wup_ref[d])
        lw_o[0] = -math.exp(-0.5) * jax.nn.sigmoid(y)
    a = jax.nn.sigmoid(a0_ref[...] + _dot(ad, aup_ref[...]))
    g = _dot(jax.nn.sigmoid(gd), gup_ref[...])
    ones = ones_ref[...]
    kk = k * kk_ref[...]
    ss = _split_dot_r(kk * kk, ones, 2)
    kk = kk / jnp.maximum(jnp.sqrt(ss), 1e-12)
    kmod = k * (1.0 + (a - 1.0) * ka_ref[...])
    bonus = _split_dot_r(r * kmod * rk_ref[...], ones, 2) * v
    r_o[0] = r
    k_o[0] = kmod
    v_o[0] = v
    kk_o[0] = kk
    a_o[0] = a
    g_o[0] = g
    bonus_o[0] = bonus


def _rw_features(z_rw, p, n_ctx):
    B, S, Z = z_rw.shape
    tt = n_ctx
    nt = S // tt
    n8 = S // 8
    W = RW_W
    full = lambda shape: pl.BlockSpec(shape, lambda b, j: (0,) * len(shape))
    out_spec = pl.BlockSpec((1, tt, W), lambda b, j: (b, j, 0))
    return pl.pallas_call(
        functools.partial(_rw_feat_kernel, tt=tt),
        grid=(B, nt),
        in_specs=[pl.BlockSpec((1, tt, Z), lambda b, j: (b, j, 0)),
                  pl.BlockSpec((1, 8, Z), lambda b, j: (b, jnp.maximum(j * (tt // 8) - 1, 0), 0)),
                  pl.BlockSpec((1, 8, Z), lambda b, j: (b, jnp.minimum((j + 1) * (tt // 8), n8 - 1), 0)),
                  full((2, Z)), full((2, W)), full((2, LANE, W)), full((1, W)), full((LANE, W)), full((LANE, W)),
                  full((1, W)), full((1, W)), full((1, W)), full((W, W))],
        out_specs=[out_spec] * 9,
        out_shape=[jax.ShapeDtypeStruct((B, S, W), F32)] * 9,
        compiler_params=_cp("parallel", "parallel"),
        name="rwkv_features",
    )(z_rw, z_rw, z_rw, p["mu"], p["w0"], p["wup"], p["a0"], p["aup"], p["gup"], p["k_k"], p["k_a"], p["r_k"],
      p["ones64"])


def _rw_scan_kernel(r_f, lw_f, k_f, v_f, kk_f, a_f, r_b, lw_b, k_b, v_b, kk_b, a_b, of_ref, ob_ref, s_scr):
    @pl.when(pl.program_id(1) == 0)
    def _():
        s_scr[...] = jnp.zeros_like(s_scr)

    L = CHUNK
    nb = r_f.shape[0]
    ti, ji = _pair_ids(L)
    lane0 = lax.broadcasted_iota(jnp.int32, (1, 2 * L), 1) < L
    incl_w = (ji <= ti, ji >= ti)
    strict_w = (ji < ti, ji > ti)
    streams = [(bb, dr) for bb in range(nb) for dr in range(2)]
    at, rt, bh, kh, bl, kl, vb, g_tot = ([] for _ in range(8))
    for bb, dr in streams:
        r_ref, lw_ref, k_ref, v_ref, kk_ref, a_ref = ((r_f, lw_f, k_f, v_f, kk_f, a_f),
                                                      (r_b, lw_b, k_b, v_b, kk_b, a_b))[dr]
        m_incl, _ = _tri_masks(L, dr == 1)
        lw = lw_ref[bb]
        cs = _split_dot(m_incl.astype(BF16), lw, 3)
        tot = cs[0:1] if dr == 1 else cs[L - 1:L]
        kk, k = kk_ref[bb], k_ref[bb]
        bv = kk * a_ref[bb]
        g_inv = jnp.exp(-cs)
        g_rem = jnp.exp(tot - cs)
        g_tot.append(jnp.exp(tot))
        vb.append(v_ref[bb].astype(BF16))
        at.append((-(jnp.exp(cs - lw) * kk)).astype(BF16))
        rt.append((jnp.exp(cs) * r_ref[bb]).astype(BF16))
        bh.append((g_inv * bv).astype(BF16))
        kh.append((g_inv * k).astype(BF16))
        bl.append((g_rem * bv).astype(BF16))
        kl.append((g_rem * k).astype(BF16))
    es = [(d, p) for d in range(len(streams)) for p in range(RW_HEADS // 2)]
    ids = range(len(es))
    drs = [streams[d][1] for d, _ in es]
    ps = lambda p: slice(p * 2 * RW_DH, (p + 1) * 2 * RW_DH)
    bd = lambda x: _pair_blockdiag(x.astype(BF16), lane0)
    same_head = (lax.broadcasted_iota(jnp.int32, (2 * L, 2 * L), 0) // L) == (
        lax.broadcasted_iota(jnp.int32, (2 * L, 2 * L), 1) // L)
    s0 = [s_scr[d, p] for d, p in es]
    lhs = [jnp.concatenate([at[d][:, ps(p)], rt[d][:, ps(p)]], axis=0) for d, p in es]
    rhs = [jnp.concatenate([bd(bh[d][:, ps(p)]), bd(kh[d][:, ps(p)])], axis=0) for d, p in es]
    vbd = [bd(vb[d][:, ps(p)]) for d, p in es]
    sc = [_dot_nt(lhs[i], rhs[i]) for i in ids]
    zs = [_dot_nt(lhs[i], s0[i]) for i in ids]
    mab = [jnp.where(strict_w[drs[i]], sc[i][:L, :2 * L], 0.0) for i in ids]
    mak = [jnp.where(strict_w[drs[i]], sc[i][:L, 2 * L:], 0.0) for i in ids]
    nrb = [jnp.where(incl_w[drs[i]], sc[i][L:, :2 * L], 0.0) for i in ids]
    nrk = [jnp.where(incl_w[drs[i]], sc[i][L:, 2 * L:], 0.0) for i in ids]
    z0 = [zs[i][:L] + _dot(mak[i], vbd[i]) for i in ids]
    tinv = _unit_tri_inverse(mab, (ti, ji), lane0)
    u = [_dot(tinv[i], bd(z0[i])) for i in ids]
    o1 = [_dot(nrk[i], vbd[i]) for i in ids]
    o2 = [_dot(nrb[i], bd(u[i])) for i in ids]
    upd = [_dot_tn(jnp.concatenate([u[i].astype(BF16), vb[d][:, ps(p)]], axis=0),
                   jnp.concatenate([bl[d][:, ps(p)], kl[d][:, ps(p)]], axis=0)) for i, (d, p) in enumerate(es)]
    for i, (d, p) in enumerate(es):
        bb, dr = streams[d]
        (of_ref, ob_ref)[dr][bb, :, ps(p)] = zs[i][L:] + o1[i] + o2[i]
        s_scr[d, p] = s0[i] * g_tot[d][:, ps(p)] + jnp.where(same_head, upd[i], 0.0)


def _rw_scan(r, lwf, lwb, k, v, kk, a, n_ctx):
    B, S, W = r.shape
    nc, fwd, bwd = _scan_index_maps(S, n_ctx)
    nb = _scan_batch(B, RW_SCAN_BATCH)
    spec = lambda cidx: pl.BlockSpec((nb, CHUNK, W), lambda b, s: (b, cidx(s), 0))
    return pl.pallas_call(
        _rw_scan_kernel,
        grid=(B // nb, nc),
        in_specs=[spec(fwd)] * 6 + [spec(bwd)] * 6,
        out_specs=[spec(fwd), spec(bwd)],
        out_shape=[jax.ShapeDtypeStruct((B, S, W), F32)] * 2,
        scratch_shapes=[pltpu.VMEM((2 * nb, RW_HEADS // 2, 2 * RW_DH, 2 * RW_DH), F32)],
        compiler_params=_cp("parallel", "arbitrary"),
        name="rwkv_scan",
    )(r, lwf, k, v, kk, a, r, lwb, k, v, kk, a)


def _even_out_kernel(x_ref, ogf, ogb, gate_ref, orf, orb, bonus_ref, grw_ref, glag, lng, lnb, ones128, ones64,
                     w_ref, gl, gc, o_ref, *, tm, n_ctx):
    og = ogf[0] + ogb[0]
    ms = _split_dot_r(og * og, ones128[...], 2) * (1.0 / GLA_DV)
    gt = gate_ref[0]
    y_gla = og * lax.rsqrt(ms + 1e-6) * glag[...] * (gt * jax.nn.sigmoid(gt))
    of = orf[0] + orb[0]
    o64 = ones64[...]
    mean = _split_dot_r(of, o64, 3) * (1.0 / RW_DH)
    cen = of - mean
    var = _split_dot_r(cen * cen, o64, 2) * (1.0 / RW_DH)
    y_rw = (cen * lax.rsqrt(var + RW_GN_EPS) * lng[...] + lnb[...] + bonus_ref[0]) * grw_ref[0]
    y = jnp.concatenate([y_gla, y_rw], axis=1).astype(BF16)
    rows = pl.program_id(1) * tm + lax.broadcasted_iota(jnp.int32, (tm, 1), 0)
    gate = jnp.where(rows < n_ctx, gc[0], gl[0])
    o_ref[0] = x_ref[0] + gate * jnp.dot(y, w_ref[...], preferred_element_type=F32)


def _even_out(xs, og_f, og_b, z_gla, or_f, or_b, bonus, g_rw, p, mod, n_ctx, *, tm):
    B, S, D = xs.shape
    W = RW_W
    row = lambda width: pl.BlockSpec((1, tm, width), lambda b, j: (b, j, 0))
    full = lambda shape: pl.BlockSpec(shape, lambda b, j: (0,) * len(shape))
    return pl.pallas_call(
        functools.partial(_even_out_kernel, tm=tm, n_ctx=n_ctx),
        grid=(B, S // tm),
        in_specs=[row(D), row(GLA_V), row(GLA_V),
                  pl.BlockSpec((1, tm, GLA_V), lambda b, j: (b, j, (2 * GLA_QK + GLA_V) // GLA_V)),
                  row(W), row(W), row(W), row(W),
                  full((1, GLA_V)), full((1, W)), full((1, W)), full((GLA_V, GLA_V)), full((W, W)),
                  full((GLA_V + W, D)), *_mod_specs(B, D, 2, 2)],
        out_specs=row(D),
        out_shape=jax.ShapeDtypeStruct((B, S, D), F32),
        compiler_params=_cp("parallel", "parallel"),
        name="even_out",
    )(xs, og_f, og_b, z_gla, or_f, or_b, bonus, g_rw, p["gla_g"], p["ln_g"], p["ln_b"], p["ones128"], p["ones64"],
      p["w_out"], mod, mod)


def _odd_in_kernel(h_ref, w_ref, cos_ref, sin_ref, qg_ref, kg_ref, ones64, qc_o, kc_o, vc_o, qd_o, kd_o, vd_o):
    h = h_ref[0]
    cos, sin = cos_ref[...], sin_ref[...]
    o64 = ones64[...]
    nq, nk, nd = C_HEADS * HEAD_DIM, C_KV * HEAD_DIM, NA_HEADS * HEAD_DIM

    def piece(lo, width):
        return jnp.dot(h, w_ref[:, lo:lo + width], preferred_element_type=F32)

    def norm_rope(x, g, width):
        ms = _split_dot_r(x * x, o64[:width, :width], 2) * (1.0 / HEAD_DIM)
        y = x * lax.rsqrt(ms + 1e-6) * g
        lane = lax.broadcasted_iota(jnp.int32, (1, width), 1)
        first = (lane % (HEAD_DIM // 2)) < (HEAD_DIM // 4)
        swapped = jnp.where(first, pltpu.roll(y, width - HEAD_DIM // 4, 1), pltpu.roll(y, HEAD_DIM // 4, 1))
        return y * cos[:, :width] + swapped * sin[:, :width]

    qc_o[0] = (norm_rope(piece(0, nq), qg_ref[...], nq) * HEAD_DIM ** -0.5).astype(BF16)
    kc_o[0] = norm_rope(piece(nq, nk), kg_ref[:, :nk], nk).astype(BF16)
    vc_o[0] = piece(nq + nk, nk).astype(BF16)
    base = nq + 2 * nk
    qd_o[0] = (piece(base, nd) * HEAD_DIM ** -0.5).astype(BF16)
    kd_o[0] = piece(base + nd, nd).astype(BF16)
    vd_o[0] = piece(base + 2 * nd, nd).astype(BF16)


def _odd_in(h, w, cos_t, sin_t, qg, kg, ones64, *, tm):
    B, S, D = h.shape
    nq, nk, nd = C_HEADS * HEAD_DIM, C_KV * HEAD_DIM, NA_HEADS * HEAD_DIM
    full = lambda shape: pl.BlockSpec(shape, lambda b, j: (0,) * len(shape))
    row = lambda width: pl.BlockSpec((1, tm, width), lambda b, j: (b, j, 0))
    widths = (nq, nk, nk, nd, nd, nd)
    return pl.pallas_call(
        _odd_in_kernel,
        grid=(B, S // tm),
        in_specs=[row(D), _resident(w.shape),
                  pl.BlockSpec((tm, nq), lambda b, j: (j, 0)), pl.BlockSpec((tm, nq), lambda b, j: (j, 0)),
                  full((1, nq)), full((1, nq)), full((nq, nq))],
        out_specs=[row(wd) for wd in widths],
        out_shape=[jax.ShapeDtypeStruct((B, S, wd), BF16) for wd in widths],
        compiler_params=_cp("parallel", "parallel"),
        name="odd_in",
    )(h, w, cos_t, sin_t, qg, kg, ones64)


def _pair_attn_kernel(q_ref, k_ref, v_ref, o_ref, *, shared_kv):
    q = q_ref[0]
    tq = q.shape[0]
    ngroups = q.shape[1] // LANE
    lane = lax.broadcasted_iota(jnp.int32, (1, LANE), 1)
    masks = (lane < HEAD_DIM, lane >= HEAD_DIM)
    zero = jnp.zeros((), q.dtype)
    if shared_kv:
        kp, vp = k_ref[0], v_ref[0]
        es = [(g, hm) for g in range(ngroups) for hm in range(2)]
        ids = range(len(es))
        lhs = [jnp.where(masks[hm], q[:, g * LANE:(g + 1) * LANE], zero) for g, hm in es]
        s = [_dot_nt(lhs[i], kp) for i in ids]
        p = [jnp.exp(s[i] - jnp.max(s[i], axis=-1, keepdims=True)) for i in ids]
        den = [jnp.sum(p[i], axis=-1, keepdims=True) for i in ids]
        o = [_dot(p[i], vp) / den[i] for i in ids]
        for g in range(ngroups):
            o_ref[0, :, g * LANE:(g + 1) * LANE] = jnp.where(masks[0], o[2 * g], o[2 * g + 1]).astype(o_ref.dtype)
    else:
        for g in range(ngroups):
            qp = q[:, g * LANE:(g + 1) * LANE]
            kp = k_ref[0, :, g * LANE:(g + 1) * LANE]
            vp = v_ref[0, :, g * LANE:(g + 1) * LANE]
            lhs = jnp.concatenate([jnp.where(masks[0], qp, zero), jnp.where(masks[1], qp, zero)], axis=0)
            s = _dot_nt(lhs, kp)
            p = jnp.exp(s - jnp.max(s, axis=-1, keepdims=True))
            o = _dot(p, vp) / jnp.sum(p, axis=-1, keepdims=True)
            o_ref[0, :, g * LANE:(g + 1) * LANE] = jnp.where(masks[0], o[:tq], o[tq:]).astype(o_ref.dtype)


def _pair_attention(q, k, v, *, q_row0, n_q, n_k, tq, shared_kv, name):
    B, S, QW = q.shape
    KW = k.shape[2]
    return pl.pallas_call(
        functools.partial(_pair_attn_kernel, shared_kv=shared_kv),
        grid=(B, n_q // tq),
        in_specs=[pl.BlockSpec((1, tq, QW), lambda b, i: (b, q_row0 // tq + i, 0)),
                  pl.BlockSpec((1, n_k, KW), lambda b, i: (b, 0, 0)),
                  pl.BlockSpec((1, n_k, KW), lambda b, i: (b, 0, 0))],
        out_specs=pl.BlockSpec((1, tq, QW), lambda b, i: (b, i, 0)),
        out_shape=jax.ShapeDtypeStruct((B, n_q, QW), BF16),
        compiler_params=_cp("parallel", "parallel"),
        name=name,
    )(q, k, v)


NA_ROWS_PER_STEP = 2


def _na_kernel(q_ref, k_ref, v_ref, *rest, n_ctx, n_rows, kh, rps):
    bias_refs, o_ref = rest[:rps], rest[rps]
    nkeys = kh * GRID_W
    lane = lax.broadcasted_iota(jnp.int32, (1, LANE), 1)
    m0 = lane < HEAD_DIM
    zero = jnp.zeros((), q_ref.dtype)
    starts = []
    for j in range(rps):
        r = pl.program_id(1) * rps + j
        rs = jnp.clip(r - kh // 2, 0, n_rows - kh)
        starts.append(pl.multiple_of(n_ctx + rs * GRID_W, GRID_W))
    es = [(j, g) for j in range(rps) for g in range(NA_HEADS // 2)]
    ids = range(len(es))
    gs = lambda g: slice(g * LANE, (g + 1) * LANE)
    qrow = lambda j, g: q_ref[0, j * GRID_W:(j + 1) * GRID_W, gs(g)]
    lhs = [jnp.concatenate([jnp.where(m0, qrow(j, g), zero), jnp.where(m0, zero, qrow(j, g))], axis=0)
           for j, g in es]
    s_nb = [_dot_nt(lhs[i], k_ref[0, pl.ds(starts[j], nkeys), gs(g)]) for i, (j, g) in enumerate(es)]
    s_cx = [_dot_nt(lhs[i], k_ref[0, 0:n_ctx, gs(g)]) for i, (j, g) in enumerate(es)]
    s_nb = [s_nb[i] + jnp.concatenate([bias_refs[j][0, 2 * g], bias_refs[j][0, 2 * g + 1]], axis=0)
            for i, (j, g) in enumerate(es)]
    m = [jnp.maximum(jnp.max(s_nb[i], axis=-1, keepdims=True), jnp.max(s_cx[i], axis=-1, keepdims=True))
         for i in ids]
    p_nb = [jnp.exp(s_nb[i] - m[i]) for i in ids]
    p_cx = [jnp.exp(s_cx[i] - m[i]) for i in ids]
    den = [jnp.sum(p_nb[i], axis=-1, keepdims=True) + jnp.sum(p_cx[i], axis=-1, keepdims=True) for i in ids]
    o_nb = [_dot(p_nb[i], v_ref[0, pl.ds(starts[j], nkeys), gs(g)]) for i, (j, g) in enumerate(es)]
    o_cx = [_dot(p_cx[i], v_ref[0, 0:n_ctx, gs(g)]) for i, (j, g) in enumerate(es)]
    for i, (j, g) in enumerate(es):
        o = (o_nb[i] + o_cx[i]) / den[i]
        o_ref[0, j * GRID_W:(j + 1) * GRID_W, gs(g)] = jnp.where(m0, o[:GRID_W], o[GRID_W:]).astype(o_ref.dtype)


def _na_attention(qd, kd, vd, bias_tab, n_ctx):
    B, S, W = qd.shape
    T = S - n_ctx
    n_rows = T // GRID_W
    kh = min(NA_KH, n_rows)
    nkeys = kh * GRID_W
    rps = NA_ROWS_PER_STEP if n_rows % NA_ROWS_PER_STEP == 0 and n_ctx % (NA_ROWS_PER_STEP * GRID_W) == 0 else 1
    tq = rps * GRID_W

    def bias_spec(j):
        def idx(b, i):
            r = i * rps + j
            return (jnp.clip(r - kh // 2, 0, n_rows - kh) - r + kh - 1, 0, 0, 0)
        return pl.BlockSpec((1, NA_HEADS, GRID_W, nkeys), idx)

    return pl.pallas_call(
        functools.partial(_na_kernel, n_ctx=n_ctx, n_rows=n_rows, kh=kh, rps=rps),
        grid=(B, n_rows // rps),
        in_specs=[pl.BlockSpec((1, tq, W), lambda b, i: (b, n_ctx // tq + i, 0)),
                  pl.BlockSpec((1, S, W), lambda b, i: (b, 0, 0)),
                  pl.BlockSpec((1, S, W), lambda b, i: (b, 0, 0)),
                  *[bias_spec(j) for j in range(rps)]],
        out_specs=pl.BlockSpec((1, tq, W), lambda b, i: (b, i, 0)),
        out_shape=jax.ShapeDtypeStruct((B, T, W), BF16),
        compiler_params=_cp("parallel", "arbitrary"),
        name="na_attention",
    )(qd, kd, vd, *([bias_tab] * rps))


def _na_bias_table(rpb, n_rows):
    kh = min(NA_KH, n_rows)
    kw = min(NA_KW, GRID_W)
    col = np.arange(GRID_W)
    start = np.clip(col - kw // 2, 0, GRID_W - kw)
    in_win = (col[None, :] >= start[:, None]) & (col[None, :] < start[:, None] + kw)
    dc = np.clip(col[None, :] - col[:, None], -(NA_KW - 1), NA_KW - 1) + NA_KW - 1
    bias_cols = rpb[:, :, dc]
    slabs = []
    for st in range(kh):
        sl = bias_cols[:, st + NA_KH - kh:st + NA_KH - kh + kh]
        sl = jnp.where(in_win[None, None], sl, NEG).transpose(0, 2, 1, 3)
        slabs.append(sl.reshape(NA_HEADS, GRID_W, kh * GRID_W))
    return jnp.stack(slabs).astype(F32)


def _proj_res_kernel(x_ref, y_ref, w_ref, gl, gc, o_ref, *, tm, n_ctx):
    rows = pl.program_id(1) * tm + lax.broadcasted_iota(jnp.int32, (tm, 1), 0)
    gate = jnp.where(rows < n_ctx, gc[0], gl[0])
    o_ref[0] = x_ref[0] + gate * jnp.dot(y_ref[0], w_ref[...], preferred_element_type=F32)


def _proj_res(xs, y, w, mod, n_ctx, *, tm):
    B, S, D = xs.shape
    K = y.shape[2]
    row = lambda width: pl.BlockSpec((1, tm, width), lambda b, j: (b, j, 0))
    return pl.pallas_call(
        functools.partial(_proj_res_kernel, tm=tm, n_ctx=n_ctx),
        grid=(B, S // tm),
        in_specs=[row(D), row(K), pl.BlockSpec((K, D), lambda b, j: (0, 0)), *_mod_specs(B, D, 2, 2)],
        out_specs=row(D),
        out_shape=jax.ShapeDtypeStruct((B, S, D), F32),
        compiler_params=_cp("parallel", "parallel"),
        name="proj_res",
    )(xs, y, w, mod, mod)


def _final_norm_kernel(x_ref, g_ref, o_ref):
    x = x_ref[0]
    o_ref[0] = x * lax.rsqrt(jnp.mean(x * x, axis=-1, keepdims=True) + 1e-6) * g_ref[...]


def _final_norm(xs, g, n_ctx, *, tm):
    B, S, D = xs.shape
    T = S - n_ctx
    return pl.pallas_call(
        _final_norm_kernel,
        grid=(B, T // tm),
        in_specs=[pl.BlockSpec((1, tm, D), lambda b, j: (b, n_ctx // tm + j, 0)),
                  pl.BlockSpec((1, D), lambda b, j: (0, 0))],
        out_specs=pl.BlockSpec((1, tm, D), lambda b, j: (b, j, 0)),
        out_shape=jax.ShapeDtypeStruct((B, T, D), F32),
        compiler_params=_cp("parallel", "parallel"),
        name="final_norm",
    )(xs, g.reshape(1, D))


def _block_ones(width, block):
    idx = np.arange(width) // block
    return jnp.asarray(idx[:, None] == idx[None, :], BF16)


def _pad_cols(w, width):
    return jnp.pad(w, ((0, 0), (0, width - w.shape[1])))


def _pad_rows(w, height):
    return jnp.pad(w, ((0, height - w.shape[0]), (0, 0)))


def _even_params(j, ev_w_in, ev_w_out, gla_a_up, gla_a_bias, gla_norm_g, rw_mu, rw_w0, rw_w_up, rw_a0, rw_a_up,
                 rw_g_up, rw_k_k, rw_k_a, rw_r_k, rw_ln_g, rw_ln_b):
    w_in = ev_w_in[j]
    D = w_in.shape[0]
    gq = 2 * GLA_QK + 2 * GLA_V
    w_gla = jnp.concatenate([w_in[:, :gq], _pad_cols(w_in[:, gq:gq + GLA_RANK], LANE),
                             _pad_cols(w_in[:, gq + GLA_RANK:gq + 2 * GLA_RANK], LANE)], axis=1)
    o = gq + 2 * GLA_RANK
    sizes = (3 * RW_W, RW_DECAY_RANK, RW_DECAY_RANK, RW_A_RANK, RW_G_RANK)
    offs = np.cumsum((0,) + sizes)
    pieces = [w_in[:, o:o + 3 * RW_W]] + [_pad_cols(w_in[:, o + offs[i]:o + offs[i + 1]], LANE) for i in range(1, 5)]
    w_rw = jnp.concatenate(pieces, axis=1)
    mu = rw_mu[j]
    mu_pieces = [mu[:, :3 * RW_W]] + [_pad_cols(mu[:, offs[i]:offs[i + 1]], LANE) for i in range(1, 5)]
    return {
        "w_gla": w_gla.astype(BF16), "w_rw": w_rw.astype(BF16), "w_out": ev_w_out[j].astype(BF16),
        "gla_aup": jnp.stack([_pad_rows(gla_a_up[j, d], LANE) for d in range(2)]).astype(BF16),
        "gla_bias": gla_a_bias[j], "gla_g": jnp.tile(gla_norm_g[j], GLA_HEADS).reshape(1, GLA_V),
        "mu": jnp.concatenate(mu_pieces, axis=1), "w0": rw_w0[j],
        "wup": jnp.stack([_pad_rows(rw_w_up[j, d], LANE) for d in range(2)]).astype(BF16),
        "a0": rw_a0[j].reshape(1, RW_W), "aup": _pad_rows(rw_a_up[j], LANE).astype(BF16),
        "gup": _pad_rows(rw_g_up[j], LANE).astype(BF16),
        "k_k": rw_k_k[j].reshape(1, RW_W), "k_a": rw_k_a[j].reshape(1, RW_W), "r_k": rw_r_k[j].reshape(1, RW_W),
        "ln_g": rw_ln_g[j].reshape(1, RW_W), "ln_b": rw_ln_b[j].reshape(1, RW_W),
        "ones64": _block_ones(RW_W, RW_DH), "ones128": _block_ones(GLA_V, GLA_DV),
    }


def _gqa_head_order():
    per_kv = C_HEADS // C_KV
    order = []
    for jj in range(per_kv):
        for kv in range(C_KV):
            h = kv * per_kv + jj
            order.extend(range(h * HEAD_DIM, (h + 1) * HEAD_DIM))
    return np.asarray(order)


def _rope_tables(n_ctx, T, width):
    t = jnp.arange(T)
    pos = jnp.stack([t // GRID_W, t % GRID_W], axis=-1).astype(F32)
    half = HEAD_DIM // 2
    inv = ROPE_THETA ** (-jnp.arange(0, half, 2, dtype=F32) / half)
    ang = pos[:, :, None] * inv
    cos, sin = jnp.cos(ang), jnp.sin(ang)
    cos_h = jnp.concatenate([cos, cos], axis=-1).reshape(T, HEAD_DIM)
    sin_h = jnp.concatenate([-sin, sin], axis=-1).reshape(T, HEAD_DIM)
    cos_h = jnp.concatenate([jnp.ones((n_ctx, HEAD_DIM), F32), cos_h], axis=0)
    sin_h = jnp.concatenate([jnp.zeros((n_ctx, HEAD_DIM), F32), sin_h], axis=0)
    reps = width // HEAD_DIM
    return jnp.tile(cos_h, (1, reps)), jnp.tile(sin_h, (1, reps))


def _even_layer(xs, h, mod, n_ctx, p):
    z_gla, z_rw = _project(h, [p["w_gla"], p["w_rw"]], tm=_row_tile(xs.shape[1], ROW_CAP_FFN), name="even_in")
    og_f, og_b = _gla_scan(z_gla, p["gla_aup"], p["gla_bias"], n_ctx)
    r, lwf, lwb, k, v, kk, a, g_rw, bonus = _rw_features(z_rw, p, n_ctx)
    or_f, or_b = _rw_scan(r, lwf, lwb, k, v, kk, a, n_ctx)
    return _even_out(xs, og_f, og_b, z_gla, or_f, or_b, bonus, g_rw, p, mod, n_ctx,
                     tm=_row_tile(xs.shape[1], ROW_CAP_ELEMENTWISE))


def _odd_layer(xs, h, mod, n_ctx, p, need_ctx):
    B, S, D = xs.shape
    T = S - n_ctx
    qc, kc, vc, qd, kd, vd = _odd_in(h, p["w_in"], p["cos"], p["sin"], p["qg"], p["kg"], p["ones64"],
                                     tm=_row_tile(S, ROW_CAP_FFN))
    y_gqa = _pair_attention(qc, kc, vc, q_row0=n_ctx, n_q=T, n_k=S, tq=256, shared_kv=True, name="gqa")
    y_na = _na_attention(qd, kd, vd, p["bias_tab"], n_ctx)
    parts_g, parts_n = [y_gqa], [y_na]
    if need_ctx:
        parts_g.insert(0, _pair_attention(qc, kc, vc, q_row0=0, n_q=n_ctx, n_k=n_ctx, tq=n_ctx, shared_kv=True,
                                          name="gqa_ctx"))
        parts_n.insert(0, _pair_attention(qd, kd, vd, q_row0=0, n_q=n_ctx, n_k=n_ctx, tq=n_ctx, shared_kv=False,
                                          name="na_ctx"))
    else:
        zeros = jnp.zeros((B, n_ctx, y_gqa.shape[2]), BF16)
        parts_g.insert(0, zeros)
        parts_n.insert(0, zeros)
    y = jnp.concatenate([jnp.concatenate(parts_g, axis=1), jnp.concatenate(parts_n, axis=1)], axis=2)
    return _proj_res(xs, y, p["w_out"], mod, n_ctx, tm=_row_tile(S, ROW_CAP_MATMUL))


def kernel(x, c, ctx, c_ctx, w_mod, b_mod, norm1_g, norm2_g, ffn_w13, ffn_w2, ev_w_in, ev_w_out, gla_a_up,
           gla_a_bias, gla_norm_g, rw_mu, rw_w0, rw_w_up, rw_a0, rw_a_up, rw_g_up, rw_k_k, rw_k_a, rw_r_k, rw_ln_g,
           rw_ln_b, od_w_in, od_w_out, cq_norm_g, ck_norm_g, na_rpb, final_g):
    B, T, D = x.shape
    n_ctx = ctx.shape[1]
    S = n_ctx + T
    xs = jnp.concatenate([ctx, x], axis=1)
    mods = _mod_vectors(c, c_ctx, w_mod, b_mod)
    nq, nk = C_HEADS * HEAD_DIM, C_KV * HEAD_DIM
    order = _gqa_head_order()
    cos_t, sin_t = _rope_tables(n_ctx, T, nq)
    ones64 = _block_ones(nq, HEAD_DIM)
    h = _norm_mod(xs, norm1_g[0], mods[0], n_ctx, tm=_row_tile(S, ROW_CAP_MATMUL))
    for i in range(DEPTH):
        j = i // 2
        mod = mods[i]
        if i % 2 == 0:
            p = _even_params(j, ev_w_in, ev_w_out, gla_a_up, gla_a_bias, gla_norm_g, rw_mu, rw_w0, rw_w_up, rw_a0,
                             rw_a_up, rw_g_up, rw_k_k, rw_k_a, rw_r_k, rw_ln_g, rw_ln_b)
            xs = _even_layer(xs, h, mod, n_ctx, p)
        else:
            w_in = od_w_in[j]
            w_out = od_w_out[j]
            p = {
                "w_in": jnp.concatenate([w_in[:, :nq][:, order], w_in[:, nq:]], axis=1).astype(BF16),
                "w_out": jnp.concatenate([w_out[:nq][order], w_out[nq:]], axis=0).astype(BF16),
                "cos": cos_t, "sin": sin_t, "ones64": ones64,
                "qg": jnp.tile(cq_norm_g[j], C_HEADS).reshape(1, nq),
                "kg": jnp.tile(ck_norm_g[j], C_HEADS).reshape(1, nq),
                "bias_tab": _na_bias_table(na_rpb[j], T // GRID_W),
            }
            xs = _odd_layer(xs, h, mod, n_ctx, p, need_ctx=i < DEPTH - 1)
        next_norm = (norm1_g[i + 1], mods[i + 1]) if i + 1 < DEPTH else None
        xs, h = _ffn(xs, norm2_g[i], mod, ffn_w13[i].astype(BF16), ffn_w2[i].astype(BF16), n_ctx,
                     tm=_row_tile(S, ROW_CAP_FFN), tf=FFN_COL_TILE, next_norm=next_norm)
    return _final_norm(xs, final_g, n_ctx, tm=_row_tile(n_ctx, ROW_CAP_ELEMENTWISE))
```

```python
import functools
import math

import jax
import jax.numpy as jnp
import numpy as np
from jax import lax
from jax.experimental import pallas as pl
from jax.experimental.pallas import tpu as pltpu

F32 = jnp.float32
BF16 = jnp.bfloat16

DEPTH = 4
GRID_W = 64
CHUNK = 64
LANE = 128

GLA_HEADS, GLA_DK, GLA_DV, GLA_RANK, GLA_TAU = 4, 64, 128, 16, 16.0
GLA_QK, GLA_V = GLA_HEADS * GLA_DK, GLA_HEADS * GLA_DV
GLA_Z = 2 * GLA_QK + 2 * GLA_V + 2 * LANE

RW_HEADS, RW_DH, RW_W = 8, 64, 512
RW_DECAY_RANK, RW_A_RANK, RW_G_RANK = 32, 32, 96
RW_GN_EPS = 64e-5
RW_Z = 3 * RW_W + 4 * LANE

HEAD_DIM, C_HEADS, C_KV, NA_HEADS, NA_KH, NA_KW = 64, 8, 2, 8, 8, 16
ROPE_THETA = 10000.0
NEG = -1e30

VMEM_LIMIT = 56 * 1024 * 1024
ROW_CAP_MATMUL = 1152
ROW_CAP_FFN = 576
ROW_CAP_ELEMENTWISE = 256
FFN_COL_TILE = 256


def _row_tile(n_rows, cap):
    return max(d for d in range(8, cap + 1, 8) if n_rows % d == 0)


def _cp(*sem):
    return pltpu.CompilerParams(dimension_semantics=sem, vmem_limit_bytes=VMEM_LIMIT)


def _dot(a, b):
    return jnp.dot(a.astype(BF16), b.astype(BF16), preferred_element_type=F32)


def _dot_nt(a, b):
    return lax.dot_general(a.astype(BF16), b.astype(BF16), (((1,), (1,)), ((), ())), preferred_element_type=F32)


def _dot_tn(a, b):
    return lax.dot_general(a.astype(BF16), b.astype(BF16), (((0,), (0,)), ((), ())), preferred_element_type=F32)


def _split_dot(m, x, parts):
    acc = None
    rem = x
    for _ in range(parts):
        piece = rem.astype(BF16)
        rem = rem - piece.astype(F32)
        t = jnp.dot(m, piece, preferred_element_type=F32)
        acc = t if acc is None else acc + t
    return acc


def _split_dot_r(x, m, parts):
    acc = None
    rem = x
    for _ in range(parts):
        piece = rem.astype(BF16)
        rem = rem - piece.astype(F32)
        t = jnp.dot(piece, m, preferred_element_type=F32)
        acc = t if acc is None else acc + t
    return acc


def _rms_mod(x, g, sc_l, sh_l, sc_c, sh_c, row0, n_ctx):
    y = x * lax.rsqrt(jnp.mean(x * x, axis=-1, keepdims=True) + 1e-6) * g
    rows = row0 + lax.broadcasted_iota(jnp.int32, (x.shape[0], 1), 0)
    is_ctx = rows < n_ctx
    sc = jnp.where(is_ctx, sc_c, sc_l)
    sh = jnp.where(is_ctx, sh_c, sh_l)
    return y * (1.0 + sc) + sh


def _tri_masks(n, reverse):
    ri = lax.broadcasted_iota(jnp.int32, (n, n), 0)
    ci = lax.broadcasted_iota(jnp.int32, (n, n), 1)
    if reverse:
        return ci >= ri, ci > ri
    return ci <= ri, ci < ri


INV_BASE = 16


def _pair_ids(n):
    ti = lax.broadcasted_iota(jnp.int32, (n, 2 * n), 0)
    ji = lax.broadcasted_iota(jnp.int32, (n, 2 * n), 1) % n
    return ti, ji


def _pair_blockdiag(x, lane_is_head0):
    zero = jnp.zeros((), x.dtype)
    return jnp.concatenate([jnp.where(lane_is_head0, x, zero), jnp.where(lane_is_head0, zero, x)], axis=0)


def _unit_tri_inverse(a, ids_tj, lane_is_head0):
    ti, ji = ids_tj
    n = a[0].shape[0]
    ids = range(len(a))
    bd = lambda x: _pair_blockdiag(x.astype(BF16), lane_is_head0)
    same = lambda bs: (ti // bs) == (ji // bs)
    base = same(INV_BASE)
    eye = (ti == ji).astype(F32)
    p = [jnp.where(base, x, 0.0) for x in a]
    t = [eye + x for x in p]
    m = 1
    while 2 * m < INV_BASE:
        p = [_dot(x, bd(x)) for x in p]
        tp = [_dot(t[i], bd(p[i])) for i in ids]
        t = [t[i] + tp[i] for i in ids]
        m *= 2
    bs = INV_BASE
    while bs < n:
        mask = same(2 * bs) & jnp.logical_not(same(bs))
        ta = [_dot(t[i], bd(jnp.where(mask, a[i], 0.0))) for i in ids]
        tat = [_dot(ta[i], bd(t[i])) for i in ids]
        t = [t[i] + tat[i] for i in ids]
        bs *= 2
    return t


def _mod_kernel(s_ref, w_ref, b_ref, o_ref):
    s = s_ref[...]
    s = s * jax.nn.sigmoid(s)
    o_ref[0] = _dot(s, w_ref[0]) + b_ref[0]


def _mod_vectors(c, c_ctx, w_mod, b_mod):
    B, D = c.shape
    bp = -(-(B + 1) // 8) * 8
    s_in = jnp.zeros((bp, D), F32).at[:B].set(c).at[B].set(c_ctx)
    n6 = w_mod.shape[-1]
    tn = 1536
    out = pl.pallas_call(
        _mod_kernel,
        grid=(DEPTH, n6 // tn),
        in_specs=[pl.BlockSpec((bp, D), lambda l, n: (0, 0)),
                  pl.BlockSpec((1, D, tn), lambda l, n: (l, 0, n)),
                  pl.BlockSpec((1, 1, tn), lambda l, n: (l, 0, n))],
        out_specs=pl.BlockSpec((1, bp, tn), lambda l, n: (l, 0, n)),
        out_shape=jax.ShapeDtypeStruct((DEPTH, bp, n6), F32),
        compiler_params=_cp("parallel", "parallel"),
        name="mod_vectors",
    )(s_in, w_mod, b_mod.reshape(DEPTH, 1, n6))
    return out.reshape(DEPTH, bp, 1, n6)


def _mod_specs(B, D, idx, grid_rank):
    if grid_rank == 2:
        return [pl.BlockSpec((1, 1, D), lambda b, j: (b, 0, idx)),
                pl.BlockSpec((1, 1, D), lambda b, j: (B, 0, idx))]
    return [pl.BlockSpec((1, 1, D), lambda b, j, n: (b, 0, idx)),
            pl.BlockSpec((1, 1, D), lambda b, j, n: (B, 0, idx))]


def _norm_mod_kernel(x_ref, g_ref, shl, shc, scl, scc, o_ref, *, tm, n_ctx):
    h = _rms_mod(x_ref[0], g_ref[...], scl[0], shl[0], scc[0], shc[0], pl.program_id(1) * tm, n_ctx)
    o_ref[0] = h.astype(BF16)


def _norm_mod(xs, g, mod, n_ctx, *, tm):
    B, S, D = xs.shape
    row = pl.BlockSpec((1, tm, D), lambda b, j: (b, j, 0))
    return pl.pallas_call(
        functools.partial(_norm_mod_kernel, tm=tm, n_ctx=n_ctx),
        grid=(B, S // tm),
        in_specs=[row, pl.BlockSpec((1, D), lambda b, j: (0, 0)), *_mod_specs(B, D, 0, 2), *_mod_specs(B, D, 1, 2)],
        out_specs=row,
        out_shape=jax.ShapeDtypeStruct((B, S, D), BF16),
        compiler_params=_cp("parallel", "parallel"),
        name="norm_mod",
    )(xs, g.reshape(1, D), mod, mod, mod, mod)


def _ffn_kernel(x_ref, g_ref, shl, shc, scl, scc, gl, gc, w13_ref, w2_ref, *rest, tm, tf, n_ctx, emit_next):
    if emit_next:
        gn_ref, nshl, nshc, nscl, nscc, o_ref, hn_ref = rest
    else:
        (o_ref,) = rest
    x = x_ref[0]
    row0 = pl.program_id(1) * tm
    h = _rms_mod(x, g_ref[...], scl[0], shl[0], scc[0], shc[0], row0, n_ctx).astype(BF16)
    n_hidden = w2_ref.shape[0]
    acc = None
    for f in range(n_hidden // tf):
        a = jnp.dot(h, w13_ref[:, f * tf:(f + 1) * tf], preferred_element_type=F32)
        b = jnp.dot(h, w13_ref[:, n_hidden + f * tf:n_hidden + (f + 1) * tf], preferred_element_type=F32)
        u = (a * jax.nn.sigmoid(a) * b).astype(BF16)
        t = jnp.dot(u, w2_ref[f * tf:(f + 1) * tf, :], preferred_element_type=F32)
        acc = t if acc is None else acc + t
    rows = row0 + lax.broadcasted_iota(jnp.int32, (tm, 1), 0)
    gate = jnp.where(rows < n_ctx, gc[0], gl[0])
    y = x + gate * acc
    o_ref[0] = y
    if emit_next:
        hn_ref[0] = _rms_mod(y, gn_ref[...], nscl[0], nshl[0], nscc[0], nshc[0], row0, n_ctx).astype(BF16)


def _resident(shape):
    return pl.BlockSpec(shape, lambda b, j: (0,) * len(shape), pipeline_mode=pl.Buffered(1))


def _ffn(xs, g, mod, w13, w2, n_ctx, *, tm, tf, next_norm=None):
    B, S, D = xs.shape
    row = pl.BlockSpec((1, tm, D), lambda b, j: (b, j, 0))
    vec = pl.BlockSpec((1, D), lambda b, j: (0, 0))
    in_specs = [row, vec, *_mod_specs(B, D, 3, 2), *_mod_specs(B, D, 4, 2), *_mod_specs(B, D, 5, 2),
                _resident(w13.shape), _resident(w2.shape)]
    args = [xs, g.reshape(1, D), mod, mod, mod, mod, mod, mod, w13, w2]
    out_specs, out_shape = [row], [jax.ShapeDtypeStruct((B, S, D), F32)]
    if next_norm is not None:
        g_next, mod_next = next_norm
        in_specs += [vec, *_mod_specs(B, D, 0, 2), *_mod_specs(B, D, 1, 2)]
        args += [g_next.reshape(1, D), mod_next, mod_next, mod_next, mod_next]
        out_specs.append(row)
        out_shape.append(jax.ShapeDtypeStruct((B, S, D), BF16))
    out = pl.pallas_call(
        functools.partial(_ffn_kernel, tm=tm, tf=tf, n_ctx=n_ctx, emit_next=next_norm is not None),
        grid=(B, S // tm),
        in_specs=in_specs,
        out_specs=out_specs,
        out_shape=out_shape,
        compiler_params=_cp("parallel", "parallel"),
        name="ffn",
    )(*args)
    return (out[0], out[1]) if next_norm is not None else (out[0], None)


def _chunk_index(s, nc_ctx, nc, reverse):
    if not reverse:
        return s
    return jnp.where(s < nc_ctx, nc_ctx - 1 - s, nc + nc_ctx - 1 - s)


def _gla_kernel(qkv_f, ad_f, qkv_b, ad_b, aup_ref, bias_ref, of_ref, ob_ref, st_scr):
    @pl.when(pl.program_id(1) == 0)
    def _():
        st_scr[...] = jnp.zeros_like(st_scr)

    L = CHUNK
    nb = qkv_f.shape[0]
    streams = [(bb, d) for bb in range(nb) for d in range(2)]
    qb, kb, kt, gt, v, incl = [], [], [], [], [], []
    for bb, d in streams:
        qkv_ref, ad_ref = ((qkv_f, ad_f), (qkv_b, ad_b))[d]
        m, _ = _tri_masks(L, d == 1)
        ad = ad_ref[bb][:, d * LANE:(d + 1) * LANE]
        y = _dot(ad, aup_ref[d]) + bias_ref[d:d + 1, :]
        la = (jnp.minimum(y, 0.0) - jnp.log(1.0 + jnp.exp(-jnp.abs(y)))) * (1.0 / GLA_TAU)
        cs = _split_dot(m.astype(BF16), la, 3)
        tot = cs[0:1] if d == 1 else cs[L - 1:L]
        qkv = qkv_ref[bb]
        q = qkv[:, 0:GLA_QK] * GLA_DK ** -0.5
        k = qkv[:, GLA_QK:2 * GLA_QK]
        incl.append(m)
        v.append(qkv[:, 2 * GLA_QK:2 * GLA_QK + GLA_V].astype(BF16))
        qb.append((q * jnp.exp(cs)).astype(BF16))
        kb.append((k * jnp.exp(-cs)).astype(BF16))
        kt.append((k * jnp.exp(tot - cs)).astype(BF16))
        gt.append(jnp.exp(tot))
    es = [(c, h) for c in range(len(streams)) for h in range(GLA_HEADS)]
    ks = lambda h: slice(h * GLA_DK, (h + 1) * GLA_DK)
    vs = lambda h: slice(h * GLA_DV, (h + 1) * GLA_DV)
    st = [st_scr[c, h] for c, h in es]
    att = [jnp.where(incl[c], _dot_nt(qb[c][:, ks(h)], kb[c][:, ks(h)]), 0.0) for c, h in es]
    o_state = [_dot_nt(qb[c][:, ks(h)], st[i]) for i, (c, h) in enumerate(es)]
    o_att = [_dot(att[i], v[c][:, vs(h)]) for i, (c, h) in enumerate(es)]
    upd = [_dot_tn(v[c][:, vs(h)], kt[c][:, ks(h)]) for c, h in es]
    for i, (c, h) in enumerate(es):
        bb, d = streams[c]
        (of_ref, ob_ref)[d][bb, :, vs(h)] = o_att[i] + o_state[i]
        st_scr[c, h] = st[i] * gt[c][:, ks(h)] + upd[i]


GLA_SCAN_BATCH = 4
RW_SCAN_BATCH = 4


def _scan_batch(B, want):
    return want if B % want == 0 else 1


def _scan_index_maps(S, n_ctx):
    nc, nc_ctx = S // CHUNK, n_ctx // CHUNK
    fwd = functools.partial(_chunk_index, nc_ctx=nc_ctx, nc=nc, reverse=False)
    bwd = functools.partial(_chunk_index, nc_ctx=nc_ctx, nc=nc, reverse=True)
    return nc, fwd, bwd


def _gla_scan(z_gla, aup_pad, bias, n_ctx):
    B, S, _ = z_gla.shape
    nc, fwd, bwd = _scan_index_maps(S, n_ctx)
    ad_blk = (2 * GLA_QK + 2 * GLA_V) // (2 * LANE)
    nb = _scan_batch(B, GLA_SCAN_BATCH)
    qkv_spec =lambda cidx: pl.BlockSpec((nb, CHUNK, 2 * GLA_QK + GLA_V), lambda b, s: (b, cidx(s), 0))
    ad_spec = lambda cidx: pl.BlockSpec((nb, CHUNK, 2 * LANE), lambda b, s: (b, cidx(s), ad_blk))
    out_spec = lambda cidx: pl.BlockSpec((nb, CHUNK, GLA_V), lambda b, s: (b, cidx(s), 0))
    return pl.pallas_call(
        _gla_kernel,
        grid=(B // nb, nc),
        in_specs=[qkv_spec(fwd), ad_spec(fwd), qkv_spec(bwd), ad_spec(bwd),
                  pl.BlockSpec((2, LANE, GLA_QK), lambda b, s: (0, 0, 0)),
                  pl.BlockSpec((2, GLA_QK), lambda b, s: (0, 0))],
        out_specs=[out_spec(fwd), out_spec(bwd)],
        out_shape=[jax.ShapeDtypeStruct((B, S, GLA_V), F32)] * 2,
        scratch_shapes=[pltpu.VMEM((2 * nb, GLA_HEADS, GLA_DV, GLA_DK), F32)],
        compiler_params=_cp("parallel", "arbitrary"),
        name="gla_scan",
    )(z_gla, z_gla, z_gla, z_gla, aup_pad, bias)


def _even_in_kernel(h_ref, hp_ref, hn_ref, wg_ref, wr_ref, mu_ref, w0_ref, wup_ref, a0_ref, aup_ref, gup_ref,
                    kk_ref, ka_ref, rk_ref, ones_ref, zg_o, r_o, lwf_o, lwb_o, k_o, v_o, kk_o, a_o, g_o, bonus_o,
                    *, tt, halo):
    j = pl.program_id(1)
    last = pl.num_programs(1) - 1
    h = h_ref[0]
    z_ext = jnp.dot(jnp.concatenate([hp_ref[0], h, hn_ref[0]], axis=0), wr_ref[...], preferred_element_type=F32)
    zg_o[0] = jnp.dot(h, wg_ref[...], preferred_element_type=F32)
    z = z_ext[halo:halo + tt]
    prev_row = jnp.where(j <= 1, 0.0, z_ext[halo - 1:halo])
    next_row = jnp.where((j == 0) | (j == last), 0.0, z_ext[halo + tt:halo + tt + 1])
    mu_p, mu_n = mu_ref[0:1, :], mu_ref[1:2, :]
    zs = z + mu_p * (pltpu.roll(z, 1, 0) - z) + mu_n * (pltpu.roll(z, tt - 1, 0) - z)
    r8 = lax.broadcasted_iota(jnp.int32, (8, 1), 0)
    head = zs[:8] + jnp.where(r8 == 0, mu_p * (prev_row - z[tt - 1:tt]), 0.0)
    tail = zs[tt - 8:] + jnp.where(r8 == 7, mu_n * (next_row - z[0:1]), 0.0)
    zs = jnp.concatenate([head, zs[8:tt - 8], tail], axis=0)

    W = RW_W
    r, k, v = zs[:, 0:W], zs[:, W:2 * W], zs[:, 2 * W:3 * W]
    base = 3 * W
    wd = (zs[:, base:base + LANE], zs[:, base + LANE:base + 2 * LANE])
    ad = zs[:, base + 2 * LANE:base + 3 * LANE]
    gd = zs[:, base + 3 * LANE:base + 4 * LANE]
    for d, lw_o in enumerate((lwf_o, lwb_o)):
        y = w0_ref[d:d + 1, :] + _dot(jnp.tanh(wd[d]), wup_ref[d])
        lw_o[0] = -math.exp(-0.5) * jax.nn.sigmoid(y)
    a = jax.nn.sigmoid(a0_ref[...] + _dot(ad, aup_ref[...]))
    g = _dot(jax.nn.sigmoid(gd), gup_ref[...])
    ones = ones_ref[...]
    kk = k * kk_ref[...]
    ss = _split_dot_r(kk * kk, ones, 2)
    kk = kk / jnp.maximum(jnp.sqrt(ss), 1e-12)
    kmod = k * (1.0 + (a - 1.0) * ka_ref[...])
    bonus = _split_dot_r(r * kmod * rk_ref[...], ones, 2) * v
    r_o[0] = r
    k_o[0] = kmod
    v_o[0] = v
    kk_o[0] = kk
    a_o[0] = a
    g_o[0] = g
    bonus_o[0] = bonus


def _even_in(h, p, n_ctx):
    B, S, D = h.shape
    tt = n_ctx
    halo = 16
    nt, nh = S // tt, S // halo
    W = RW_W
    full = lambda shape: pl.BlockSpec(shape, lambda b, j: (0,) * len(shape))
    row = lambda width: pl.BlockSpec((1, tt, width), lambda b, j: (b, j, 0))
    wg, wr = p["w_gla"], p["w_rw"]
    Z = wr.shape[1]
    return pl.pallas_call(
        functools.partial(_even_in_kernel, tt=tt, halo=halo),
        grid=(B, nt),
        in_specs=[row(D),
                  pl.BlockSpec((1, halo, D), lambda b, j: (b, jnp.maximum(j * (tt // halo) - 1, 0), 0)),
                  pl.BlockSpec((1, halo, D), lambda b, j: (b, jnp.minimum((j + 1) * (tt // halo), nh - 1), 0)),
                  _resident(wg.shape), _resident(wr.shape),
                  full((2, Z)), full((2, W)), full((2, LANE, W)), full((1, W)), full((LANE, W)), full((LANE, W)),
                  full((1, W)), full((1, W)), full((1, W)), full((W, W))],
        out_specs=[row(wg.shape[1])] + [row(W)] * 9,
        out_shape=[jax.ShapeDtypeStruct((B, S, wg.shape[1]), F32)] + [jax.ShapeDtypeStruct((B, S, W), F32)] * 9,
        compiler_params=_cp("parallel", "parallel"),
        name="even_in",
    )(h, h, h, wg, wr, p["mu"], p["w0"], p["wup"], p["a0"], p["aup"], p["gup"], p["k_k"], p["k_a"], p["r_k"],
      p["ones64"])


def _rw_scan_kernel(r_f, lw_f, k_f, v_f, kk_f, a_f, r_b, lw_b, k_b, v_b, kk_b, a_b, of_ref, ob_ref, s_scr):
    @pl.when(pl.program_id(1) == 0)
    def _():
        s_scr[...] = jnp.zeros_like(s_scr)

    L = CHUNK
    nb = r_f.shape[0]
    ti, ji = _pair_ids(L)
    lane0 = lax.broadcasted_iota(jnp.int32, (1, 2 * L), 1) < L
    incl_w = (ji <= ti, ji >= ti)
    strict_w = (ji < ti, ji > ti)
    streams = [(bb, dr) for bb in range(nb) for dr in range(2)]
    at, rt, bh, kh, bl, kl, vb, g_tot = ([] for _ in range(8))
    for bb, dr in streams:
        r_ref, lw_ref, k_ref, v_ref, kk_ref, a_ref = ((r_f, lw_f, k_f, v_f, kk_f, a_f),
                                                      (r_b, lw_b, k_b, v_b, kk_b, a_b))[dr]
        m_incl, _ = _tri_masks(L, dr == 1)
        lw = lw_ref[bb]
        cs = _split_dot(m_incl.astype(BF16), lw, 3)
        tot = cs[0:1] if dr == 1 else cs[L - 1:L]
        kk, k = kk_ref[bb], k_ref[bb]
        bv = kk * a_ref[bb]
        g_inv = jnp.exp(-cs)
        g_rem = jnp.exp(tot - cs)
        g_tot.append(jnp.exp(tot))
        vb.append(v_ref[bb].astype(BF16))
        at.append((-(jnp.exp(cs - lw) * kk)).astype(BF16))
        rt.append((jnp.exp(cs) * r_ref[bb]).astype(BF16))
        bh.append((g_inv * bv).astype(BF16))
        kh.append((g_inv * k).astype(BF16))
        bl.append((g_rem * bv).astype(BF16))
        kl.append((g_rem * k).astype(BF16))
    es = [(d, p) for d in range(len(streams)) for p in range(RW_HEADS // 2)]
    ids = range(len(es))
    drs = [streams[d][1] for d, _ in es]
    ps = lambda p: slice(p * 2 * RW_DH, (p + 1) * 2 * RW_DH)
    bd = lambda x: _pair_blockdiag(x.astype(BF16), lane0)
    same_head = (lax.broadcasted_iota(jnp.int32, (2 * L, 2 * L), 0) // L) == (
        lax.broadcasted_iota(jnp.int32, (2 * L, 2 * L), 1) // L)
    s0 = [s_scr[d, p] for d, p in es]
    lhs = [jnp.concatenate([at[d][:, ps(p)], rt[d][:, ps(p)]], axis=0) for d, p in es]
    rhs = [jnp.concatenate([bd(bh[d][:, ps(p)]), bd(kh[d][:, ps(p)])], axis=0) for d, p in es]
    vbd = [bd(vb[d][:, ps(p)]) for d, p in es]
    sc = [_dot_nt(lhs[i], rhs[i]) for i in ids]
    zs = [_dot_nt(lhs[i], s0[i]) for i in ids]
    mab = [jnp.where(strict_w[drs[i]], sc[i][:L, :2 * L], 0.0) for i in ids]
    mak = [jnp.where(strict_w[drs[i]], sc[i][:L, 2 * L:], 0.0) for i in ids]
    nrb = [jnp.where(incl_w[drs[i]], sc[i][L:, :2 * L], 0.0) for i in ids]
    nrk = [jnp.where(incl_w[drs[i]], sc[i][L:, 2 * L:], 0.0) for i in ids]
    z0 = [zs[i][:L] + _dot(mak[i], vbd[i]) for i in ids]
    tinv = _unit_tri_inverse(mab, (ti, ji), lane0)
    u = [_dot(tinv[i], bd(z0[i])) for i in ids]
    o1 = [_dot(nrk[i], vbd[i]) for i in ids]
    o2 = [_dot(nrb[i], bd(u[i])) for i in ids]
    upd = [_dot_tn(jnp.concatenate([u[i].astype(BF16), vb[d][:, ps(p)]], axis=0),
                   jnp.concatenate([bl[d][:, ps(p)], kl[d][:, ps(p)]], axis=0)) for i, (d, p) in enumerate(es)]
    for i, (d, p) in enumerate(es):
        bb, dr = streams[d]
        (of_ref, ob_ref)[dr][bb, :, ps(p)] = zs[i][L:] + o1[i] + o2[i]
        s_scr[d, p] = s0[i] * g_tot[d][:, ps(p)] + jnp.where(same_head, upd[i], 0.0)


def _rw_scan(r, lwf, lwb, k, v, kk, a, n_ctx):
    B, S, W = r.shape
    nc, fwd, bwd = _scan_index_maps(S, n_ctx)
    nb = _scan_batch(B, RW_SCAN_BATCH)
    spec = lambda cidx: pl.BlockSpec((nb, CHUNK, W), lambda b, s: (b, cidx(s), 0))
    return pl.pallas_call(
        _rw_scan_kernel,
        grid=(B // nb, nc),
        in_specs=[spec(fwd)] * 6 + [spec(bwd)] * 6,
        out_specs=[spec(fwd), spec(bwd)],
        out_shape=[jax.ShapeDtypeStruct((B, S, W), F32)] * 2,
        scratch_shapes=[pltpu.VMEM((2 * nb, RW_HEADS // 2, 2 * RW_DH, 2 * RW_DH), F32)],
        compiler_params=_cp("parallel", "arbitrary"),
        name="rwkv_scan",
    )(r, lwf, k, v, kk, a, r, lwb, k, v, kk, a)


def _even_out_kernel(x_ref, ogf, ogb, gate_ref, orf, orb, bonus_ref, grw_ref, glag, lng, lnb, ones128, ones64,
                     w_ref, gl, gc, o_ref, *, tm, n_ctx):
    og = ogf[0] + ogb[0]
    ms = _split_dot_r(og * og, ones128[...], 1) * (1.0 / GLA_DV)
    gt = gate_ref[0]
    y_gla = og * lax.rsqrt(ms + 1e-6) * glag[...] * (gt * jax.nn.sigmoid(gt))
    of = orf[0] + orb[0]
    o64 = ones64[...]
    mean = _split_dot_r(of, o64, 2) * (1.0 / RW_DH)
    cen = of - mean
    var = _split_dot_r(cen * cen, o64, 1) * (1.0 / RW_DH)
    y_rw = (cen * lax.rsqrt(var + RW_GN_EPS) * lng[...] + lnb[...] + bonus_ref[0]) * grw_ref[0]
    y = jnp.concatenate([y_gla, y_rw], axis=1).astype(BF16)
    rows = pl.program_id(1) * tm + lax.broadcasted_iota(jnp.int32, (tm, 1), 0)
    gate = jnp.where(rows < n_ctx, gc[0], gl[0])
    o_ref[0] = x_ref[0] + gate * jnp.dot(y, w_ref[...], preferred_element_type=F32)


def _even_out(xs, og_f, og_b, z_gla, or_f, or_b, bonus, g_rw, p, mod, n_ctx, *, tm):
    B, S, D = xs.shape
    W = RW_W
    row = lambda width: pl.BlockSpec((1, tm, width), lambda b, j: (b, j, 0))
    full = lambda shape: pl.BlockSpec(shape, lambda b, j: (0,) * len(shape))
    return pl.pallas_call(
        functools.partial(_even_out_kernel, tm=tm, n_ctx=n_ctx),
        grid=(B, S // tm),
        in_specs=[row(D), row(GLA_V), row(GLA_V),
                  pl.BlockSpec((1, tm, GLA_V), lambda b, j: (b, j, (2 * GLA_QK + GLA_V) // GLA_V)),
                  row(W), row(W), row(W), row(W),
                  full((1, GLA_V)), full((1, W)), full((1, W)), full((GLA_V, GLA_V)), full((W, W)),
                  full((GLA_V + W, D)), *_mod_specs(B, D, 2, 2)],
        out_specs=row(D),
        out_shape=jax.ShapeDtypeStruct((B, S, D), F32),
        compiler_params=_cp("parallel", "parallel"),
        name="even_out",
    )(xs, og_f, og_b, z_gla, or_f, or_b, bonus, g_rw, p["gla_g"], p["ln_g"], p["ln_b"], p["ones128"], p["ones64"],
      p["w_out"], mod, mod)


def _odd_in_kernel(h_ref, w_ref, cos_ref, sin_ref, qg_ref, kg_ref, ones64, qc_o, kc_o, vc_o, qd_o, kd_o, vd_o):
    h = h_ref[0]
    cos, sin = cos_ref[...], sin_ref[...]
    o64 = ones64[...]
    nq, nk, nd = C_HEADS * HEAD_DIM, C_KV * HEAD_DIM, NA_HEADS * HEAD_DIM

    def piece(lo, width):
        return jnp.dot(h, w_ref[:, lo:lo + width], preferred_element_type=F32)

    def norm_rope(x, g, width):
        ms = _split_dot_r(x * x, o64[:width, :width], 1) * (1.0 / HEAD_DIM)
        y = x * lax.rsqrt(ms + 1e-6) * g
        lane = lax.broadcasted_iota(jnp.int32, (1, width), 1)
        first = (lane % (HEAD_DIM // 2)) < (HEAD_DIM // 4)
        swapped = jnp.where(first, pltpu.roll(y, width - HEAD_DIM // 4, 1), pltpu.roll(y, HEAD_DIM // 4, 1))
        return y * cos[:, :width] + swapped * sin[:, :width]

    qc_o[0] = (norm_rope(piece(0, nq), qg_ref[...], nq) * HEAD_DIM ** -0.5).astype(BF16)
    kc_o[0] = norm_rope(piece(nq, nk), kg_ref[:, :nk], nk).astype(BF16)
    vc_o[0] = piece(nq + nk, nk).astype(BF16)
    base = nq + 2 * nk
    qd_o[0] = (piece(base, nd) * HEAD_DIM ** -0.5).astype(BF16)
    kd_o[0] = piece(base + nd, nd).astype(BF16)
    vd_o[0] = piece(base + 2 * nd, nd).astype(BF16)


def _odd_in(h, w, cos_t, sin_t, qg, kg, ones64, *, tm):
    B, S, D = h.shape
    nq, nk, nd = C_HEADS * HEAD_DIM, C_KV * HEAD_DIM, NA_HEADS * HEAD_DIM
    full = lambda shape: pl.BlockSpec(shape, lambda b, j: (0,) * len(shape))
    row = lambda width: pl.BlockSpec((1, tm, width), lambda b, j: (b, j, 0))
    widths = (nq, nk, nk, nd, nd, nd)
    return pl.pallas_call(
        _odd_in_kernel,
        grid=(B, S // tm),
        in_specs=[row(D), _resident(w.shape),
                  pl.BlockSpec((tm, nq), lambda b, j: (j, 0)), pl.BlockSpec((tm, nq), lambda b, j: (j, 0)),
                  full((1, nq)), full((1, nq)), full((nq, nq))],
        out_specs=[row(wd) for wd in widths],
        out_shape=[jax.ShapeDtypeStruct((B, S, wd), BF16) for wd in widths],
        compiler_params=_cp("parallel", "parallel"),
        name="odd_in",
    )(h, w, cos_t, sin_t, qg, kg, ones64)


def _pair_attn_kernel(q_ref, k_ref, v_ref, o_ref, *, shared_kv):
    q = q_ref[0]
    tq = q.shape[0]
    ngroups = q.shape[1] // LANE
    lane = lax.broadcasted_iota(jnp.int32, (1, LANE), 1)
    masks = (lane < HEAD_DIM, lane >= HEAD_DIM)
    zero = jnp.zeros((), q.dtype)
    if shared_kv:
        kp, vp = k_ref[0], v_ref[0]
        es = [(g, hm) for g in range(ngroups) for hm in range(2)]
        ids = range(len(es))
        lhs = [jnp.where(masks[hm], q[:, g * LANE:(g + 1) * LANE], zero) for g, hm in es]
        s = [_dot_nt(lhs[i], kp) for i in ids]
        p = [jnp.exp(s[i] - jnp.max(s[i], axis=-1, keepdims=True)) for i in ids]
        den = [jnp.sum(p[i], axis=-1, keepdims=True) for i in ids]
        o = [_dot(p[i], vp) / den[i] for i in ids]
        for g in range(ngroups):
            o_ref[0, :, g * LANE:(g + 1) * LANE] = jnp.where(masks[0], o[2 * g], o[2 * g + 1]).astype(o_ref.dtype)
    else:
        for g in range(ngroups):
            qp = q[:, g * LANE:(g + 1) * LANE]
            kp = k_ref[0, :, g * LANE:(g + 1) * LANE]
            vp = v_ref[0, :, g * LANE:(g + 1) * LANE]
            lhs = jnp.concatenate([jnp.where(masks[0], qp, zero), jnp.where(masks[1], qp, zero)], axis=0)
            s = _dot_nt(lhs, kp)
            p = jnp.exp(s - jnp.max(s, axis=-1, keepdims=True))
            o = _dot(p, vp) / jnp.sum(p, axis=-1, keepdims=True)
            o_ref[0, :, g * LANE:(g + 1) * LANE] = jnp.where(masks[0], o[:tq], o[tq:]).astype(o_ref.dtype)


def _pair_attention(q, k, v, *, q_row0, n_q, n_k, tq, shared_kv, name):
    B, S, QW = q.shape
    KW = k.shape[2]
    return pl.pallas_call(
        functools.partial(_pair_attn_kernel, shared_kv=shared_kv),
        grid=(B, n_q // tq),
        in_specs=[pl.BlockSpec((1, tq, QW), lambda b, i: (b, q_row0 // tq + i, 0)),
                  pl.BlockSpec((1, n_k, KW), lambda b, i: (b, 0, 0)),
                  pl.BlockSpec((1, n_k, KW), lambda b, i: (b, 0, 0))],
        out_specs=pl.BlockSpec((1, tq, QW), lambda b, i: (b, i, 0)),
        out_shape=jax.ShapeDtypeStruct((B, n_q, QW), BF16),
        compiler_params=_cp("parallel", "parallel"),
        name=name,
    )(q, k, v)


NA_ROWS_PER_STEP = 2


def _na_kernel(q_ref, k_ref, v_ref, *rest, n_ctx, n_rows, kh, rps):
    bias_refs, o_ref = rest[:rps], rest[rps]
    nkeys = kh * GRID_W
    lane = lax.broadcasted_iota(jnp.int32, (1, LANE), 1)
    m0 = lane < HEAD_DIM
    zero = jnp.zeros((), q_ref.dtype)
    starts = []
    for j in range(rps):
        r = pl.program_id(1) * rps + j
        rs = jnp.clip(r - kh // 2, 0, n_rows - kh)
        starts.append(pl.multiple_of(n_ctx + rs * GRID_W, GRID_W))
    es = [(j, g) for j in range(rps) for g in range(NA_HEADS // 2)]
    ids = range(len(es))
    gs = lambda g: slice(g * LANE, (g + 1) * LANE)
    qrow = lambda j, g: q_ref[0, j * GRID_W:(j + 1) * GRID_W, gs(g)]
    lhs = [jnp.concatenate([jnp.where(m0, qrow(j, g), zero), jnp.where(m0, zero, qrow(j, g))], axis=0)
           for j, g in es]
    s_nb = [_dot_nt(lhs[i], k_ref[0, pl.ds(starts[j], nkeys), gs(g)]) for i, (j, g) in enumerate(es)]
    s_cx = [_dot_nt(lhs[i], k_ref[0, 0:n_ctx, gs(g)]) for i, (j, g) in enumerate(es)]
    s_nb = [s_nb[i] + jnp.concatenate([bias_refs[j][0, 2 * g], bias_refs[j][0, 2 * g + 1]], axis=0)
            for i, (j, g) in enumerate(es)]
    m = [jnp.maximum(jnp.max(s_nb[i], axis=-1, keepdims=True), jnp.max(s_cx[i], axis=-1, keepdims=True))
         for i in ids]
    p_nb = [jnp.exp(s_nb[i] - m[i]) for i in ids]
    p_cx = [jnp.exp(s_cx[i] - m[i]) for i in ids]
    den = [jnp.sum(p_nb[i], axis=-1, keepdims=True) + jnp.sum(p_cx[i], axis=-1, keepdims=True) for i in ids]
    o_nb = [_dot(p_nb[i], v_ref[0, pl.ds(starts[j], nkeys), gs(g)]) for i, (j, g) in enumerate(es)]
    o_cx = [_dot(p_cx[i], v_ref[0, 0:n_ctx, gs(g)]) for i, (j, g) in enumerate(es)]
    for i, (j, g) in enumerate(es):
        o = (o_nb[i] + o_cx[i]) / den[i]
        o_ref[0, j * GRID_W:(j + 1) * GRID_W, gs(g)] = jnp.where(m0, o[:GRID_W], o[GRID_W:]).astype(o_ref.dtype)


def _na_attention(qd, kd, vd, bias_tab, n_ctx):
    B, S, W = qd.shape
    T = S - n_ctx
    n_rows = T // GRID_W
    kh = min(NA_KH, n_rows)
    nkeys = kh * GRID_W
    rps = NA_ROWS_PER_STEP if n_rows % NA_ROWS_PER_STEP == 0 and n_ctx % (NA_ROWS_PER_STEP * GRID_W) == 0 else 1
    tq = rps * GRID_W

    def bias_spec(j):
        def idx(b, i):
            r = i * rps + j
            return (jnp.clip(r - kh // 2, 0, n_rows - kh) - r + kh - 1, 0, 0, 0)
        return pl.BlockSpec((1, NA_HEADS, GRID_W, nkeys), idx)

    return pl.pallas_call(
        functools.partial(_na_kernel, n_ctx=n_ctx, n_rows=n_rows, kh=kh, rps=rps),
        grid=(B, n_rows // rps),
        in_specs=[pl.BlockSpec((1, tq, W), lambda b, i: (b, n_ctx // tq + i, 0)),
                  pl.BlockSpec((1, S, W), lambda b, i: (b, 0, 0)),
                  pl.BlockSpec((1, S, W), lambda b, i: (b, 0, 0)),
                  *[bias_spec(j) for j in range(rps)]],
        out_specs=pl.BlockSpec((1, tq, W), lambda b, i: (b, i, 0)),
        out_shape=jax.ShapeDtypeStruct((B, T, W), BF16),
        compiler_params=_cp("parallel", "arbitrary"),
        name="na_attention",
    )(qd, kd, vd, *([bias_tab] * rps))


def _na_bias_table(rpb, n_rows):
    kh = min(NA_KH, n_rows)
    kw = min(NA_KW, GRID_W)
    col = np.arange(GRID_W)
    start = np.clip(col - kw // 2, 0, GRID_W - kw)
    in_win = (col[None, :] >= start[:, None]) & (col[None, :] < start[:, None] + kw)
    dc = np.clip(col[None, :] - col[:, None], -(NA_KW - 1), NA_KW - 1) + NA_KW - 1
    bias_cols = rpb[:, :, dc]
    slabs = []
    for st in range(kh):
        sl = bias_cols[:, st + NA_KH - kh:st + NA_KH - kh + kh]
        sl = jnp.where(in_win[None, None], sl, NEG).transpose(0, 2, 1, 3)
        slabs.append(sl.reshape(NA_HEADS, GRID_W, kh * GRID_W))
    return jnp.stack(slabs).astype(F32)


def _proj_res_kernel(x_ref, y_ref, w_ref, gl, gc, o_ref, *, tm, n_ctx):
    rows = pl.program_id(1) * tm + lax.broadcasted_iota(jnp.int32, (tm, 1), 0)
    gate = jnp.where(rows < n_ctx, gc[0], gl[0])
    o_ref[0] = x_ref[0] + gate * jnp.dot(y_ref[0], w_ref[...], preferred_element_type=F32)


def _proj_res(xs, y, w, mod, n_ctx, *, tm):
    B, S, D = xs.shape
    K = y.shape[2]
    row = lambda width: pl.BlockSpec((1, tm, width), lambda b, j: (b, j, 0))
    return pl.pallas_call(
        functools.partial(_proj_res_kernel, tm=tm, n_ctx=n_ctx),
        grid=(B, S // tm),
        in_specs=[row(D), row(K), pl.BlockSpec((K, D), lambda b, j: (0, 0)), *_mod_specs(B, D, 2, 2)],
        out_specs=row(D),
        out_shape=jax.ShapeDtypeStruct((B, S, D), F32),
        compiler_params=_cp("parallel", "parallel"),
        name="proj_res",
    )(xs, y, w, mod, mod)


def _final_norm_kernel(x_ref, g_ref, o_ref):
    x = x_ref[0]
    o_ref[0] = x * lax.rsqrt(jnp.mean(x * x, axis=-1, keepdims=True) + 1e-6) * g_ref[...]


def _final_norm(xs, g, n_ctx, *, tm):
    B, S, D = xs.shape
    T = S - n_ctx
    return pl.pallas_call(
        _final_norm_kernel,
        grid=(B, T // tm),
        in_specs=[pl.BlockSpec((1, tm, D), lambda b, j: (b, n_ctx // tm + j, 0)),
                  pl.BlockSpec((1, D), lambda b, j: (0, 0))],
        out_specs=pl.BlockSpec((1, tm, D), lambda b, j: (b, j, 0)),
        out_shape=jax.ShapeDtypeStruct((B, T, D), F32),
        compiler_params=_cp("parallel", "parallel"),
        name="final_norm",
    )(xs, g.reshape(1, D))


def _block_ones(width, block):
    idx = np.arange(width) // block
    return jnp.asarray(idx[:, None] == idx[None, :], BF16)


def _pad_cols(w, width):
    return jnp.pad(w, ((0, 0), (0, width - w.shape[1])))


def _pad_rows(w, height):
    return jnp.pad(w, ((0, height - w.shape[0]), (0, 0)))


def _even_params(j, ev_w_in, ev_w_out, gla_a_up, gla_a_bias, gla_norm_g, rw_mu, rw_w0, rw_w_up, rw_a0, rw_a_up,
                 rw_g_up, rw_k_k, rw_k_a, rw_r_k, rw_ln_g, rw_ln_b):
    w_in = ev_w_in[j]
    D = w_in.shape[0]
    gq = 2 * GLA_QK + 2 * GLA_V
    w_gla = jnp.concatenate([w_in[:, :gq], _pad_cols(w_in[:, gq:gq + GLA_RANK], LANE),
                             _pad_cols(w_in[:, gq + GLA_RANK:gq + 2 * GLA_RANK], LANE)], axis=1)
    o = gq + 2 * GLA_RANK
    sizes = (3 * RW_W, RW_DECAY_RANK, RW_DECAY_RANK, RW_A_RANK, RW_G_RANK)
    offs = np.cumsum((0,) + sizes)
    pieces = [w_in[:, o:o + 3 * RW_W]] + [_pad_cols(w_in[:, o + offs[i]:o + offs[i + 1]], LANE) for i in range(1, 5)]
    w_rw = jnp.concatenate(pieces, axis=1)
    mu = rw_mu[j]
    mu_pieces = [mu[:, :3 * RW_W]] + [_pad_cols(mu[:, offs[i]:offs[i + 1]], LANE) for i in range(1, 5)]
    return {
        "w_gla": w_gla.astype(BF16), "w_rw": w_rw.astype(BF16), "w_out": ev_w_out[j].astype(BF16),
        "gla_aup": jnp.stack([_pad_rows(gla_a_up[j, d], LANE) for d in range(2)]).astype(BF16),
        "gla_bias": gla_a_bias[j], "gla_g": jnp.tile(gla_norm_g[j], GLA_HEADS).reshape(1, GLA_V),
        "mu": jnp.concatenate(mu_pieces, axis=1), "w0": rw_w0[j],
        "wup": jnp.stack([_pad_rows(rw_w_up[j, d], LANE) for d in range(2)]).astype(BF16),
        "a0": rw_a0[j].reshape(1, RW_W), "aup": _pad_rows(rw_a_up[j], LANE).astype(BF16),
        "gup": _pad_rows(rw_g_up[j], LANE).astype(BF16),
        "k_k": rw_k_k[j].reshape(1, RW_W), "k_a": rw_k_a[j].reshape(1, RW_W), "r_k": rw_r_k[j].reshape(1, RW_W),
        "ln_g": rw_ln_g[j].reshape(1, RW_W), "ln_b": rw_ln_b[j].reshape(1, RW_W),
        "ones64": _block_ones(RW_W, RW_DH), "ones128": _block_ones(GLA_V, GLA_DV),
    }


def _gqa_head_order():
    per_kv = C_HEADS // C_KV
    order = []
    for jj in range(per_kv):
        for kv in range(C_KV):
            h = kv * per_kv + jj
            order.extend(range(h * HEAD_DIM, (h + 1) * HEAD_DIM))
    return np.asarray(order)


def _rope_tables(n_ctx, T, width):
    t = jnp.arange(T)
    pos = jnp.stack([t // GRID_W, t % GRID_W], axis=-1).astype(F32)
    half = HEAD_DIM // 2
    inv = ROPE_THETA ** (-jnp.arange(0, half, 2, dtype=F32) / half)
    ang = pos[:, :, None] * inv
    cos, sin = jnp.cos(ang), jnp.sin(ang)
    cos_h = jnp.concatenate([cos, cos], axis=-1).reshape(T, HEAD_DIM)
    sin_h = jnp.concatenate([-sin, sin], axis=-1).reshape(T, HEAD_DIM)
    cos_h = jnp.concatenate([jnp.ones((n_ctx, HEAD_DIM), F32), cos_h], axis=0)
    sin_h = jnp.concatenate([jnp.zeros((n_ctx, HEAD_DIM), F32), sin_h], axis=0)
    reps = width // HEAD_DIM
    return jnp.tile(cos_h, (1, reps)), jnp.tile(sin_h, (1, reps))


def _even_layer(xs, h, mod, n_ctx, p):
    z_gla, r, lwf, lwb, k, v, kk, a, g_rw, bonus = _even_in(h, p, n_ctx)
    og_f, og_b = _gla_scan(z_gla, p["gla_aup"], p["gla_bias"], n_ctx)
    or_f, or_b = _rw_scan(r, lwf, lwb, k, v, kk, a, n_ctx)
    return _even_out(xs, og_f, og_b, z_gla, or_f, or_b, bonus, g_rw, p, mod, n_ctx,
                     tm=_row_tile(xs.shape[1], ROW_CAP_ELEMENTWISE))


def _odd_layer(xs, h, mod, n_ctx, p, need_ctx):
    B, S, D = xs.shape
    T = S - n_ctx
    qc, kc, vc, qd, kd, vd = _odd_in(h, p["w_in"], p["cos"], p["sin"], p["qg"], p["kg"], p["ones64"],
                                     tm=_row_tile(S, ROW_CAP_FFN))
    y_gqa = _pair_attention(qc, kc, vc, q_row0=n_ctx, n_q=T, n_k=S, tq=256, shared_kv=True, name="gqa")
    y_na = _na_attention(qd, kd, vd, p["bias_tab"], n_ctx)
    parts_g, parts_n = [y_gqa], [y_na]
    if need_ctx:
        parts_g.insert(0, _pair_attention(qc, kc, vc, q_row0=0, n_q=n_ctx, n_k=n_ctx, tq=n_ctx, shared_kv=True,
                                          name="gqa_ctx"))
        parts_n.insert(0, _pair_attention(qd, kd, vd, q_row0=0, n_q=n_ctx, n_k=n_ctx, tq=n_ctx, shared_kv=False,
                                          name="na_ctx"))
    else:
        zeros = jnp.zeros((B, n_ctx, y_gqa.shape[2]), BF16)
        parts_g.insert(0, zeros)
        parts_n.insert(0, zeros)
    y = jnp.concatenate([jnp.concatenate(parts_g, axis=1), jnp.concatenate(parts_n, axis=1)], axis=2)
    return _proj_res(xs, y, p["w_out"], mod, n_ctx, tm=_row_tile(S, ROW_CAP_MATMUL))


def kernel(x, c, ctx, c_ctx, w_mod, b_mod, norm1_g, norm2_g, ffn_w13, ffn_w2, ev_w_in, ev_w_out, gla_a_up,
           gla_a_bias, gla_norm_g, rw_mu, rw_w0, rw_w_up, rw_a0, rw_a_up, rw_g_up, rw_k_k, rw_k_a, rw_r_k, rw_ln_g,
           rw_ln_b, od_w_in, od_w_out, cq_norm_g, ck_norm_g, na_rpb, final_g):
    B, T, D = x.shape
    n_ctx = ctx.shape[1]
    S = n_ctx + T
    xs = jnp.concatenate([ctx, x], axis=1)
    mods = _mod_vectors(c, c_ctx, w_mod, b_mod)
    nq, nk = C_HEADS * HEAD_DIM, C_KV * HEAD_DIM
    order = _gqa_head_order()
    cos_t, sin_t = _rope_tables(n_ctx, T, nq)
    ones64 = _block_ones(nq, HEAD_DIM)
    h = _norm_mod(xs, norm1_g[0], mods[0], n_ctx, tm=_row_tile(S, ROW_CAP_MATMUL))
    for i in range(DEPTH):
        j = i // 2
        mod = mods[i]
        if i % 2 == 0:
            p = _even_params(j, ev_w_in, ev_w_out, gla_a_up, gla_a_bias, gla_norm_g, rw_mu, rw_w0, rw_w_up, rw_a0,
                             rw_a_up, rw_g_up, rw_k_k, rw_k_a, rw_r_k, rw_ln_g, rw_ln_b)
            xs = _even_layer(xs, h, mod, n_ctx, p)
        else:
            w_in = od_w_in[j]
            w_out = od_w_out[j]
            p = {
                "w_in": jnp.concatenate([w_in[:, :nq][:, order], w_in[:, nq:]], axis=1).astype(BF16),
                "w_out": jnp.concatenate([w_out[:nq][order], w_out[nq:]], axis=0).astype(BF16),
                "cos": cos_t, "sin": sin_t, "ones64": ones64,
                "qg": jnp.tile(cq_norm_g[j], C_HEADS).reshape(1, nq),
                "kg": jnp.tile(ck_norm_g[j], C_HEADS).reshape(1, nq),
                "bias_tab": _na_bias_table(na_rpb[j], T // GRID_W),
            }
            xs = _odd_layer(xs, h, mod, n_ctx, p, need_ctx=i < DEPTH - 1)
        next_norm = (norm1_g[i + 1], mods[i + 1]) if i + 1 < DEPTH else None
        xs, h = _ffn(xs, norm2_g[i], mod, ffn_w13[i].astype(BF16), ffn_w2[i].astype(BF16), n_ctx,
                     tm=_row_tile(S, ROW_CAP_FFN), tf=FFN_COL_TILE, next_norm=next_norm)
    return _final_norm(xs, final_g, n_ctx, tm=_row_tile(n_ctx, ROW_CAP_ELEMENTWISE))
```

```python
import functools
import itertools
import math

import jax
import jax.numpy as jnp
import numpy as np
from jax import lax
from jax.experimental import pallas as pl
from jax.experimental.pallas import tpu as pltpu

F32 = jnp.float32
BF16 = jnp.bfloat16

DEPTH = 4
GRID_W = 64
CHUNK = 64
LANE = 128

GLA_HEADS, GLA_DK, GLA_DV, GLA_RANK, GLA_TAU = 4, 64, 128, 16, 16.0
GLA_QK, GLA_V = GLA_HEADS * GLA_DK, GLA_HEADS * GLA_DV
GLA_Z = 2 * GLA_QK + 2 * GLA_V + 2 * LANE

RW_HEADS, RW_DH, RW_W = 8, 64, 512
RW_DECAY_RANK, RW_A_RANK, RW_G_RANK = 32, 32, 96
RW_GN_EPS = 64e-5
RW_Z = 3 * RW_W + 4 * LANE

HEAD_DIM, C_HEADS, C_KV, NA_HEADS, NA_KH, NA_KW = 64, 8, 2, 8, 8, 16
ROPE_THETA = 10000.0
NEG = -1e30

VMEM_LIMIT = 56 * 1024 * 1024
ROW_CAP_MATMUL = 1152
ROW_CAP_FFN = 576
ROW_CAP_ELEMENTWISE = 256
FFN_COL_TILE = 256


def _row_tile(n_rows, cap):
    return max(d for d in range(8, cap + 1, 8) if n_rows % d == 0)


def _cp(*sem):
    return pltpu.CompilerParams(dimension_semantics=sem, vmem_limit_bytes=VMEM_LIMIT)


def _dot(a, b):
    return jnp.dot(a.astype(BF16), b.astype(BF16), preferred_element_type=F32)


def _dot_nt(a, b):
    return lax.dot_general(a.astype(BF16), b.astype(BF16), (((1,), (1,)), ((), ())), preferred_element_type=F32)


def _dot_tn(a, b):
    return lax.dot_general(a.astype(BF16), b.astype(BF16), (((0,), (0,)), ((), ())), preferred_element_type=F32)


def _split_dot(m, x, parts):
    acc = None
    rem = x
    for _ in range(parts):
        piece = rem.astype(BF16)
        rem = rem - piece.astype(F32)
        t = jnp.dot(m, piece, preferred_element_type=F32)
        acc = t if acc is None else acc + t
    return acc


def _split_dot_r(x, m, parts):
    acc = None
    rem = x
    for _ in range(parts):
        piece = rem.astype(BF16)
        rem = rem - piece.astype(F32)
        t = jnp.dot(piece, m, preferred_element_type=F32)
        acc = t if acc is None else acc + t
    return acc


def _rms_mod(x, g, sc_l, sh_l, sc_c, sh_c, row0, n_ctx):
    y = x * lax.rsqrt(jnp.mean(x * x, axis=-1, keepdims=True) + 1e-6) * g
    rows = row0 + lax.broadcasted_iota(jnp.int32, (x.shape[0], 1), 0)
    is_ctx = rows < n_ctx
    sc = jnp.where(is_ctx, sc_c, sc_l)
    sh = jnp.where(is_ctx, sh_c, sh_l)
    return y * (1.0 + sc) + sh


def _tri_masks(n, reverse):
    ri = lax.broadcasted_iota(jnp.int32, (n, n), 0)
    ci = lax.broadcasted_iota(jnp.int32, (n, n), 1)
    if reverse:
        return ci >= ri, ci > ri
    return ci <= ri, ci < ri


INV_BASE = 16


def _pair_ids(n):
    ti = lax.broadcasted_iota(jnp.int32, (n, 2 * n), 0)
    ji = lax.broadcasted_iota(jnp.int32, (n, 2 * n), 1) % n
    return ti, ji


def _pair_blockdiag(x, lane_is_head0):
    zero = jnp.zeros((), x.dtype)
    return jnp.concatenate([jnp.where(lane_is_head0, x, zero), jnp.where(lane_is_head0, zero, x)], axis=0)


def _unit_tri_inverse(a, ids_tj, lane_is_head0):
    ti, ji = ids_tj
    n = a[0].shape[0]
    ids = range(len(a))
    bd = lambda x: _pair_blockdiag(x.astype(BF16), lane_is_head0)
    same = lambda bs: (ti // bs) == (ji // bs)
    base = same(INV_BASE)
    eye = (ti == ji).astype(F32)
    p = [jnp.where(base, x, 0.0) for x in a]
    t = [eye + x for x in p]
    m = 1
    while 2 * m < INV_BASE:
        p = [_dot(x, bd(x)) for x in p]
        yield
        tp = [_dot(t[i], bd(p[i])) for i in ids]
        t = [t[i] + tp[i] for i in ids]
        yield
        m *= 2
    bs = INV_BASE
    while bs < n:
        mask = same(2 * bs) & jnp.logical_not(same(bs))
        ta = [_dot(t[i], bd(jnp.where(mask, a[i], 0.0))) for i in ids]
        yield
        tat = [_dot(ta[i], bd(t[i])) for i in ids]
        t = [t[i] + tat[i] for i in ids]
        yield
        bs *= 2
    return t


def _mod_kernel(s_ref, w_ref, b_ref, o_ref):
    s = s_ref[...]
    s = s * jax.nn.sigmoid(s)
    o_ref[0] = _dot(s, w_ref[0]) + b_ref[0]


def _mod_vectors(c, c_ctx, w_mod, b_mod):
    B, D = c.shape
    bp = -(-(B + 1) // 8) * 8
    s_in = jnp.zeros((bp, D), F32).at[:B].set(c).at[B].set(c_ctx)
    n6 = w_mod.shape[-1]
    tn = 1536
    out = pl.pallas_call(
        _mod_kernel,
        grid=(DEPTH, n6 // tn),
        in_specs=[pl.BlockSpec((bp, D), lambda l, n: (0, 0)),
                  pl.BlockSpec((1, D, tn), lambda l, n: (l, 0, n)),
                  pl.BlockSpec((1, 1, tn), lambda l, n: (l, 0, n))],
        out_specs=pl.BlockSpec((1, bp, tn), lambda l, n: (l, 0, n)),
        out_shape=jax.ShapeDtypeStruct((DEPTH, bp, n6), F32),
        compiler_params=_cp("parallel", "parallel"),
        name="mod_vectors",
    )(s_in, w_mod, b_mod.reshape(DEPTH, 1, n6))
    return out.reshape(DEPTH, bp, 1, n6)


def _mod_specs(B, D, idx, grid_rank):
    if grid_rank == 2:
        return [pl.BlockSpec((1, 1, D), lambda b, j: (b, 0, idx)),
                pl.BlockSpec((1, 1, D), lambda b, j: (B, 0, idx))]
    return [pl.BlockSpec((1, 1, D), lambda b, j, n: (b, 0, idx)),
            pl.BlockSpec((1, 1, D), lambda b, j, n: (B, 0, idx))]


def _norm_mod_kernel(x_ref, g_ref, shl, shc, scl, scc, o_ref, *, tm, n_ctx):
    h = _rms_mod(x_ref[0], g_ref[...], scl[0], shl[0], scc[0], shc[0], pl.program_id(1) * tm, n_ctx)
    o_ref[0] = h.astype(BF16)


def _norm_mod(xs, g, mod, n_ctx, *, tm):
    B, S, D = xs.shape
    row = pl.BlockSpec((1, tm, D), lambda b, j: (b, j, 0))
    return pl.pallas_call(
        functools.partial(_norm_mod_kernel, tm=tm, n_ctx=n_ctx),
        grid=(B, S // tm),
        in_specs=[row, pl.BlockSpec((1, D), lambda b, j: (0, 0)), *_mod_specs(B, D, 0, 2), *_mod_specs(B, D, 1, 2)],
        out_specs=row,
        out_shape=jax.ShapeDtypeStruct((B, S, D), BF16),
        compiler_params=_cp("parallel", "parallel"),
        name="norm_mod",
    )(xs, g.reshape(1, D), mod, mod, mod, mod)


def _ffn_kernel(x_ref, g_ref, shl, shc, scl, scc, gl, gc, w13_ref, w2_ref, *rest, tm, tf, n_ctx, emit_next):
    if emit_next:
        gn_ref, nshl, nshc, nscl, nscc, o_ref, hn_ref = rest
    else:
        (o_ref,) = rest
    x = x_ref[0]
    row0 = pl.program_id(1) * tm
    h = _rms_mod(x, g_ref[...], scl[0], shl[0], scc[0], shc[0], row0, n_ctx).astype(BF16)
    n_hidden = w2_ref.shape[0]
    acc = None
    for f in range(n_hidden // tf):
        a = jnp.dot(h, w13_ref[:, f * tf:(f + 1) * tf], preferred_element_type=F32)
        b = jnp.dot(h, w13_ref[:, n_hidden + f * tf:n_hidden + (f + 1) * tf], preferred_element_type=F32)
        u = (a * jax.nn.sigmoid(a) * b).astype(BF16)
        t = jnp.dot(u, w2_ref[f * tf:(f + 1) * tf, :], preferred_element_type=F32)
        acc = t if acc is None else acc + t
    rows = row0 + lax.broadcasted_iota(jnp.int32, (tm, 1), 0)
    gate = jnp.where(rows < n_ctx, gc[0], gl[0])
    y = x + gate * acc
    o_ref[0] = y
    if emit_next:
        hn_ref[0] = _rms_mod(y, gn_ref[...], nscl[0], nshl[0], nscc[0], nshc[0], row0, n_ctx).astype(BF16)


def _resident(shape):
    return pl.BlockSpec(shape, lambda b, j: (0,) * len(shape), pipeline_mode=pl.Buffered(1))


def _ffn(xs, g, mod, w13, w2, n_ctx, *, tm, tf, next_norm=None):
    B, S, D = xs.shape
    row = pl.BlockSpec((1, tm, D), lambda b, j: (b, j, 0))
    vec = pl.BlockSpec((1, D), lambda b, j: (0, 0))
    in_specs = [row, vec, *_mod_specs(B, D, 3, 2), *_mod_specs(B, D, 4, 2), *_mod_specs(B, D, 5, 2),
                _resident(w13.shape), _resident(w2.shape)]
    args = [xs, g.reshape(1, D), mod, mod, mod, mod, mod, mod, w13, w2]
    out_specs, out_shape = [row], [jax.ShapeDtypeStruct((B, S, D), F32)]
    if next_norm is not None:
        g_next, mod_next = next_norm
        in_specs += [vec, *_mod_specs(B, D, 0, 2), *_mod_specs(B, D, 1, 2)]
        args += [g_next.reshape(1, D), mod_next, mod_next, mod_next, mod_next]
        out_specs.append(row)
        out_shape.append(jax.ShapeDtypeStruct((B, S, D), BF16))
    out = pl.pallas_call(
        functools.partial(_ffn_kernel, tm=tm, tf=tf, n_ctx=n_ctx, emit_next=next_norm is not None),
        grid=(B, S // tm),
        in_specs=in_specs,
        out_specs=out_specs,
        out_shape=out_shape,
        compiler_params=_cp("parallel", "parallel"),
        name="ffn",
    )(*args)
    return (out[0], out[1]) if next_norm is not None else (out[0], None)


def _chunk_index(s, nc_ctx, nc, reverse):
    if not reverse:
        return s
    return jnp.where(s < nc_ctx, nc_ctx - 1 - s, nc + nc_ctx - 1 - s)


def _gla_stages(qkv_f, ad_f, qkv_b, ad_b, aup_ref, bias_ref, of_ref, ob_ref, st_scr):
    L = CHUNK
    nb = qkv_f.shape[0]
    streams = [(bb, d) for bb in range(nb) for d in range(2)]
    qb, kb, kt, gt, v, incl = [], [], [], [], [], []
    for bb, d in streams:
        qkv_ref, ad_ref = ((qkv_f, ad_f), (qkv_b, ad_b))[d]
        m, _ = _tri_masks(L, d == 1)
        ad = ad_ref[bb][:, d * LANE:(d + 1) * LANE]
        y = _dot(ad, aup_ref[d]) + bias_ref[d:d + 1, :]
        la = (jnp.minimum(y, 0.0) - jnp.log(1.0 + jnp.exp(-jnp.abs(y)))) * (1.0 / GLA_TAU)
        cs = _split_dot(m.astype(BF16), la, 3)
        tot = cs[0:1] if d == 1 else cs[L - 1:L]
        qkv = qkv_ref[bb]
        q = qkv[:, 0:GLA_QK] * GLA_DK ** -0.5
        k = qkv[:, GLA_QK:2 * GLA_QK]
        incl.append(m)
        v.append(qkv[:, 2 * GLA_QK:2 * GLA_QK + GLA_V].astype(BF16))
        qb.append((q * jnp.exp(cs)).astype(BF16))
        kb.append((k * jnp.exp(-cs)).astype(BF16))
        kt.append((k * jnp.exp(tot - cs)).astype(BF16))
        gt.append(jnp.exp(tot))
    yield
    es = [(c, h) for c in range(len(streams)) for h in range(GLA_HEADS)]
    ks = lambda h: slice(h * GLA_DK, (h + 1) * GLA_DK)
    vs = lambda h: slice(h * GLA_DV, (h + 1) * GLA_DV)
    st = [st_scr[c, h] for c, h in es]
    att = [jnp.where(incl[c], _dot_nt(qb[c][:, ks(h)], kb[c][:, ks(h)]), 0.0) for c, h in es]
    o_state = [_dot_nt(qb[c][:, ks(h)], st[i]) for i, (c, h) in enumerate(es)]
    yield
    o_att = [_dot(att[i], v[c][:, vs(h)]) for i, (c, h) in enumerate(es)]
    upd = [_dot_tn(v[c][:, vs(h)], kt[c][:, ks(h)]) for c, h in es]
    yield
    for i, (c, h) in enumerate(es):
        bb, d = streams[c]
        (of_ref, ob_ref)[d][bb, :, vs(h)] = o_att[i] + o_state[i]
        st_scr[c, h] = st[i] * gt[c][:, ks(h)] + upd[i]


SCAN_BATCH = 4


def _scan_batch(B, want):
    return want if B % want == 0 else 1


def _scan_index_maps(S, n_ctx):
    nc, nc_ctx = S // CHUNK, n_ctx // CHUNK
    fwd = functools.partial(_chunk_index, nc_ctx=nc_ctx, nc=nc, reverse=False)
    bwd = functools.partial(_chunk_index, nc_ctx=nc_ctx, nc=nc, reverse=True)
    return nc, fwd, bwd


def _even_in_kernel(h_ref, hp_ref, hn_ref, wg_ref, wr_ref, mu_ref, w0_ref, wup_ref, a0_ref, aup_ref, gup_ref,
                    kk_ref, ka_ref, rk_ref, ones_ref, zg_o, r_o, lwf_o, lwb_o, k_o, v_o, kk_o, a_o, g_o, bonus_o,
                    *, tt, halo):
    j = pl.program_id(1)
    last = pl.num_programs(1) - 1
    h = h_ref[0]
    z_ext = jnp.dot(jnp.concatenate([hp_ref[0], h, hn_ref[0]], axis=0), wr_ref[...], preferred_element_type=F32)
    zg_o[0] = jnp.dot(h, wg_ref[...], preferred_element_type=F32)
    z = z_ext[halo:halo + tt]
    prev_row = jnp.where(j <= 1, 0.0, z_ext[halo - 1:halo])
    next_row = jnp.where((j == 0) | (j == last), 0.0, z_ext[halo + tt:halo + tt + 1])
    mu_p, mu_n = mu_ref[0:1, :], mu_ref[1:2, :]
    zs = z + mu_p * (pltpu.roll(z, 1, 0) - z) + mu_n * (pltpu.roll(z, tt - 1, 0) - z)
    r8 = lax.broadcasted_iota(jnp.int32, (8, 1), 0)
    head = zs[:8] + jnp.where(r8 == 0, mu_p * (prev_row - z[tt - 1:tt]), 0.0)
    tail = zs[tt - 8:] + jnp.where(r8 == 7, mu_n * (next_row - z[0:1]), 0.0)
    zs = jnp.concatenate([head, zs[8:tt - 8], tail], axis=0)

    W = RW_W
    r, k, v = zs[:, 0:W], zs[:, W:2 * W], zs[:, 2 * W:3 * W]
    base = 3 * W
    wd = (zs[:, base:base + LANE], zs[:, base + LANE:base + 2 * LANE])
    ad = zs[:, base + 2 * LANE:base + 3 * LANE]
    gd = zs[:, base + 3 * LANE:base + 4 * LANE]
    for d, lw_o in enumerate((lwf_o, lwb_o)):
        y = w0_ref[d:d + 1, :] + _dot(jnp.tanh(wd[d]), wup_ref[d])
        lw_o[0] = -math.exp(-0.5) * jax.nn.sigmoid(y)
    a = jax.nn.sigmoid(a0_ref[...] + _dot(ad, aup_ref[...]))
    g = _dot(jax.nn.sigmoid(gd), gup_ref[...])
    ones = ones_ref[...]
    kk = k * kk_ref[...]
    ss = _split_dot_r(kk * kk, ones, 2)
    kk = kk / jnp.maximum(jnp.sqrt(ss), 1e-12)
    kmod = k * (1.0 + (a - 1.0) * ka_ref[...])
    bonus = _split_dot_r(r * kmod * rk_ref[...], ones, 2) * v
    r_o[0] = r
    k_o[0] = kmod
    v_o[0] = v
    kk_o[0] = kk
    a_o[0] = a
    g_o[0] = g
    bonus_o[0] = bonus


def _even_in(h, p, n_ctx):
    B, S, D = h.shape
    tt = n_ctx
    halo = 16
    nt, nh = S // tt, S // halo
    W = RW_W
    full = lambda shape: pl.BlockSpec(shape, lambda b, j: (0,) * len(shape))
    row = lambda width: pl.BlockSpec((1, tt, width), lambda b, j: (b, j, 0))
    wg, wr = p["w_gla"], p["w_rw"]
    Z = wr.shape[1]
    return pl.pallas_call(
        functools.partial(_even_in_kernel, tt=tt, halo=halo),
        grid=(B, nt),
        in_specs=[row(D),
                  pl.BlockSpec((1, halo, D), lambda b, j: (b, jnp.maximum(j * (tt // halo) - 1, 0), 0)),
                  pl.BlockSpec((1, halo, D), lambda b, j: (b, jnp.minimum((j + 1) * (tt // halo), nh - 1), 0)),
                  _resident(wg.shape), _resident(wr.shape),
                  full((2, Z)), full((2, W)), full((2, LANE, W)), full((1, W)), full((LANE, W)), full((LANE, W)),
                  full((1, W)), full((1, W)), full((1, W)), full((W, W))],
        out_specs=[row(wg.shape[1])] + [row(W)] * 9,
        out_shape=[jax.ShapeDtypeStruct((B, S, wg.shape[1]), F32)] + [jax.ShapeDtypeStruct((B, S, W), F32)] * 9,
        compiler_params=_cp("parallel", "parallel"),
        name="even_in",
    )(h, h, h, wg, wr, p["mu"], p["w0"], p["wup"], p["a0"], p["aup"], p["gup"], p["k_k"], p["k_a"], p["r_k"],
      p["ones64"])


def _rw_stages(r_f, lw_f, k_f, v_f, kk_f, a_f, r_b, lw_b, k_b, v_b, kk_b, a_b, of_ref, ob_ref, s_scr):
    L = CHUNK
    nb = r_f.shape[0]
    ti, ji = _pair_ids(L)
    lane0 = lax.broadcasted_iota(jnp.int32, (1, 2 * L), 1) < L
    incl_w = (ji <= ti, ji >= ti)
    strict_w = (ji < ti, ji > ti)
    streams = [(bb, dr) for bb in range(nb) for dr in range(2)]
    at, rt, bh, kh, bl, kl, vb, g_tot = ([] for _ in range(8))
    for bb, dr in streams:
        r_ref, lw_ref, k_ref, v_ref, kk_ref, a_ref = ((r_f, lw_f, k_f, v_f, kk_f, a_f),
                                                      (r_b, lw_b, k_b, v_b, kk_b, a_b))[dr]
        m_incl, _ = _tri_masks(L, dr == 1)
        lw = lw_ref[bb]
        cs = _split_dot(m_incl.astype(BF16), lw, 3)
        tot = cs[0:1] if dr == 1 else cs[L - 1:L]
        kk, k = kk_ref[bb], k_ref[bb]
        bv = kk * a_ref[bb]
        g_inv = jnp.exp(-cs)
        g_rem = jnp.exp(tot - cs)
        g_tot.append(jnp.exp(tot))
        vb.append(v_ref[bb].astype(BF16))
        at.append((-(jnp.exp(cs - lw) * kk)).astype(BF16))
        rt.append((jnp.exp(cs) * r_ref[bb]).astype(BF16))
        bh.append((g_inv * bv).astype(BF16))
        kh.append((g_inv * k).astype(BF16))
        bl.append((g_rem * bv).astype(BF16))
        kl.append((g_rem * k).astype(BF16))
    es = [(d, p) for d in range(len(streams)) for p in range(RW_HEADS // 2)]
    ids = range(len(es))
    drs = [streams[d][1] for d, _ in es]
    ps = lambda p: slice(p * 2 * RW_DH, (p + 1) * 2 * RW_DH)
    bd = lambda x: _pair_blockdiag(x.astype(BF16), lane0)
    same_head = (lax.broadcasted_iota(jnp.int32, (2 * L, 2 * L), 0) // L) == (
        lax.broadcasted_iota(jnp.int32, (2 * L, 2 * L), 1) // L)
    yield
    s0 = [s_scr[d, p] for d, p in es]
    lhs = [jnp.concatenate([at[d][:, ps(p)], rt[d][:, ps(p)]], axis=0) for d, p in es]
    rhs = [jnp.concatenate([bd(bh[d][:, ps(p)]), bd(kh[d][:, ps(p)])], axis=0) for d, p in es]
    vbd = [bd(vb[d][:, ps(p)]) for d, p in es]
    sc = [_dot_nt(lhs[i], rhs[i]) for i in ids]
    zs = [_dot_nt(lhs[i], s0[i]) for i in ids]
    yield
    mab = [jnp.where(strict_w[drs[i]], sc[i][:L, :2 * L], 0.0) for i in ids]
    mak = [jnp.where(strict_w[drs[i]], sc[i][:L, 2 * L:], 0.0) for i in ids]
    nrb = [jnp.where(incl_w[drs[i]], sc[i][L:, :2 * L], 0.0) for i in ids]
    nrk = [jnp.where(incl_w[drs[i]], sc[i][L:, 2 * L:], 0.0) for i in ids]
    z0 = [zs[i][:L] + _dot(mak[i], vbd[i]) for i in ids]
    yield
    tinv = yield from _unit_tri_inverse(mab, (ti, ji), lane0)
    u = [_dot(tinv[i], bd(z0[i])) for i in ids]
    o1 = [_dot(nrk[i], vbd[i]) for i in ids]
    yield
    o2 = [_dot(nrb[i], bd(u[i])) for i in ids]
    upd = [_dot_tn(jnp.concatenate([u[i].astype(BF16), vb[d][:, ps(p)]], axis=0),
                   jnp.concatenate([bl[d][:, ps(p)], kl[d][:, ps(p)]], axis=0)) for i, (d, p) in enumerate(es)]
    yield
    for i, (d, p) in enumerate(es):
        bb, dr = streams[d]
        (of_ref, ob_ref)[dr][bb, :, ps(p)] = zs[i][L:] + o1[i] + o2[i]
        s_scr[d, p] = s0[i] * g_tot[d][:, ps(p)] + jnp.where(same_head, upd[i], 0.0)


def _even_scan_kernel(*refs):
    gla_in, rw_in = refs[0:6], refs[6:18]
    og_f, og_b, or_f, or_b, gla_state, rw_state = refs[18:]

    @pl.when(pl.program_id(1) == 0)
    def _():
        gla_state[...] = jnp.zeros_like(gla_state)
        rw_state[...] = jnp.zeros_like(rw_state)

    for _ in itertools.zip_longest(_rw_stages(*rw_in, or_f, or_b, rw_state),
                                   _gla_stages(*gla_in, og_f, og_b, gla_state)):
        pass


def _even_scans(z_gla, p, r, lwf, lwb, k, v, kk, a, n_ctx):
    B, S, W = r.shape
    nc, fwd, bwd = _scan_index_maps(S, n_ctx)
    nb = _scan_batch(B, SCAN_BATCH)
    ad_blk = (2 * GLA_QK + 2 * GLA_V) // (2 * LANE)
    qkv_spec = lambda cidx: pl.BlockSpec((nb, CHUNK, 2 * GLA_QK + GLA_V), lambda b, s: (b, cidx(s), 0))
    ad_spec = lambda cidx: pl.BlockSpec((nb, CHUNK, 2 * LANE), lambda b, s: (b, cidx(s), ad_blk))
    spec = lambda cidx, width: pl.BlockSpec((nb, CHUNK, width), lambda b, s: (b, cidx(s), 0))
    return pl.pallas_call(
        _even_scan_kernel,
        grid=(B // nb, nc),
        in_specs=[qkv_spec(fwd), ad_spec(fwd), qkv_spec(bwd), ad_spec(bwd),
                  pl.BlockSpec((2, LANE, GLA_QK), lambda b, s: (0, 0, 0)),
                  pl.BlockSpec((2, GLA_QK), lambda b, s: (0, 0))]
                 + [spec(fwd, W)] * 6 + [spec(bwd, W)] * 6,
        out_specs=[spec(fwd, GLA_V), spec(bwd, GLA_V), spec(fwd, W), spec(bwd, W)],
        out_shape=[jax.ShapeDtypeStruct((B, S, GLA_V), F32)] * 2 + [jax.ShapeDtypeStruct((B, S, W), F32)] * 2,
        scratch_shapes=[pltpu.VMEM((2 * nb, GLA_HEADS, GLA_DV, GLA_DK), F32),
                        pltpu.VMEM((2 * nb, RW_HEADS // 2, 2 * RW_DH, 2 * RW_DH), F32)],
        compiler_params=_cp("parallel", "arbitrary"),
        name="even_scans",
    )(z_gla, z_gla, z_gla, z_gla, p["gla_aup"], p["gla_bias"], r, lwf, k, v, kk, a, r, lwb, k, v, kk, a)


def _even_out_kernel(x_ref, ogf, ogb, gate_ref, orf, orb, bonus_ref, grw_ref, glag, lng, lnb, ones128, ones64,
                     w_ref, gl, gc, o_ref, *, tm, n_ctx):
    og = ogf[0] + ogb[0]
    ms = _split_dot_r(og * og, ones128[...], 1) * (1.0 / GLA_DV)
    gt = gate_ref[0]
    y_gla = og * lax.rsqrt(ms + 1e-6) * glag[...] * (gt * jax.nn.sigmoid(gt))
    of = orf[0] + orb[0]
    o64 = ones64[...]
    mean = _split_dot_r(of, o64, 2) * (1.0 / RW_DH)
    cen = of - mean
    var = _split_dot_r(cen * cen, o64, 1) * (1.0 / RW_DH)
    y_rw = (cen * lax.rsqrt(var + RW_GN_EPS) * lng[...] + lnb[...] + bonus_ref[0]) * grw_ref[0]
    y = jnp.concatenate([y_gla, y_rw], axis=1).astype(BF16)
    rows = pl.program_id(1) * tm + lax.broadcasted_iota(jnp.int32, (tm, 1), 0)
    gate = jnp.where(rows < n_ctx, gc[0], gl[0])
    o_ref[0] = x_ref[0] + gate * jnp.dot(y, w_ref[...], preferred_element_type=F32)


def _even_out(xs, og_f, og_b, z_gla, or_f, or_b, bonus, g_rw, p, mod, n_ctx, *, tm):
    B, S, D = xs.shape
    W = RW_W
    row = lambda width: pl.BlockSpec((1, tm, width), lambda b, j: (b, j, 0))
    full = lambda shape: pl.BlockSpec(shape, lambda b, j: (0,) * len(shape))
    return pl.pallas_call(
        functools.partial(_even_out_kernel, tm=tm, n_ctx=n_ctx),
        grid=(B, S // tm),
        in_specs=[row(D), row(GLA_V), row(GLA_V),
                  pl.BlockSpec((1, tm, GLA_V), lambda b, j: (b, j, (2 * GLA_QK + GLA_V) // GLA_V)),
                  row(W), row(W), row(W), row(W),
                  full((1, GLA_V)), full((1, W)), full((1, W)), full((GLA_V, GLA_V)), full((W, W)),
                  full((GLA_V + W, D)), *_mod_specs(B, D, 2, 2)],
        out_specs=row(D),
        out_shape=jax.ShapeDtypeStruct((B, S, D), F32),
        compiler_params=_cp("parallel", "parallel"),
        name="even_out",
    )(xs, og_f, og_b, z_gla, or_f, or_b, bonus, g_rw, p["gla_g"], p["ln_g"], p["ln_b"], p["ones128"], p["ones64"],
      p["w_out"], mod, mod)


def _odd_in_kernel(h_ref, w_ref, cos_ref, sin_ref, qg_ref, kg_ref, ones64, qc_o, kc_o, vc_o, qd_o, kd_o, vd_o):
    h = h_ref[0]
    cos, sin = cos_ref[...], sin_ref[...]
    o64 = ones64[...]
    nq, nk, nd = C_HEADS * HEAD_DIM, C_KV * HEAD_DIM, NA_HEADS * HEAD_DIM

    def piece(lo, width):
        return jnp.dot(h, w_ref[:, lo:lo + width], preferred_element_type=F32)

    def norm_rope(x, g, width):
        ms = _split_dot_r(x * x, o64[:width, :width], 1) * (1.0 / HEAD_DIM)
        y = x * lax.rsqrt(ms + 1e-6) * g
        lane = lax.broadcasted_iota(jnp.int32, (1, width), 1)
        first = (lane % (HEAD_DIM // 2)) < (HEAD_DIM // 4)
        swapped = jnp.where(first, pltpu.roll(y, width - HEAD_DIM // 4, 1), pltpu.roll(y, HEAD_DIM // 4, 1))
        return y * cos[:, :width] + swapped * sin[:, :width]

    qc_o[0] = (norm_rope(piece(0, nq), qg_ref[...], nq) * HEAD_DIM ** -0.5).astype(BF16)
    kc_o[0] = norm_rope(piece(nq, nk), kg_ref[:, :nk], nk).astype(BF16)
    vc_o[0] = piece(nq + nk, nk).astype(BF16)
    base = nq + 2 * nk
    qd_o[0] = (piece(base, nd) * HEAD_DIM ** -0.5).astype(BF16)
    kd_o[0] = piece(base + nd, nd).astype(BF16)
    vd_o[0] = piece(base + 2 * nd, nd).astype(BF16)


def _odd_in(h, w, cos_t, sin_t, qg, kg, ones64, *, tm):
    B, S, D = h.shape
    nq, nk, nd = C_HEADS * HEAD_DIM, C_KV * HEAD_DIM, NA_HEADS * HEAD_DIM
    full = lambda shape: pl.BlockSpec(shape, lambda b, j: (0,) * len(shape))
    row = lambda width: pl.BlockSpec((1, tm, width), lambda b, j: (b, j, 0))
    widths = (nq, nk, nk, nd, nd, nd)
    return pl.pallas_call(
        _odd_in_kernel,
        grid=(B, S // tm),
        in_specs=[row(D), _resident(w.shape),
                  pl.BlockSpec((tm, nq), lambda b, j: (j, 0)), pl.BlockSpec((tm, nq), lambda b, j: (j, 0)),
                  full((1, nq)), full((1, nq)), full((nq, nq))],
        out_specs=[row(wd) for wd in widths],
        out_shape=[jax.ShapeDtypeStruct((B, S, wd), BF16) for wd in widths],
        compiler_params=_cp("parallel", "parallel"),
        name="odd_in",
    )(h, w, cos_t, sin_t, qg, kg, ones64)


def _pair_attn_kernel(q_ref, k_ref, v_ref, o_ref, *, shared_kv):
    q = q_ref[0]
    tq = q.shape[0]
    ngroups = q.shape[1] // LANE
    lane = lax.broadcasted_iota(jnp.int32, (1, LANE), 1)
    masks = (lane < HEAD_DIM, lane >= HEAD_DIM)
    zero = jnp.zeros((), q.dtype)
    if shared_kv:
        kp, vp = k_ref[0], v_ref[0]
        es = [(g, hm) for g in range(ngroups) for hm in range(2)]
        ids = range(len(es))
        lhs = [jnp.where(masks[hm], q[:, g * LANE:(g + 1) * LANE], zero) for g, hm in es]
        s = [_dot_nt(lhs[i], kp) for i in ids]
        p = [jnp.exp(s[i] - jnp.max(s[i], axis=-1, keepdims=True)) for i in ids]
        den = [jnp.sum(p[i], axis=-1, keepdims=True) for i in ids]
        o = [_dot(p[i], vp) / den[i] for i in ids]
        for g in range(ngroups):
            o_ref[0, :, g * LANE:(g + 1) * LANE] = jnp.where(masks[0], o[2 * g], o[2 * g + 1]).astype(o_ref.dtype)
    else:
        for g in range(ngroups):
            qp = q[:, g * LANE:(g + 1) * LANE]
            kp = k_ref[0, :, g * LANE:(g + 1) * LANE]
            vp = v_ref[0, :, g * LANE:(g + 1) * LANE]
            lhs = jnp.concatenate([jnp.where(masks[0], qp, zero), jnp.where(masks[1], qp, zero)], axis=0)
            s = _dot_nt(lhs, kp)
            p = jnp.exp(s - jnp.max(s, axis=-1, keepdims=True))
            o = _dot(p, vp) / jnp.sum(p, axis=-1, keepdims=True)
            o_ref[0, :, g * LANE:(g + 1) * LANE] = jnp.where(masks[0], o[:tq], o[tq:]).astype(o_ref.dtype)


def _pair_attention(q, k, v, *, q_row0, n_q, n_k, tq, shared_kv, name):
    B, S, QW = q.shape
    KW = k.shape[2]
    return pl.pallas_call(
        functools.partial(_pair_attn_kernel, shared_kv=shared_kv),
        grid=(B, n_q // tq),
        in_specs=[pl.BlockSpec((1, tq, QW), lambda b, i: (b, q_row0 // tq + i, 0)),
                  pl.BlockSpec((1, n_k, KW), lambda b, i: (b, 0, 0)),
                  pl.BlockSpec((1, n_k, KW), lambda b, i: (b, 0, 0))],
        out_specs=pl.BlockSpec((1, tq, QW), lambda b, i: (b, i, 0)),
        out_shape=jax.ShapeDtypeStruct((B, n_q, QW), BF16),
        compiler_params=_cp("parallel", "parallel"),
        name=name,
    )(q, k, v)


NA_ROWS_PER_STEP = 2


def _na_kernel(q_ref, k_ref, v_ref, *rest, n_ctx, n_rows, kh, rps):
    bias_refs, o_ref = rest[:rps], rest[rps]
    nkeys = kh * GRID_W
    lane = lax.broadcasted_iota(jnp.int32, (1, LANE), 1)
    m0 = lane < HEAD_DIM
    zero = jnp.zeros((), q_ref.dtype)
    starts = []
    for j in range(rps):
        r = pl.program_id(1) * rps + j
        rs = jnp.clip(r - kh // 2, 0, n_rows - kh)
        starts.append(pl.multiple_of(n_ctx + rs * GRID_W, GRID_W))
    es = [(j, g) for j in range(rps) for g in range(NA_HEADS // 2)]
    ids = range(len(es))
    gs = lambda g: slice(g * LANE, (g + 1) * LANE)
    qrow = lambda j, g: q_ref[0, j * GRID_W:(j + 1) * GRID_W, gs(g)]
    lhs = [jnp.concatenate([jnp.where(m0, qrow(j, g), zero), jnp.where(m0, zero, qrow(j, g))], axis=0)
           for j, g in es]
    s_nb = [_dot_nt(lhs[i], k_ref[0, pl.ds(starts[j], nkeys), gs(g)]) for i, (j, g) in enumerate(es)]
    s_cx = [_dot_nt(lhs[i], k_ref[0, 0:n_ctx, gs(g)]) for i, (j, g) in enumerate(es)]
    s_nb = [s_nb[i] + jnp.concatenate([bias_refs[j][0, 2 * g], bias_refs[j][0, 2 * g + 1]], axis=0)
            for i, (j, g) in enumerate(es)]
    m = [jnp.maximum(jnp.max(s_nb[i], axis=-1, keepdims=True), jnp.max(s_cx[i], axis=-1, keepdims=True))
         for i in ids]
    p_nb = [jnp.exp(s_nb[i] - m[i]) for i in ids]
    p_cx = [jnp.exp(s_cx[i] - m[i]) for i in ids]
    den = [jnp.sum(p_nb[i], axis=-1, keepdims=True) + jnp.sum(p_cx[i], axis=-1, keepdims=True) for i in ids]
    o_nb = [_dot(p_nb[i], v_ref[0, pl.ds(starts[j], nkeys), gs(g)]) for i, (j, g) in enumerate(es)]
    o_cx = [_dot(p_cx[i], v_ref[0, 0:n_ctx, gs(g)]) for i, (j, g) in enumerate(es)]
    for i, (j, g) in enumerate(es):
        o = (o_nb[i] + o_cx[i]) / den[i]
        o_ref[0, j * GRID_W:(j + 1) * GRID_W, gs(g)] = jnp.where(m0, o[:GRID_W], o[GRID_W:]).astype(o_ref.dtype)


def _na_attention(qd, kd, vd, bias_tab, n_ctx):
    B, S, W = qd.shape
    T = S - n_ctx
    n_rows = T // GRID_W
    kh = min(NA_KH, n_rows)
    nkeys = kh * GRID_W
    rps = NA_ROWS_PER_STEP if n_rows % NA_ROWS_PER_STEP == 0 and n_ctx % (NA_ROWS_PER_STEP * GRID_W) == 0 else 1
    tq = rps * GRID_W

    def bias_spec(j):
        def idx(b, i):
            r = i * rps + j
            return (jnp.clip(r - kh // 2, 0, n_rows - kh) - r + kh - 1, 0, 0, 0)
        return pl.BlockSpec((1, NA_HEADS, GRID_W, nkeys), idx)

    return pl.pallas_call(
        functools.partial(_na_kernel, n_ctx=n_ctx, n_rows=n_rows, kh=kh, rps=rps),
        grid=(B, n_rows // rps),
        in_specs=[pl.BlockSpec((1, tq, W), lambda b, i: (b, n_ctx // tq + i, 0)),
                  pl.BlockSpec((1, S, W), lambda b, i: (b, 0, 0)),
                  pl.BlockSpec((1, S, W), lambda b, i: (b, 0, 0)),
                  *[bias_spec(j) for j in range(rps)]],
        out_specs=pl.BlockSpec((1, tq, W), lambda b, i: (b, i, 0)),
        out_shape=jax.ShapeDtypeStruct((B, T, W), BF16),
        compiler_params=_cp("parallel", "arbitrary"),
        name="na_attention",
    )(qd, kd, vd, *([bias_tab] * rps))


def _na_bias_table(rpb, n_rows):
    kh = min(NA_KH, n_rows)
    kw = min(NA_KW, GRID_W)
    col = np.arange(GRID_W)
    start = np.clip(col - kw // 2, 0, GRID_W - kw)
    in_win = (col[None, :] >= start[:, None]) & (col[None, :] < start[:, None] + kw)
    dc = np.clip(col[None, :] - col[:, None], -(NA_KW - 1), NA_KW - 1) + NA_KW - 1
    bias_cols = rpb[:, :, dc]
    slabs = []
    for st in range(kh):
        sl = bias_cols[:, st + NA_KH - kh:st + NA_KH - kh + kh]
        sl = jnp.where(in_win[None, None], sl, NEG).transpose(0, 2, 1, 3)
        slabs.append(sl.reshape(NA_HEADS, GRID_W, kh * GRID_W))
    return jnp.stack(slabs).astype(F32)


def _proj_res_kernel(x_ref, y_ref, w_ref, gl, gc, o_ref, *, tm, n_ctx):
    rows = pl.program_id(1) * tm + lax.broadcasted_iota(jnp.int32, (tm, 1), 0)
    gate = jnp.where(rows < n_ctx, gc[0], gl[0])
    o_ref[0] = x_ref[0] + gate * jnp.dot(y_ref[0], w_ref[...], preferred_element_type=F32)


def _proj_res(xs, y, w, mod, n_ctx, *, tm):
    B, S, D = xs.shape
    K = y.shape[2]
    row = lambda width: pl.BlockSpec((1, tm, width), lambda b, j: (b, j, 0))
    return pl.pallas_call(
        functools.partial(_proj_res_kernel, tm=tm, n_ctx=n_ctx),
        grid=(B, S // tm),
        in_specs=[row(D), row(K), pl.BlockSpec((K, D), lambda b, j: (0, 0)), *_mod_specs(B, D, 2, 2)],
        out_specs=row(D),
        out_shape=jax.ShapeDtypeStruct((B, S, D), F32),
        compiler_params=_cp("parallel", "parallel"),
        name="proj_res",
    )(xs, y, w, mod, mod)


def _final_norm_kernel(x_ref, g_ref, o_ref):
    x = x_ref[0]
    o_ref[0] = x * lax.rsqrt(jnp.mean(x * x, axis=-1, keepdims=True) + 1e-6) * g_ref[...]


def _final_norm(xs, g, n_ctx, *, tm):
    B, S, D = xs.shape
    T = S - n_ctx
    return pl.pallas_call(
        _final_norm_kernel,
        grid=(B, T // tm),
        in_specs=[pl.BlockSpec((1, tm, D), lambda b, j: (b, n_ctx // tm + j, 0)),
                  pl.BlockSpec((1, D), lambda b, j: (0, 0))],
        out_specs=pl.BlockSpec((1, tm, D), lambda b, j: (b, j, 0)),
        out_shape=jax.ShapeDtypeStruct((B, T, D), F32),
        compiler_params=_cp("parallel", "parallel"),
        name="final_norm",
    )(xs, g.reshape(1, D))


def _block_ones(width, block):
    idx = np.arange(width) // block
    return jnp.asarray(idx[:, None] == idx[None, :], BF16)


def _pad_cols(w, width):
    return jnp.pad(w, ((0, 0), (0, width - w.shape[1])))


def _pad_rows(w, height):
    return jnp.pad(w, ((0, height - w.shape[0]), (0, 0)))


def _even_params(j, ev_w_in, ev_w_out, gla_a_up, gla_a_bias, gla_norm_g, rw_mu, rw_w0, rw_w_up, rw_a0, rw_a_up,
                 rw_g_up, rw_k_k, rw_k_a, rw_r_k, rw_ln_g, rw_ln_b):
    w_in = ev_w_in[j]
    D = w_in.shape[0]
    gq = 2 * GLA_QK + 2 * GLA_V
    w_gla = jnp.concatenate([w_in[:, :gq], _pad_cols(w_in[:, gq:gq + GLA_RANK], LANE),
                             _pad_cols(w_in[:, gq + GLA_RANK:gq + 2 * GLA_RANK], LANE)], axis=1)
    o = gq + 2 * GLA_RANK
    sizes = (3 * RW_W, RW_DECAY_RANK, RW_DECAY_RANK, RW_A_RANK, RW_G_RANK)
    offs = np.cumsum((0,) + sizes)
    pieces = [w_in[:, o:o + 3 * RW_W]] + [_pad_cols(w_in[:, o + offs[i]:o + offs[i + 1]], LANE) for i in range(1, 5)]
    w_rw = jnp.concatenate(pieces, axis=1)
    mu = rw_mu[j]
    mu_pieces = [mu[:, :3 * RW_W]] + [_pad_cols(mu[:, offs[i]:offs[i + 1]], LANE) for i in range(1, 5)]
    return {
        "w_gla": w_gla.astype(BF16), "w_rw": w_rw.astype(BF16), "w_out": ev_w_out[j].astype(BF16),
        "gla_aup": jnp.stack([_pad_rows(gla_a_up[j, d], LANE) for d in range(2)]).astype(BF16),
        "gla_bias": gla_a_bias[j], "gla_g": jnp.tile(gla_norm_g[j], GLA_HEADS).reshape(1, GLA_V),
        "mu": jnp.concatenate(mu_pieces, axis=1), "w0": rw_w0[j],
        "wup": jnp.stack([_pad_rows(rw_w_up[j, d], LANE) for d in range(2)]).astype(BF16),
        "a0": rw_a0[j].reshape(1, RW_W), "aup": _pad_rows(rw_a_up[j], LANE).astype(BF16),
        "gup": _pad_rows(rw_g_up[j], LANE).astype(BF16),
        "k_k": rw_k_k[j].reshape(1, RW_W), "k_a": rw_k_a[j].reshape(1, RW_W), "r_k": rw_r_k[j].reshape(1, RW_W),
        "ln_g": rw_ln_g[j].reshape(1, RW_W), "ln_b": rw_ln_b[j].reshape(1, RW_W),
        "ones64": _block_ones(RW_W, RW_DH), "ones128": _block_ones(GLA_V, GLA_DV),
    }


def _gqa_head_order():
    per_kv = C_HEADS // C_KV
    order = []
    for jj in range(per_kv):
        for kv in range(C_KV):
            h = kv * per_kv + jj
            order.extend(range(h * HEAD_DIM, (h + 1) * HEAD_DIM))
    return np.asarray(order)


def _rope_tables(n_ctx, T, width):
    t = jnp.arange(T)
    pos = jnp.stack([t // GRID_W, t % GRID_W], axis=-1).astype(F32)
    half = HEAD_DIM // 2
    inv = ROPE_THETA ** (-jnp.arange(0, half, 2, dtype=F32) / half)
    ang = pos[:, :, None] * inv
    cos, sin = jnp.cos(ang), jnp.sin(ang)
    cos_h = jnp.concatenate([cos, cos], axis=-1).reshape(T, HEAD_DIM)
    sin_h = jnp.concatenate([-sin, sin], axis=-1).reshape(T, HEAD_DIM)
    cos_h = jnp.concatenate([jnp.ones((n_ctx, HEAD_DIM), F32), cos_h], axis=0)
    sin_h = jnp.concatenate([jnp.zeros((n_ctx, HEAD_DIM), F32), sin_h], axis=0)
    reps = width // HEAD_DIM
    return jnp.tile(cos_h, (1, reps)), jnp.tile(sin_h, (1, reps))


def _even_layer(xs, h, mod, n_ctx, p):
    z_gla, r, lwf, lwb, k, v, kk, a, g_rw, bonus = _even_in(h, p, n_ctx)
    og_f, og_b, or_f, or_b = _even_scans(z_gla, p, r, lwf, lwb, k, v, kk, a, n_ctx)
    return _even_out(xs, og_f, og_b, z_gla, or_f, or_b, bonus, g_rw, p, mod, n_ctx,
                     tm=_row_tile(xs.shape[1], ROW_CAP_FFN))


def _odd_layer(xs, h, mod, n_ctx, p, need_ctx):
    B, S, D = xs.shape
    T = S - n_ctx
    qc, kc, vc, qd, kd, vd = _odd_in(h, p["w_in"], p["cos"], p["sin"], p["qg"], p["kg"], p["ones64"],
                                     tm=_row_tile(S, ROW_CAP_FFN))
    y_gqa = _pair_attention(qc, kc, vc, q_row0=n_ctx, n_q=T, n_k=S, tq=256, shared_kv=True, name="gqa")
    y_na = _na_attention(qd, kd, vd, p["bias_tab"], n_ctx)
    parts_g, parts_n = [y_gqa], [y_na]
    if need_ctx:
        parts_g.insert(0, _pair_attention(qc, kc, vc, q_row0=0, n_q=n_ctx, n_k=n_ctx, tq=n_ctx, shared_kv=True,
                                          name="gqa_ctx"))
        parts_n.insert(0, _pair_attention(qd, kd, vd, q_row0=0, n_q=n_ctx, n_k=n_ctx, tq=n_ctx, shared_kv=False,
                                          name="na_ctx"))
    else:
        zeros = jnp.zeros((B, n_ctx, y_gqa.shape[2]), BF16)
        parts_g.insert(0, zeros)
        parts_n.insert(0, zeros)
    y = jnp.concatenate([jnp.concatenate(parts_g, axis=1), jnp.concatenate(parts_n, axis=1)], axis=2)
    return _proj_res(xs, y, p["w_out"], mod, n_ctx, tm=_row_tile(S, ROW_CAP_MATMUL))


def kernel(x, c, ctx, c_ctx, w_mod, b_mod, norm1_g, norm2_g, ffn_w13, ffn_w2, ev_w_in, ev_w_out, gla_a_up,
           gla_a_bias, gla_norm_g, rw_mu, rw_w0, rw_w_up, rw_a0, rw_a_up, rw_g_up, rw_k_k, rw_k_a, rw_r_k, rw_ln_g,
           rw_ln_b, od_w_in, od_w_out, cq_norm_g, ck_norm_g, na_rpb, final_g):
    B, T, D = x.shape
    n_ctx = ctx.shape[1]
    S = n_ctx + T
    xs = jnp.concatenate([ctx, x], axis=1)
    mods = _mod_vectors(c, c_ctx, w_mod, b_mod)
    nq, nk = C_HEADS * HEAD_DIM, C_KV * HEAD_DIM
    order = _gqa_head_order()
    cos_t, sin_t = _rope_tables(n_ctx, T, nq)
    ones64 = _block_ones(nq, HEAD_DIM)
    h = _norm_mod(xs, norm1_g[0], mods[0], n_ctx, tm=_row_tile(S, ROW_CAP_MATMUL))
    for i in range(DEPTH):
        j = i // 2
        mod = mods[i]
        if i % 2 == 0:
            p = _even_params(j, ev_w_in, ev_w_out, gla_a_up, gla_a_bias, gla_norm_g, rw_mu, rw_w0, rw_w_up, rw_a0,
                             rw_a_up, rw_g_up, rw_k_k, rw_k_a, rw_r_k, rw_ln_g, rw_ln_b)
            xs = _even_layer(xs, h, mod, n_ctx, p)
        else:
            w_in = od_w_in[j]
            w_out = od_w_out[j]
            p = {
                "w_in": jnp.concatenate([w_in[:, :nq][:, order], w_in[:, nq:]], axis=1).astype(BF16),
                "w_out": jnp.concatenate([w_out[:nq][order], w_out[nq:]], axis=0).astype(BF16),
                "cos": cos_t, "sin": sin_t, "ones64": ones64,
                "qg": jnp.tile(cq_norm_g[j], C_HEADS).reshape(1, nq),
                "kg": jnp.tile(ck_norm_g[j], C_HEADS).reshape(1, nq),
                "bias_tab": _na_bias_table(na_rpb[j], T // GRID_W),
            }
            xs = _odd_layer(xs, h, mod, n_ctx, p, need_ctx=i < DEPTH - 1)
        next_norm = (norm1_g[i + 1], mods[i + 1]) if i + 1 < DEPTH else None
        xs, h = _ffn(xs, norm2_g[i], mod, ffn_w13[i].astype(BF16), ffn_w2[i].astype(BF16), n_ctx,
                     tm=_row_tile(S, ROW_CAP_FFN), tf=FFN_COL_TILE, next_norm=next_norm)
    return _final_norm(xs, final_g, n_ctx, tm=_row_tile(n_ctx, ROW_CAP_ELEMENTWISE))
```

```python
import functools
import itertools
import math

import jax
import jax.numpy as jnp
import numpy as np
from jax import lax
from jax.experimental import pallas as pl
from jax.experimental.pallas import tpu as pltpu

F32 = jnp.float32
BF16 = jnp.bfloat16

DEPTH = 4
GRID_W = 64
CHUNK = 64
LANE = 128

GLA_HEADS, GLA_DK, GLA_DV, GLA_RANK, GLA_TAU = 4, 64, 128, 16, 16.0
GLA_QK, GLA_V = GLA_HEADS * GLA_DK, GLA_HEADS * GLA_DV
GLA_Z = 2 * GLA_QK + 2 * GLA_V + 2 * LANE

RW_HEADS, RW_DH, RW_W = 8, 64, 512
RW_DECAY_RANK, RW_A_RANK, RW_G_RANK = 32, 32, 96
RW_GN_EPS = 64e-5
RW_Z = 3 * RW_W + 4 * LANE

HEAD_DIM, C_HEADS, C_KV, NA_HEADS, NA_KH, NA_KW = 64, 8, 2, 8, 8, 16
ROPE_THETA = 10000.0
NEG = -1e30

VMEM_LIMIT = 56 * 1024 * 1024
ROW_CAP_MATMUL = 1152
ROW_CAP_FFN = 576
ROW_CAP_ELEMENTWISE = 256
FFN_COL_TILE = 256


def _row_tile(n_rows, cap):
    return max(d for d in range(8, cap + 1, 8) if n_rows % d == 0)


def _cp(*sem):
    return pltpu.CompilerParams(dimension_semantics=sem, vmem_limit_bytes=VMEM_LIMIT)


def _dot(a, b):
    return jnp.dot(a.astype(BF16), b.astype(BF16), preferred_element_type=F32)


def _dot_nt(a, b):
    return lax.dot_general(a.astype(BF16), b.astype(BF16), (((1,), (1,)), ((), ())), preferred_element_type=F32)


def _dot_tn(a, b):
    return lax.dot_general(a.astype(BF16), b.astype(BF16), (((0,), (0,)), ((), ())), preferred_element_type=F32)


def _split_dot(m, x, parts):
    acc = None
    rem = x
    for _ in range(parts):
        piece = rem.astype(BF16)
        rem = rem - piece.astype(F32)
        t = jnp.dot(m, piece, preferred_element_type=F32)
        acc = t if acc is None else acc + t
    return acc


def _split_dot_r(x, m, parts):
    acc = None
    rem = x
    for _ in range(parts):
        piece = rem.astype(BF16)
        rem = rem - piece.astype(F32)
        t = jnp.dot(piece, m, preferred_element_type=F32)
        acc = t if acc is None else acc + t
    return acc


def _rms_mod(x, g, sc_l, sh_l, sc_c, sh_c, row0, n_ctx):
    y = x * lax.rsqrt(jnp.mean(x * x, axis=-1, keepdims=True) + 1e-6) * g
    rows = row0 + lax.broadcasted_iota(jnp.int32, (x.shape[0], 1), 0)
    is_ctx = rows < n_ctx
    sc = jnp.where(is_ctx, sc_c, sc_l)
    sh = jnp.where(is_ctx, sh_c, sh_l)
    return y * (1.0 + sc) + sh


def _tri_masks(n, reverse):
    ri = lax.broadcasted_iota(jnp.int32, (n, n), 0)
    ci = lax.broadcasted_iota(jnp.int32, (n, n), 1)
    if reverse:
        return ci >= ri, ci > ri
    return ci <= ri, ci < ri


INV_BASE = 16


def _pair_ids(n):
    ti = lax.broadcasted_iota(jnp.int32, (n, 2 * n), 0)
    ji = lax.broadcasted_iota(jnp.int32, (n, 2 * n), 1) % n
    return ti, ji


def _pair_blockdiag(x, lane_is_head0):
    zero = jnp.zeros((), x.dtype)
    return jnp.concatenate([jnp.where(lane_is_head0, x, zero), jnp.where(lane_is_head0, zero, x)], axis=0)


def _unit_tri_inverse(a, ids_tj, lane_is_head0):
    ti, ji = ids_tj
    n = a[0].shape[0]
    ids = range(len(a))
    bd = lambda x: _pair_blockdiag(x.astype(BF16), lane_is_head0)
    same = lambda bs: (ti // bs) == (ji // bs)
    base = same(INV_BASE)
    eye = (ti == ji).astype(F32)
    p = [jnp.where(base, x, 0.0) for x in a]
    t = [eye + x for x in p]
    m = 1
    while 2 * m < INV_BASE:
        p = [_dot(x, bd(x)) for x in p]
        yield
        tp = [_dot(t[i], bd(p[i])) for i in ids]
        t = [t[i] + tp[i] for i in ids]
        yield
        m *= 2
    bs = INV_BASE
    while bs < n:
        mask = same(2 * bs) & jnp.logical_not(same(bs))
        ta = [_dot(t[i], bd(jnp.where(mask, a[i], 0.0))) for i in ids]
        yield
        tat = [_dot(ta[i], bd(t[i])) for i in ids]
        t = [t[i] + tat[i] for i in ids]
        yield
        bs *= 2
    return t


def _mod_kernel(s_ref, w_ref, b_ref, o_ref):
    s = s_ref[...]
    s = s * jax.nn.sigmoid(s)
    o_ref[0] = _dot(s, w_ref[0]) + b_ref[0]


def _mod_vectors(c, c_ctx, w_mod, b_mod):
    B, D = c.shape
    bp = -(-(B + 1) // 8) * 8
    s_in = jnp.zeros((bp, D), F32).at[:B].set(c).at[B].set(c_ctx)
    n6 = w_mod.shape[-1]
    tn = 1536
    out = pl.pallas_call(
        _mod_kernel,
        grid=(DEPTH, n6 // tn),
        in_specs=[pl.BlockSpec((bp, D), lambda l, n: (0, 0)),
                  pl.BlockSpec((1, D, tn), lambda l, n: (l, 0, n)),
                  pl.BlockSpec((1, 1, tn), lambda l, n: (l, 0, n))],
        out_specs=pl.BlockSpec((1, bp, tn), lambda l, n: (l, 0, n)),
        out_shape=jax.ShapeDtypeStruct((DEPTH, bp, n6), F32),
        compiler_params=_cp("parallel", "parallel"),
        name="mod_vectors",
    )(s_in, w_mod, b_mod.reshape(DEPTH, 1, n6))
    return out.reshape(DEPTH, bp, 1, n6)


def _mod_specs(B, D, idx, grid_rank):
    if grid_rank == 2:
        return [pl.BlockSpec((1, 1, D), lambda b, j: (b, 0, idx)),
                pl.BlockSpec((1, 1, D), lambda b, j: (B, 0, idx))]
    return [pl.BlockSpec((1, 1, D), lambda b, j, n: (b, 0, idx)),
            pl.BlockSpec((1, 1, D), lambda b, j, n: (B, 0, idx))]


def _norm_mod_kernel(x_ref, g_ref, shl, shc, scl, scc, o_ref, *, tm, n_ctx):
    h = _rms_mod(x_ref[0], g_ref[...], scl[0], shl[0], scc[0], shc[0], pl.program_id(1) * tm, n_ctx)
    o_ref[0] = h.astype(BF16)


def _norm_mod(xs, g, mod, n_ctx, *, tm):
    B, S, D = xs.shape
    row = pl.BlockSpec((1, tm, D), lambda b, j: (b, j, 0))
    return pl.pallas_call(
        functools.partial(_norm_mod_kernel, tm=tm, n_ctx=n_ctx),
        grid=(B, S // tm),
        in_specs=[row, pl.BlockSpec((1, D), lambda b, j: (0, 0)), *_mod_specs(B, D, 0, 2), *_mod_specs(B, D, 1, 2)],
        out_specs=row,
        out_shape=jax.ShapeDtypeStruct((B, S, D), BF16),
        compiler_params=_cp("parallel", "parallel"),
        name="norm_mod",
    )(xs, g.reshape(1, D), mod, mod, mod, mod)


def _ffn_kernel(x_ref, g_ref, shl, shc, scl, scc, gl, gc, w13_ref, w2_ref, *rest, tm, tf, n_ctx, emit_next):
    if emit_next:
        gn_ref, nshl, nshc, nscl, nscc, o_ref, hn_ref = rest
    else:
        (o_ref,) = rest
    x = x_ref[0]
    row0 = pl.program_id(1) * tm
    h = _rms_mod(x, g_ref[...], scl[0], shl[0], scc[0], shc[0], row0, n_ctx).astype(BF16)
    n_hidden = w2_ref.shape[0]
    acc = None
    for f in range(n_hidden // tf):
        a = jnp.dot(h, w13_ref[:, f * tf:(f + 1) * tf], preferred_element_type=F32)
        b = jnp.dot(h, w13_ref[:, n_hidden + f * tf:n_hidden + (f + 1) * tf], preferred_element_type=F32)
        u = (a * jax.nn.sigmoid(a) * b).astype(BF16)
        t = jnp.dot(u, w2_ref[f * tf:(f + 1) * tf, :], preferred_element_type=F32)
        acc = t if acc is None else acc + t
    rows = row0 + lax.broadcasted_iota(jnp.int32, (tm, 1), 0)
    gate = jnp.where(rows < n_ctx, gc[0], gl[0])
    y = x + gate * acc
    o_ref[0] = y
    if emit_next:
        hn_ref[0] = _rms_mod(y, gn_ref[...], nscl[0], nshl[0], nscc[0], nshc[0], row0, n_ctx).astype(BF16)


def _resident(shape):
    return pl.BlockSpec(shape, lambda b, j: (0,) * len(shape), pipeline_mode=pl.Buffered(1))


def _ffn(xs, g, mod, w13, w2, n_ctx, *, tm, tf, next_norm=None):
    B, S, D = xs.shape
    row = pl.BlockSpec((1, tm, D), lambda b, j: (b, j, 0))
    vec = pl.BlockSpec((1, D), lambda b, j: (0, 0))
    in_specs = [row, vec, *_mod_specs(B, D, 3, 2), *_mod_specs(B, D, 4, 2), *_mod_specs(B, D, 5, 2),
                _resident(w13.shape), _resident(w2.shape)]
    args = [xs, g.reshape(1, D), mod, mod, mod, mod, mod, mod, w13, w2]
    out_specs, out_shape = [row], [jax.ShapeDtypeStruct((B, S, D), F32)]
    if next_norm is not None:
        g_next, mod_next = next_norm
        in_specs += [vec, *_mod_specs(B, D, 0, 2), *_mod_specs(B, D, 1, 2)]
        args += [g_next.reshape(1, D), mod_next, mod_next, mod_next, mod_next]
        out_specs.append(row)
        out_shape.append(jax.ShapeDtypeStruct((B, S, D), BF16))
    out = pl.pallas_call(
        functools.partial(_ffn_kernel, tm=tm, tf=tf, n_ctx=n_ctx, emit_next=next_norm is not None),
        grid=(B, S // tm),
        in_specs=in_specs,
        out_specs=out_specs,
        out_shape=out_shape,
        compiler_params=_cp("parallel", "parallel"),
        name="ffn",
    )(*args)
    return (out[0], out[1]) if next_norm is not None else (out[0], None)


def _chunk_index(s, nc_ctx, nc, reverse):
    if not reverse:
        return s
    return jnp.where(s < nc_ctx, nc_ctx - 1 - s, nc + nc_ctx - 1 - s)


def _gla_stages(qkv_f, ad_f, qkv_b, ad_b, aup_ref, bias_ref, of_ref, ob_ref, st_scr):
    L = CHUNK
    nb = qkv_f.shape[0]
    streams = [(bb, d) for bb in range(nb) for d in range(2)]
    qb, kb, kt, gt, v, incl = [], [], [], [], [], []
    for bb, d in streams:
        qkv_ref, ad_ref = ((qkv_f, ad_f), (qkv_b, ad_b))[d]
        m, _ = _tri_masks(L, d == 1)
        ad = ad_ref[bb][:, d * LANE:(d + 1) * LANE]
        y = _dot(ad, aup_ref[d]) + bias_ref[d:d + 1, :]
        la = (jnp.minimum(y, 0.0) - jnp.log(1.0 + jnp.exp(-jnp.abs(y)))) * (1.0 / GLA_TAU)
        cs = _split_dot(m.astype(BF16), la, 3)
        tot = cs[0:1] if d == 1 else cs[L - 1:L]
        qkv = qkv_ref[bb]
        q = qkv[:, 0:GLA_QK] * GLA_DK ** -0.5
        k = qkv[:, GLA_QK:2 * GLA_QK]
        incl.append(m)
        v.append(qkv[:, 2 * GLA_QK:2 * GLA_QK + GLA_V].astype(BF16))
        qb.append((q * jnp.exp(cs)).astype(BF16))
        kb.append((k * jnp.exp(-cs)).astype(BF16))
        kt.append((k * jnp.exp(tot - cs)).astype(BF16))
        gt.append(jnp.exp(tot))
    yield
    es = [(c, h) for c in range(len(streams)) for h in range(GLA_HEADS)]
    ks = lambda h: slice(h * GLA_DK, (h + 1) * GLA_DK)
    vs = lambda h: slice(h * GLA_DV, (h + 1) * GLA_DV)
    st = [st_scr[c, h] for c, h in es]
    att = [jnp.where(incl[c], _dot_nt(qb[c][:, ks(h)], kb[c][:, ks(h)]), 0.0) for c, h in es]
    o_state = [_dot_nt(qb[c][:, ks(h)], st[i]) for i, (c, h) in enumerate(es)]
    yield
    o_att = [_dot(att[i], v[c][:, vs(h)]) for i, (c, h) in enumerate(es)]
    upd = [_dot_tn(v[c][:, vs(h)], kt[c][:, ks(h)]) for c, h in es]
    yield
    for i, (c, h) in enumerate(es):
        bb, d = streams[c]
        (of_ref, ob_ref)[d][bb, :, vs(h)] = (o_att[i] + o_state[i]).astype(of_ref.dtype)
        st_scr[c, h] = st[i] * gt[c][:, ks(h)] + upd[i]


SCAN_BATCH = 4


def _scan_batch(B, want):
    return want if B % want == 0 else 1


def _scan_index_maps(S, n_ctx):
    nc, nc_ctx = S // CHUNK, n_ctx // CHUNK
    fwd = functools.partial(_chunk_index, nc_ctx=nc_ctx, nc=nc, reverse=False)
    bwd = functools.partial(_chunk_index, nc_ctx=nc_ctx, nc=nc, reverse=True)
    return nc, fwd, bwd


def _even_in_kernel(h_ref, hp_ref, hn_ref, wg_ref, wr_ref, mu_ref, w0_ref, wup_ref, a0_ref, aup_ref, gup_ref,
                    kk_ref, ka_ref, rk_ref, ones_ref, zg_o, r_o, lwf_o, lwb_o, k_o, v_o, kk_o, a_o, g_o, bonus_o,
                    *, tt, halo):
    j = pl.program_id(1)
    last = pl.num_programs(1) - 1
    h = h_ref[0]
    z_ext = jnp.dot(jnp.concatenate([hp_ref[0], h, hn_ref[0]], axis=0), wr_ref[...], preferred_element_type=F32)
    zg_o[0] = jnp.dot(h, wg_ref[...], preferred_element_type=F32)
    z = z_ext[halo:halo + tt]
    prev_row = jnp.where(j <= 1, 0.0, z_ext[halo - 1:halo])
    next_row = jnp.where((j == 0) | (j == last), 0.0, z_ext[halo + tt:halo + tt + 1])
    mu_p, mu_n = mu_ref[0:1, :], mu_ref[1:2, :]
    zs = z + mu_p * (pltpu.roll(z, 1, 0) - z) + mu_n * (pltpu.roll(z, tt - 1, 0) - z)
    r8 = lax.broadcasted_iota(jnp.int32, (8, 1), 0)
    head = zs[:8] + jnp.where(r8 == 0, mu_p * (prev_row - z[tt - 1:tt]), 0.0)
    tail = zs[tt - 8:] + jnp.where(r8 == 7, mu_n * (next_row - z[0:1]), 0.0)
    zs = jnp.concatenate([head, zs[8:tt - 8], tail], axis=0)

    W = RW_W
    r, k, v = zs[:, 0:W], zs[:, W:2 * W], zs[:, 2 * W:3 * W]
    base = 3 * W
    wd = (zs[:, base:base + LANE], zs[:, base + LANE:base + 2 * LANE])
    ad = zs[:, base + 2 * LANE:base + 3 * LANE]
    gd = zs[:, base + 3 * LANE:base + 4 * LANE]
    for d, lw_o in enumerate((lwf_o, lwb_o)):
        y = w0_ref[d:d + 1, :] + _dot(jnp.tanh(wd[d]), wup_ref[d])
        lw_o[0] = -math.exp(-0.5) * jax.nn.sigmoid(y)
    a = jax.nn.sigmoid(a0_ref[...] + _dot(ad, aup_ref[...]))
    g = _dot(jax.nn.sigmoid(gd), gup_ref[...])
    ones = ones_ref[...]
    kk = k * kk_ref[...]
    ss = _split_dot_r(kk * kk, ones, 2)
    kk = kk / jnp.maximum(jnp.sqrt(ss), 1e-12)
    kmod = k * (1.0 + (a - 1.0) * ka_ref[...])
    bonus = _split_dot_r(r * kmod * rk_ref[...], ones, 2) * v
    r_o[0] = r
    k_o[0] = kmod
    v_o[0] = v
    kk_o[0] = kk
    a_o[0] = a
    g_o[0] = g.astype(g_o.dtype)
    bonus_o[0] = bonus.astype(bonus_o.dtype)


def _even_in(h, p, n_ctx):
    B, S, D = h.shape
    tt = n_ctx
    halo = 16
    nt, nh = S // tt, S // halo
    W = RW_W
    full = lambda shape: pl.BlockSpec(shape, lambda b, j: (0,) * len(shape))
    row = lambda width: pl.BlockSpec((1, tt, width), lambda b, j: (b, j, 0))
    wg, wr = p["w_gla"], p["w_rw"]
    Z = wr.shape[1]
    return pl.pallas_call(
        functools.partial(_even_in_kernel, tt=tt, halo=halo),
        grid=(B, nt),
        in_specs=[row(D),
                  pl.BlockSpec((1, halo, D), lambda b, j: (b, jnp.maximum(j * (tt // halo) - 1, 0), 0)),
                  pl.BlockSpec((1, halo, D), lambda b, j: (b, jnp.minimum((j + 1) * (tt // halo), nh - 1), 0)),
                  _resident(wg.shape), _resident(wr.shape),
                  full((2, Z)), full((2, W)), full((2, LANE, W)), full((1, W)), full((LANE, W)), full((LANE, W)),
                  full((1, W)), full((1, W)), full((1, W)), full((W, W))],
        out_specs=[row(wg.shape[1])] + [row(W)] * 9,
        out_shape=[jax.ShapeDtypeStruct((B, S, wg.shape[1]), F32)] + [jax.ShapeDtypeStruct((B, S, W), F32)] * 7
                  + [jax.ShapeDtypeStruct((B, S, W), BF16)] * 2,
        compiler_params=_cp("parallel", "parallel"),
        name="even_in",
    )(h, h, h, wg, wr, p["mu"], p["w0"], p["wup"], p["a0"], p["aup"], p["gup"], p["k_k"], p["k_a"], p["r_k"],
      p["ones64"])


def _rw_stages(r_f, lw_f, k_f, v_f, kk_f, a_f, r_b, lw_b, k_b, v_b, kk_b, a_b, of_ref, ob_ref, s_scr):
    L = CHUNK
    nb = r_f.shape[0]
    ti, ji = _pair_ids(L)
    lane0 = lax.broadcasted_iota(jnp.int32, (1, 2 * L), 1) < L
    incl_w = (ji <= ti, ji >= ti)
    strict_w = (ji < ti, ji > ti)
    streams = [(bb, dr) for bb in range(nb) for dr in range(2)]
    at, rt, bh, kh, bl, kl, vb, g_tot = ([] for _ in range(8))
    for bb, dr in streams:
        r_ref, lw_ref, k_ref, v_ref, kk_ref, a_ref = ((r_f, lw_f, k_f, v_f, kk_f, a_f),
                                                      (r_b, lw_b, k_b, v_b, kk_b, a_b))[dr]
        m_incl, _ = _tri_masks(L, dr == 1)
        lw = lw_ref[bb]
        cs = _split_dot(m_incl.astype(BF16), lw, 3)
        tot = cs[0:1] if dr == 1 else cs[L - 1:L]
        kk, k = kk_ref[bb], k_ref[bb]
        bv = kk * a_ref[bb]
        g_inv = jnp.exp(-cs)
        g_rem = jnp.exp(tot - cs)
        g_tot.append(jnp.exp(tot))
        vb.append(v_ref[bb].astype(BF16))
        at.append((-(jnp.exp(cs - lw) * kk)).astype(BF16))
        rt.append((jnp.exp(cs) * r_ref[bb]).astype(BF16))
        bh.append((g_inv * bv).astype(BF16))
        kh.append((g_inv * k).astype(BF16))
        bl.append((g_rem * bv).astype(BF16))
        kl.append((g_rem * k).astype(BF16))
    es = [(d, p) for d in range(len(streams)) for p in range(RW_HEADS // 2)]
    ids = range(len(es))
    drs = [streams[d][1] for d, _ in es]
    ps = lambda p: slice(p * 2 * RW_DH, (p + 1) * 2 * RW_DH)
    bd = lambda x: _pair_blockdiag(x.astype(BF16), lane0)
    same_head = (lax.broadcasted_iota(jnp.int32, (2 * L, 2 * L), 0) // L) == (
        lax.broadcasted_iota(jnp.int32, (2 * L, 2 * L), 1) // L)
    yield
    s0 = [s_scr[d, p] for d, p in es]
    lhs = [jnp.concatenate([at[d][:, ps(p)], rt[d][:, ps(p)]], axis=0) for d, p in es]
    rhs = [jnp.concatenate([bd(bh[d][:, ps(p)]), bd(kh[d][:, ps(p)])], axis=0) for d, p in es]
    vbd = [bd(vb[d][:, ps(p)]) for d, p in es]
    sc = [_dot_nt(lhs[i], rhs[i]) for i in ids]
    zs = [_dot_nt(lhs[i], s0[i]) for i in ids]
    yield
    mab = [jnp.where(strict_w[drs[i]], sc[i][:L, :2 * L], 0.0) for i in ids]
    mak = [jnp.where(strict_w[drs[i]], sc[i][:L, 2 * L:], 0.0) for i in ids]
    nrb = [jnp.where(incl_w[drs[i]], sc[i][L:, :2 * L], 0.0) for i in ids]
    nrk = [jnp.where(incl_w[drs[i]], sc[i][L:, 2 * L:], 0.0) for i in ids]
    z0 = [zs[i][:L] + _dot(mak[i], vbd[i]) for i in ids]
    yield
    tinv = yield from _unit_tri_inverse(mab, (ti, ji), lane0)
    u = [_dot(tinv[i], bd(z0[i])) for i in ids]
    o1 = [_dot(nrk[i], vbd[i]) for i in ids]
    yield
    o2 = [_dot(nrb[i], bd(u[i])) for i in ids]
    upd = [_dot_tn(jnp.concatenate([u[i].astype(BF16), vb[d][:, ps(p)]], axis=0),
                   jnp.concatenate([bl[d][:, ps(p)], kl[d][:, ps(p)]], axis=0)) for i, (d, p) in enumerate(es)]
    yield
    for i, (d, p) in enumerate(es):
        bb, dr = streams[d]
        (of_ref, ob_ref)[dr][bb, :, ps(p)] = (zs[i][L:] + o1[i] + o2[i]).astype(of_ref.dtype)
        s_scr[d, p] = s0[i] * g_tot[d][:, ps(p)] + jnp.where(same_head, upd[i], 0.0)


def _even_scan_kernel(*refs):
    gla_in, rw_in = refs[0:6], refs[6:18]
    og_f, og_b, or_f, or_b, gla_state, rw_state = refs[18:]

    @pl.when(pl.program_id(1) == 0)
    def _():
        gla_state[...] = jnp.zeros_like(gla_state)
        rw_state[...] = jnp.zeros_like(rw_state)

    for _ in itertools.zip_longest(_rw_stages(*rw_in, or_f, or_b, rw_state),
                                   _gla_stages(*gla_in, og_f, og_b, gla_state)):
        pass


def _even_scans(z_gla, p, r, lwf, lwb, k, v, kk, a, n_ctx):
    B, S, W = r.shape
    nc, fwd, bwd = _scan_index_maps(S, n_ctx)
    nb = _scan_batch(B, SCAN_BATCH)
    ad_blk = (2 * GLA_QK + 2 * GLA_V) // (2 * LANE)
    qkv_spec = lambda cidx: pl.BlockSpec((nb, CHUNK, 2 * GLA_QK + GLA_V), lambda b, s: (b, cidx(s), 0))
    ad_spec = lambda cidx: pl.BlockSpec((nb, CHUNK, 2 * LANE), lambda b, s: (b, cidx(s), ad_blk))
    spec = lambda cidx, width: pl.BlockSpec((nb, CHUNK, width), lambda b, s: (b, cidx(s), 0))
    return pl.pallas_call(
        _even_scan_kernel,
        grid=(B // nb, nc),
        in_specs=[qkv_spec(fwd), ad_spec(fwd), qkv_spec(bwd), ad_spec(bwd),
                  pl.BlockSpec((2, LANE, GLA_QK), lambda b, s: (0, 0, 0)),
                  pl.BlockSpec((2, GLA_QK), lambda b, s: (0, 0))]
                 + [spec(fwd, W)] * 6 + [spec(bwd, W)] * 6,
        out_specs=[spec(fwd, GLA_V), spec(bwd, GLA_V), spec(fwd, W), spec(bwd, W)],
        out_shape=[jax.ShapeDtypeStruct((B, S, GLA_V), BF16)] * 2 + [jax.ShapeDtypeStruct((B, S, W), BF16)] * 2,
        scratch_shapes=[pltpu.VMEM((2 * nb, GLA_HEADS, GLA_DV, GLA_DK), F32),
                        pltpu.VMEM((2 * nb, RW_HEADS // 2, 2 * RW_DH, 2 * RW_DH), F32)],
        compiler_params=_cp("parallel", "arbitrary"),
        name="even_scans",
    )(z_gla, z_gla, z_gla, z_gla, p["gla_aup"], p["gla_bias"], r, lwf, k, v, kk, a, r, lwb, k, v, kk, a)


def _even_out_kernel(x_ref, ogf, ogb, gate_ref, orf, orb, bonus_ref, grw_ref, glag, lng, lnb, ones128, ones64,
                     w_ref, gl, gc, o_ref, *, tm, n_ctx):
    og = ogf[0].astype(F32) + ogb[0].astype(F32)
    ms = _split_dot_r(og * og, ones128[...], 1) * (1.0 / GLA_DV)
    gt = gate_ref[0]
    y_gla = og * lax.rsqrt(ms + 1e-6) * glag[...] * (gt * jax.nn.sigmoid(gt))
    of = orf[0].astype(F32) + orb[0].astype(F32)
    o64 = ones64[...]
    mean = _split_dot_r(of, o64, 2) * (1.0 / RW_DH)
    cen = of - mean
    var = _split_dot_r(cen * cen, o64, 1) * (1.0 / RW_DH)
    y_rw = (cen * lax.rsqrt(var + RW_GN_EPS) * lng[...] + lnb[...] + bonus_ref[0].astype(F32)) * grw_ref[0].astype(F32)
    y = jnp.concatenate([y_gla, y_rw], axis=1).astype(BF16)
    rows = pl.program_id(1) * tm + lax.broadcasted_iota(jnp.int32, (tm, 1), 0)
    gate = jnp.where(rows < n_ctx, gc[0], gl[0])
    o_ref[0] = x_ref[0] + gate * jnp.dot(y, w_ref[...], preferred_element_type=F32)


def _even_out(xs, og_f, og_b, z_gla, or_f, or_b, bonus, g_rw, p, mod, n_ctx, *, tm):
    B, S, D = xs.shape
    W = RW_W
    row = lambda width: pl.BlockSpec((1, tm, width), lambda b, j: (b, j, 0))
    full = lambda shape: pl.BlockSpec(shape, lambda b, j: (0,) * len(shape))
    return pl.pallas_call(
        functools.partial(_even_out_kernel, tm=tm, n_ctx=n_ctx),
        grid=(B, S // tm),
        in_specs=[row(D), row(GLA_V), row(GLA_V),
                  pl.BlockSpec((1, tm, GLA_V), lambda b, j: (b, j, (2 * GLA_QK + GLA_V) // GLA_V)),
                  row(W), row(W), row(W), row(W),
                  full((1, GLA_V)), full((1, W)), full((1, W)), full((GLA_V, GLA_V)), full((W, W)),
                  full((GLA_V + W, D)), *_mod_specs(B, D, 2, 2)],
        out_specs=row(D),
        out_shape=jax.ShapeDtypeStruct((B, S, D), F32),
        compiler_params=_cp("parallel", "parallel"),
        name="even_out",
    )(xs, og_f, og_b, z_gla, or_f, or_b, bonus, g_rw, p["gla_g"], p["ln_g"], p["ln_b"], p["ones128"], p["ones64"],
      p["w_out"], mod, mod)


def _odd_in_kernel(h_ref, w_ref, cos_ref, sin_ref, qg_ref, kg_ref, ones64, qc_o, kc_o, vc_o, qd_o, kd_o, vd_o):
    h = h_ref[0]
    cos, sin = cos_ref[...], sin_ref[...]
    o64 = ones64[...]
    nq, nk, nd = C_HEADS * HEAD_DIM, C_KV * HEAD_DIM, NA_HEADS * HEAD_DIM

    def piece(lo, width):
        return jnp.dot(h, w_ref[:, lo:lo + width], preferred_element_type=F32)

    def norm_rope(x, g, width):
        ms = _split_dot_r(x * x, o64[:width, :width], 1) * (1.0 / HEAD_DIM)
        y = x * lax.rsqrt(ms + 1e-6) * g
        lane = lax.broadcasted_iota(jnp.int32, (1, width), 1)
        first = (lane % (HEAD_DIM // 2)) < (HEAD_DIM // 4)
        swapped = jnp.where(first, pltpu.roll(y, width - HEAD_DIM // 4, 1), pltpu.roll(y, HEAD_DIM // 4, 1))
        return y * cos[:, :width] + swapped * sin[:, :width]

    qc_o[0] = (norm_rope(piece(0, nq), qg_ref[...], nq) * HEAD_DIM ** -0.5).astype(BF16)
    kc_o[0] = norm_rope(piece(nq, nk), kg_ref[:, :nk], nk).astype(BF16)
    vc_o[0] = piece(nq + nk, nk).astype(BF16)
    base = nq + 2 * nk
    qd_o[0] = (piece(base, nd) * HEAD_DIM ** -0.5).astype(BF16)
    kd_o[0] = piece(base + nd, nd).astype(BF16)
    vd_o[0] = piece(base + 2 * nd, nd).astype(BF16)


def _odd_in(h, w, cos_t, sin_t, qg, kg, ones64, *, tm):
    B, S, D = h.shape
    nq, nk, nd = C_HEADS * HEAD_DIM, C_KV * HEAD_DIM, NA_HEADS * HEAD_DIM
    full = lambda shape: pl.BlockSpec(shape, lambda b, j: (0,) * len(shape))
    row = lambda width: pl.BlockSpec((1, tm, width), lambda b, j: (b, j, 0))
    widths = (nq, nk, nk, nd, nd, nd)
    return pl.pallas_call(
        _odd_in_kernel,
        grid=(B, S // tm),
        in_specs=[row(D), _resident(w.shape),
                  pl.BlockSpec((tm, nq), lambda b, j: (j, 0)), pl.BlockSpec((tm, nq), lambda b, j: (j, 0)),
                  full((1, nq)), full((1, nq)), full((nq, nq))],
        out_specs=[row(wd) for wd in widths],
        out_shape=[jax.ShapeDtypeStruct((B, S, wd), BF16) for wd in widths],
        compiler_params=_cp("parallel", "parallel"),
        name="odd_in",
    )(h, w, cos_t, sin_t, qg, kg, ones64)


def _pair_attn_kernel(q_ref, k_ref, v_ref, o_ref, *, shared_kv):
    q = q_ref[0]
    tq = q.shape[0]
    ngroups = q.shape[1] // LANE
    lane = lax.broadcasted_iota(jnp.int32, (1, LANE), 1)
    masks = (lane < HEAD_DIM, lane >= HEAD_DIM)
    zero = jnp.zeros((), q.dtype)
    if shared_kv:
        kp, vp = k_ref[0], v_ref[0]
        es = [(g, hm) for g in range(ngroups) for hm in range(2)]
        ids = range(len(es))
        lhs = [jnp.where(masks[hm], q[:, g * LANE:(g + 1) * LANE], zero) for g, hm in es]
        s = [_dot_nt(lhs[i], kp) for i in ids]
        p = [jnp.exp(s[i] - jnp.max(s[i], axis=-1, keepdims=True)) for i in ids]
        den = [jnp.sum(p[i], axis=-1, keepdims=True) for i in ids]
        o = [_dot(p[i], vp) / den[i] for i in ids]
        for g in range(ngroups):
            o_ref[0, :, g * LANE:(g + 1) * LANE] = jnp.where(masks[0], o[2 * g], o[2 * g + 1]).astype(o_ref.dtype)
    else:
        for g in range(ngroups):
            qp = q[:, g * LANE:(g + 1) * LANE]
            kp = k_ref[0, :, g * LANE:(g + 1) * LANE]
            vp = v_ref[0, :, g * LANE:(g + 1) * LANE]
            lhs = jnp.concatenate([jnp.where(masks[0], qp, zero), jnp.where(masks[1], qp, zero)], axis=0)
            s = _dot_nt(lhs, kp)
            p = jnp.exp(s - jnp.max(s, axis=-1, keepdims=True))
            o = _dot(p, vp) / jnp.sum(p, axis=-1, keepdims=True)
            o_ref[0, :, g * LANE:(g + 1) * LANE] = jnp.where(masks[0], o[:tq], o[tq:]).astype(o_ref.dtype)


def _pair_attention(q, k, v, *, q_row0, n_q, n_k, tq, shared_kv, name):
    B, S, QW = q.shape
    KW = k.shape[2]
    return pl.pallas_call(
        functools.partial(_pair_attn_kernel, shared_kv=shared_kv),
        grid=(B, n_q // tq),
        in_specs=[pl.BlockSpec((1, tq, QW), lambda b, i: (b, q_row0 // tq + i, 0)),
                  pl.BlockSpec((1, n_k, KW), lambda b, i: (b, 0, 0)),
                  pl.BlockSpec((1, n_k, KW), lambda b, i: (b, 0, 0))],
        out_specs=pl.BlockSpec((1, tq, QW), lambda b, i: (b, i, 0)),
        out_shape=jax.ShapeDtypeStruct((B, n_q, QW), BF16),
        compiler_params=_cp("parallel", "parallel"),
        name=name,
    )(q, k, v)


NA_ROWS_PER_STEP = 2


def _na_kernel(q_ref, k_ref, v_ref, *rest, n_ctx, n_rows, kh, rps):
    bias_refs, o_ref = rest[:rps], rest[rps]
    nkeys = kh * GRID_W
    lane = lax.broadcasted_iota(jnp.int32, (1, LANE), 1)
    m0 = lane < HEAD_DIM
    zero = jnp.zeros((), q_ref.dtype)
    starts = []
    for j in range(rps):
        r = pl.program_id(1) * rps + j
        rs = jnp.clip(r - kh // 2, 0, n_rows - kh)
        starts.append(pl.multiple_of(n_ctx + rs * GRID_W, GRID_W))
    es = [(j, g) for j in range(rps) for g in range(NA_HEADS // 2)]
    ids = range(len(es))
    gs = lambda g: slice(g * LANE, (g + 1) * LANE)
    qrow = lambda j, g: q_ref[0, j * GRID_W:(j + 1) * GRID_W, gs(g)]
    lhs = [jnp.concatenate([jnp.where(m0, qrow(j, g), zero), jnp.where(m0, zero, qrow(j, g))], axis=0)
           for j, g in es]
    s_nb = [_dot_nt(lhs[i], k_ref[0, pl.ds(starts[j], nkeys), gs(g)]) for i, (j, g) in enumerate(es)]
    s_cx = [_dot_nt(lhs[i], k_ref[0, 0:n_ctx, gs(g)]) for i, (j, g) in enumerate(es)]
    s_nb = [s_nb[i] + jnp.concatenate([bias_refs[j][0, 2 * g], bias_refs[j][0, 2 * g + 1]], axis=0)
            for i, (j, g) in enumerate(es)]
    m = [jnp.maximum(jnp.max(s_nb[i], axis=-1, keepdims=True), jnp.max(s_cx[i], axis=-1, keepdims=True))
         for i in ids]
    p_nb = [jnp.exp(s_nb[i] - m[i]) for i in ids]
    p_cx = [jnp.exp(s_cx[i] - m[i]) for i in ids]
    den = [jnp.sum(p_nb[i], axis=-1, keepdims=True) + jnp.sum(p_cx[i], axis=-1, keepdims=True) for i in ids]
    o_nb = [_dot(p_nb[i], v_ref[0, pl.ds(starts[j], nkeys), gs(g)]) for i, (j, g) in enumerate(es)]
    o_cx = [_dot(p_cx[i], v_ref[0, 0:n_ctx, gs(g)]) for i, (j, g) in enumerate(es)]
    for i, (j, g) in enumerate(es):
        o = (o_nb[i] + o_cx[i]) / den[i]
        o_ref[0, j * GRID_W:(j + 1) * GRID_W, gs(g)] = jnp.where(m0, o[:GRID_W], o[GRID_W:]).astype(o_ref.dtype)


def _na_attention(qd, kd, vd, bias_tab, n_ctx):
    B, S, W = qd.shape
    T = S - n_ctx
    n_rows = T // GRID_W
    kh = min(NA_KH, n_rows)
    nkeys = kh * GRID_W
    rps = NA_ROWS_PER_STEP if n_rows % NA_ROWS_PER_STEP == 0 and n_ctx % (NA_ROWS_PER_STEP * GRID_W) == 0 else 1
    tq = rps * GRID_W

    def bias_spec(j):
        def idx(b, i):
            r = i * rps + j
            return (jnp.clip(r - kh // 2, 0, n_rows - kh) - r + kh - 1, 0, 0, 0)
        return pl.BlockSpec((1, NA_HEADS, GRID_W, nkeys), idx)

    return pl.pallas_call(
        functools.partial(_na_kernel, n_ctx=n_ctx, n_rows=n_rows, kh=kh, rps=rps),
        grid=(B, n_rows // rps),
        in_specs=[pl.BlockSpec((1, tq, W), lambda b, i: (b, n_ctx // tq + i, 0)),
                  pl.BlockSpec((1, S, W), lambda b, i: (b, 0, 0)),
                  pl.BlockSpec((1, S, W), lambda b, i: (b, 0, 0)),
                  *[bias_spec(j) for j in range(rps)]],
        out_specs=pl.BlockSpec((1, tq, W), lambda b, i: (b, i, 0)),
        out_shape=jax.ShapeDtypeStruct((B, T, W), BF16),
        compiler_params=_cp("parallel", "arbitrary"),
        name="na_attention",
    )(qd, kd, vd, *([bias_tab] * rps))


def _na_bias_table(rpb, n_rows):
    kh = min(NA_KH, n_rows)
    kw = min(NA_KW, GRID_W)
    col = np.arange(GRID_W)
    start = np.clip(col - kw // 2, 0, GRID_W - kw)
    in_win = (col[None, :] >= start[:, None]) & (col[None, :] < start[:, None] + kw)
    dc = np.clip(col[None, :] - col[:, None], -(NA_KW - 1), NA_KW - 1) + NA_KW - 1
    bias_cols = rpb[:, :, dc]
    slabs = []
    for st in range(kh):
        sl = bias_cols[:, st + NA_KH - kh:st + NA_KH - kh + kh]
        sl = jnp.where(in_win[None, None], sl, NEG).transpose(0, 2, 1, 3)
        slabs.append(sl.reshape(NA_HEADS, GRID_W, kh * GRID_W))
    return jnp.stack(slabs).astype(F32)


def _proj_res_kernel(x_ref, y_ref, w_ref, gl, gc, o_ref, *, tm, n_ctx):
    rows = pl.program_id(1) * tm + lax.broadcasted_iota(jnp.int32, (tm, 1), 0)
    gate = jnp.where(rows < n_ctx, gc[0], gl[0])
    o_ref[0] = x_ref[0] + gate * jnp.dot(y_ref[0], w_ref[...], preferred_element_type=F32)


def _proj_res(xs, y, w, mod, n_ctx, *, tm):
    B, S, D = xs.shape
    K = y.shape[2]
    row = lambda width: pl.BlockSpec((1, tm, width), lambda b, j: (b, j, 0))
    return pl.pallas_call(
        functools.partial(_proj_res_kernel, tm=tm, n_ctx=n_ctx),
        grid=(B, S // tm),
        in_specs=[row(D), row(K), pl.BlockSpec((K, D), lambda b, j: (0, 0)), *_mod_specs(B, D, 2, 2)],
        out_specs=row(D),
        out_shape=jax.ShapeDtypeStruct((B, S, D), F32),
        compiler_params=_cp("parallel", "parallel"),
        name="proj_res",
    )(xs, y, w, mod, mod)


def _final_norm_kernel(x_ref, g_ref, o_ref):
    x = x_ref[0]
    o_ref[0] = x * lax.rsqrt(jnp.mean(x * x, axis=-1, keepdims=True) + 1e-6) * g_ref[...]


def _final_norm(xs, g, n_ctx, *, tm):
    B, S, D = xs.shape
    T = S - n_ctx
    return pl.pallas_call(
        _final_norm_kernel,
        grid=(B, T // tm),
        in_specs=[pl.BlockSpec((1, tm, D), lambda b, j: (b, n_ctx // tm + j, 0)),
                  pl.BlockSpec((1, D), lambda b, j: (0, 0))],
        out_specs=pl.BlockSpec((1, tm, D), lambda b, j: (b, j, 0)),
        out_shape=jax.ShapeDtypeStruct((B, T, D), F32),
        compiler_params=_cp("parallel", "parallel"),
        name="final_norm",
    )(xs, g.reshape(1, D))


def _block_ones(width, block):
    idx = np.arange(width) // block
    return jnp.asarray(idx[:, None] == idx[None, :], BF16)


def _pad_cols(w, width):
    return jnp.pad(w, ((0, 0), (0, width - w.shape[1])))


def _pad_rows(w, height):
    return jnp.pad(w, ((0, height - w.shape[0]), (0, 0)))


def _even_params(j, ev_w_in, ev_w_out, gla_a_up, gla_a_bias, gla_norm_g, rw_mu, rw_w0, rw_w_up, rw_a0, rw_a_up,
                 rw_g_up, rw_k_k, rw_k_a, rw_r_k, rw_ln_g, rw_ln_b):
    w_in = ev_w_in[j]
    D = w_in.shape[0]
    gq = 2 * GLA_QK + 2 * GLA_V
    w_gla = jnp.concatenate([w_in[:, :gq], _pad_cols(w_in[:, gq:gq + GLA_RANK], LANE),
                             _pad_cols(w_in[:, gq + GLA_RANK:gq + 2 * GLA_RANK], LANE)], axis=1)
    o = gq + 2 * GLA_RANK
    sizes = (3 * RW_W, RW_DECAY_RANK, RW_DECAY_RANK, RW_A_RANK, RW_G_RANK)
    offs = np.cumsum((0,) + sizes)
    pieces = [w_in[:, o:o + 3 * RW_W]] + [_pad_cols(w_in[:, o + offs[i]:o + offs[i + 1]], LANE) for i in range(1, 5)]
    w_rw = jnp.concatenate(pieces, axis=1)
    mu = rw_mu[j]
    mu_pieces = [mu[:, :3 * RW_W]] + [_pad_cols(mu[:, offs[i]:offs[i + 1]], LANE) for i in range(1, 5)]
    return {
        "w_gla": w_gla.astype(BF16), "w_rw": w_rw.astype(BF16), "w_out": ev_w_out[j].astype(BF16),
        "gla_aup": jnp.stack([_pad_rows(gla_a_up[j, d], LANE) for d in range(2)]).astype(BF16),
        "gla_bias": gla_a_bias[j], "gla_g": jnp.tile(gla_norm_g[j], GLA_HEADS).reshape(1, GLA_V),
        "mu": jnp.concatenate(mu_pieces, axis=1), "w0": rw_w0[j],
        "wup": jnp.stack([_pad_rows(rw_w_up[j, d], LANE) for d in range(2)]).astype(BF16),
        "a0": rw_a0[j].reshape(1, RW_W), "aup": _pad_rows(rw_a_up[j], LANE).astype(BF16),
        "gup": _pad_rows(rw_g_up[j], LANE).astype(BF16),
        "k_k": rw_k_k[j].reshape(1, RW_W), "k_a": rw_k_a[j].reshape(1, RW_W), "r_k": rw_r_k[j].reshape(1, RW_W),
        "ln_g": rw_ln_g[j].reshape(1, RW_W), "ln_b": rw_ln_b[j].reshape(1, RW_W),
        "ones64": _block_ones(RW_W, RW_DH), "ones128": _block_ones(GLA_V, GLA_DV),
    }


def _gqa_head_order():
    per_kv = C_HEADS // C_KV
    order = []
    for jj in range(per_kv):
        for kv in range(C_KV):
            h = kv * per_kv + jj
            order.extend(range(h * HEAD_DIM, (h + 1) * HEAD_DIM))
    return np.asarray(order)


def _rope_tables(n_ctx, T, width):
    t = jnp.arange(T)
    pos = jnp.stack([t // GRID_W, t % GRID_W], axis=-1).astype(F32)
    half = HEAD_DIM // 2
    inv = ROPE_THETA ** (-jnp.arange(0, half, 2, dtype=F32) / half)
    ang = pos[:, :, None] * inv
    cos, sin = jnp.cos(ang), jnp.sin(ang)
    cos_h = jnp.concatenate([cos, cos], axis=-1).reshape(T, HEAD_DIM)
    sin_h = jnp.concatenate([-sin, sin], axis=-1).reshape(T, HEAD_DIM)
    cos_h = jnp.concatenate([jnp.ones((n_ctx, HEAD_DIM), F32), cos_h], axis=0)
    sin_h = jnp.concatenate([jnp.zeros((n_ctx, HEAD_DIM), F32), sin_h], axis=0)
    reps = width // HEAD_DIM
    return jnp.tile(cos_h, (1, reps)), jnp.tile(sin_h, (1, reps))


def _even_layer(xs, h, mod, n_ctx, p):
    z_gla, r, lwf, lwb, k, v, kk, a, g_rw, bonus = _even_in(h, p, n_ctx)
    og_f, og_b, or_f, or_b = _even_scans(z_gla, p, r, lwf, lwb, k, v, kk, a, n_ctx)
    return _even_out(xs, og_f, og_b, z_gla, or_f, or_b, bonus, g_rw, p, mod, n_ctx,
                     tm=_row_tile(xs.shape[1], ROW_CAP_FFN))


def _odd_layer(xs, h, mod, n_ctx, p, need_ctx):
    B, S, D = xs.shape
    T = S - n_ctx
    qc, kc, vc, qd, kd, vd = _odd_in(h, p["w_in"], p["cos"], p["sin"], p["qg"], p["kg"], p["ones64"],
                                     tm=_row_tile(S, ROW_CAP_FFN))
    y_gqa = _pair_attention(qc, kc, vc, q_row0=n_ctx, n_q=T, n_k=S, tq=256, shared_kv=True, name="gqa")
    y_na = _na_attention(qd, kd, vd, p["bias_tab"], n_ctx)
    parts_g, parts_n = [y_gqa], [y_na]
    if need_ctx:
        parts_g.insert(0, _pair_attention(qc, kc, vc, q_row0=0, n_q=n_ctx, n_k=n_ctx, tq=n_ctx, shared_kv=True,
                                          name="gqa_ctx"))
        parts_n.insert(0, _pair_attention(qd, kd, vd, q_row0=0, n_q=n_ctx, n_k=n_ctx, tq=n_ctx, shared_kv=False,
                                          name="na_ctx"))
    else:
        zeros = jnp.zeros((B, n_ctx, y_gqa.shape[2]), BF16)
        parts_g.insert(0, zeros)
        parts_n.insert(0, zeros)
    y = jnp.concatenate([jnp.concatenate(parts_g, axis=1), jnp.concatenate(parts_n, axis=1)], axis=2)
    return _proj_res(xs, y, p["w_out"], mod, n_ctx, tm=_row_tile(S, ROW_CAP_MATMUL))


def kernel(x, c, ctx, c_ctx, w_mod, b_mod, norm1_g, norm2_g, ffn_w13, ffn_w2, ev_w_in, ev_w_out, gla_a_up,
           gla_a_bias, gla_norm_g, rw_mu, rw_w0, rw_w_up, rw_a0, rw_a_up, rw_g_up, rw_k_k, rw_k_a, rw_r_k, rw_ln_g,
           rw_ln_b, od_w_in, od_w_out, cq_norm_g, ck_norm_g, na_rpb, final_g):
    B, T, D = x.shape
    n_ctx = ctx.shape[1]
    S = n_ctx + T
    xs = jnp.concatenate([ctx, x], axis=1)
    mods = _mod_vectors(c, c_ctx, w_mod, b_mod)
    nq, nk = C_HEADS * HEAD_DIM, C_KV * HEAD_DIM
    order = _gqa_head_order()
    cos_t, sin_t = _rope_tables(n_ctx, T, nq)
    ones64 = _block_ones(nq, HEAD_DIM)
    h = _norm_mod(xs, norm1_g[0], mods[0], n_ctx, tm=_row_tile(S, ROW_CAP_MATMUL))
    for i in range(DEPTH):
        j = i // 2
        mod = mods[i]
        if i % 2 == 0:
            p = _even_params(j, ev_w_in, ev_w_out, gla_a_up, gla_a_bias, gla_norm_g, rw_mu, rw_w0, rw_w_up, rw_a0,
                             rw_a_up, rw_g_up, rw_k_k, rw_k_a, rw_r_k, rw_ln_g, rw_ln_b)
            xs = _even_layer(xs, h, mod, n_ctx, p)
        else:
            w_in = od_w_in[j]
            w_out = od_w_out[j]
            p = {
                "w_in": jnp.concatenate([w_in[:, :nq][:, order], w_in[:, nq:]], axis=1).astype(BF16),
                "w_out": jnp.concatenate([w_out[:nq][order], w_out[nq:]], axis=0).astype(BF16),
                "cos": cos_t, "sin": sin_t, "ones64": ones64,
                "qg": jnp.tile(cq_norm_g[j], C_HEADS).reshape(1, nq),
                "kg": jnp.tile(ck_norm_g[j], C_HEADS).reshape(1, nq),
                "bias_tab": _na_bias_table(na_rpb[j], T // GRID_W),
            }
            xs = _odd_layer(xs, h, mod, n_ctx, p, need_ctx=i < DEPTH - 1)
        next_norm = (norm1_g[i + 1], mods[i + 1]) if i + 1 < DEPTH else None
        xs, h = _ffn(xs, norm2_g[i], mod, ffn_w13[i].astype(BF16), ffn_w2[i].astype(BF16), n_ctx,
                     tm=_row_tile(S, ROW_CAP_FFN), tf=FFN_COL_TILE, next_norm=next_norm)
    return _final_norm(xs, final_g, n_ctx, tm=_row_tile(n_ctx, ROW_CAP_ELEMENTWISE))
```

```python
import functools
import itertools
import math

import jax
import jax.numpy as jnp
import numpy as np
from jax import lax
from jax.experimental import pallas as pl
from jax.experimental.pallas import tpu as pltpu

F32 = jnp.float32
BF16 = jnp.bfloat16

DEPTH = 4
GRID_W = 64
CHUNK = 64
LANE = 128

GLA_HEADS, GLA_DK, GLA_DV, GLA_RANK, GLA_TAU = 4, 64, 128, 16, 16.0
GLA_QK, GLA_V = GLA_HEADS * GLA_DK, GLA_HEADS * GLA_DV
GLA_Z = 2 * GLA_QK + 2 * GLA_V + 2 * LANE

RW_HEADS, RW_DH, RW_W = 8, 64, 512
RW_DECAY_RANK, RW_A_RANK, RW_G_RANK = 32, 32, 96
RW_GN_EPS = 64e-5
RW_Z = 3 * RW_W + 4 * LANE

HEAD_DIM, C_HEADS, C_KV, NA_HEADS, NA_KH, NA_KW = 64, 8, 2, 8, 8, 16
ROPE_THETA = 10000.0
NEG = -1e30

VMEM_LIMIT = 56 * 1024 * 1024
ROW_CAP_MATMUL = 1152
ROW_CAP_FFN = 576
ROW_CAP_ELEMENTWISE = 256
FFN_COL_TILE = 256


def _row_tile(n_rows, cap):
    return max(d for d in range(8, cap + 1, 8) if n_rows % d == 0)


def _cp(*sem):
    return pltpu.CompilerParams(dimension_semantics=sem, vmem_limit_bytes=VMEM_LIMIT)


def _dot(a, b):
    return jnp.dot(a.astype(BF16), b.astype(BF16), preferred_element_type=F32)


def _dot_nt(a, b):
    return lax.dot_general(a.astype(BF16), b.astype(BF16), (((1,), (1,)), ((), ())), preferred_element_type=F32)


def _dot_tn(a, b):
    return lax.dot_general(a.astype(BF16), b.astype(BF16), (((0,), (0,)), ((), ())), preferred_element_type=F32)


def _split_dot(m, x, parts):
    acc = None
    rem = x
    for _ in range(parts):
        piece = rem.astype(BF16)
        rem = rem - piece.astype(F32)
        t = jnp.dot(m, piece, preferred_element_type=F32)
        acc = t if acc is None else acc + t
    return acc


def _split_dot_r(x, m, parts):
    acc = None
    rem = x
    for _ in range(parts):
        piece = rem.astype(BF16)
        rem = rem - piece.astype(F32)
        t = jnp.dot(piece, m, preferred_element_type=F32)
        acc = t if acc is None else acc + t
    return acc


def _rms_mod(x, g, sc_l, sh_l, sc_c, sh_c, row0, n_ctx):
    y = x * lax.rsqrt(jnp.mean(x * x, axis=-1, keepdims=True) + 1e-6) * g
    rows = row0 + lax.broadcasted_iota(jnp.int32, (x.shape[0], 1), 0)
    is_ctx = rows < n_ctx
    sc = jnp.where(is_ctx, sc_c, sc_l)
    sh = jnp.where(is_ctx, sh_c, sh_l)
    return y * (1.0 + sc) + sh


def _tri_masks(n, reverse):
    ri = lax.broadcasted_iota(jnp.int32, (n, n), 0)
    ci = lax.broadcasted_iota(jnp.int32, (n, n), 1)
    if reverse:
        return ci >= ri, ci > ri
    return ci <= ri, ci < ri


INV_BASE = 16


def _pair_ids(n):
    ti = lax.broadcasted_iota(jnp.int32, (n, 2 * n), 0)
    ji = lax.broadcasted_iota(jnp.int32, (n, 2 * n), 1) % n
    return ti, ji


def _pair_blockdiag(x, lane_is_head0):
    zero = jnp.zeros((), x.dtype)
    return jnp.concatenate([jnp.where(lane_is_head0, x, zero), jnp.where(lane_is_head0, zero, x)], axis=0)


def _unit_tri_inverse(a, ids_tj, lane_is_head0):
    ti, ji = ids_tj
    n = a[0].shape[0]
    ids = range(len(a))
    bd = lambda x: _pair_blockdiag(x.astype(BF16), lane_is_head0)
    same = lambda bs: (ti // bs) == (ji // bs)
    base = same(INV_BASE)
    eye = (ti == ji).astype(F32)
    p = [jnp.where(base, x, 0.0) for x in a]
    t = [eye + x for x in p]
    m = 1
    while 2 * m < INV_BASE:
        p = [_dot(x, bd(x)) for x in p]
        yield
        tp = [_dot(t[i], bd(p[i])) for i in ids]
        t = [t[i] + tp[i] for i in ids]
        yield
        m *= 2
    bs = INV_BASE
    while bs < n:
        mask = same(2 * bs) & jnp.logical_not(same(bs))
        ta = [_dot(t[i], bd(jnp.where(mask, a[i], 0.0))) for i in ids]
        yield
        tat = [_dot(ta[i], bd(t[i])) for i in ids]
        t = [t[i] + tat[i] for i in ids]
        yield
        bs *= 2
    return t


def _mod_kernel(s_ref, w_ref, b_ref, o_ref):
    s = s_ref[...]
    s = s * jax.nn.sigmoid(s)
    o_ref[0] = _dot(s, w_ref[0]) + b_ref[0]


def _mod_vectors(c, c_ctx, w_mod, b_mod):
    B, D = c.shape
    bp = -(-(B + 1) // 8) * 8
    s_in = jnp.zeros((bp, D), F32).at[:B].set(c).at[B].set(c_ctx)
    n6 = w_mod.shape[-1]
    tn = 1536
    out = pl.pallas_call(
        _mod_kernel,
        grid=(DEPTH, n6 // tn),
        in_specs=[pl.BlockSpec((bp, D), lambda l, n: (0, 0)),
                  pl.BlockSpec((1, D, tn), lambda l, n: (l, 0, n)),
                  pl.BlockSpec((1, 1, tn), lambda l, n: (l, 0, n))],
        out_specs=pl.BlockSpec((1, bp, tn), lambda l, n: (l, 0, n)),
        out_shape=jax.ShapeDtypeStruct((DEPTH, bp, n6), F32),
        compiler_params=_cp("parallel", "parallel"),
        name="mod_vectors",
    )(s_in, w_mod, b_mod.reshape(DEPTH, 1, n6))
    return out.reshape(DEPTH, bp, 1, n6)


def _mod_specs(B, D, idx, grid_rank):
    if grid_rank == 2:
        return [pl.BlockSpec((1, 1, D), lambda b, j: (b, 0, idx)),
                pl.BlockSpec((1, 1, D), lambda b, j: (B, 0, idx))]
    return [pl.BlockSpec((1, 1, D), lambda b, j, n: (b, 0, idx)),
            pl.BlockSpec((1, 1, D), lambda b, j, n: (B, 0, idx))]


def _norm_mod_kernel(x_ref, g_ref, shl, shc, scl, scc, o_ref, *, tm, n_ctx):
    h = _rms_mod(x_ref[0], g_ref[...], scl[0], shl[0], scc[0], shc[0], pl.program_id(1) * tm, n_ctx)
    o_ref[0] = h.astype(BF16)


def _norm_mod(xs, g, mod, n_ctx, *, tm):
    B, S, D = xs.shape
    row = pl.BlockSpec((1, tm, D), lambda b, j: (b, j, 0))
    return pl.pallas_call(
        functools.partial(_norm_mod_kernel, tm=tm, n_ctx=n_ctx),
        grid=(B, S // tm),
        in_specs=[row, pl.BlockSpec((1, D), lambda b, j: (0, 0)), *_mod_specs(B, D, 0, 2), *_mod_specs(B, D, 1, 2)],
        out_specs=row,
        out_shape=jax.ShapeDtypeStruct((B, S, D), BF16),
        compiler_params=_cp("parallel", "parallel"),
        name="norm_mod",
    )(xs, g.reshape(1, D), mod, mod, mod, mod)


def _ffn_kernel(x_ref, g_ref, shl, shc, scl, scc, gl, gc, w13_ref, w2_ref, *rest, tm, tf, n_ctx, emit_next):
    if emit_next:
        gn_ref, nshl, nshc, nscl, nscc, o_ref, hn_ref = rest
    else:
        (o_ref,) = rest
    x = x_ref[0]
    row0 = pl.program_id(1) * tm
    h = _rms_mod(x, g_ref[...], scl[0], shl[0], scc[0], shc[0], row0, n_ctx).astype(BF16)
    n_hidden = w2_ref.shape[0]
    acc = None
    for f in range(n_hidden // tf):
        a = jnp.dot(h, w13_ref[:, f * tf:(f + 1) * tf], preferred_element_type=F32)
        b = jnp.dot(h, w13_ref[:, n_hidden + f * tf:n_hidden + (f + 1) * tf], preferred_element_type=F32)
        u = (a * jax.nn.sigmoid(a) * b).astype(BF16)
        t = jnp.dot(u, w2_ref[f * tf:(f + 1) * tf, :], preferred_element_type=F32)
        acc = t if acc is None else acc + t
    rows = row0 + lax.broadcasted_iota(jnp.int32, (tm, 1), 0)
    gate = jnp.where(rows < n_ctx, gc[0], gl[0])
    y = x + gate * acc
    o_ref[0] = y
    if emit_next:
        hn_ref[0] = _rms_mod(y, gn_ref[...], nscl[0], nshl[0], nscc[0], nshc[0], row0, n_ctx).astype(BF16)


def _resident(shape):
    return pl.BlockSpec(shape, lambda b, j: (0,) * len(shape), pipeline_mode=pl.Buffered(1))


def _ffn(xs, g, mod, w13, w2, n_ctx, *, tm, tf, next_norm=None):
    B, S, D = xs.shape
    row = pl.BlockSpec((1, tm, D), lambda b, j: (b, j, 0))
    vec = pl.BlockSpec((1, D), lambda b, j: (0, 0))
    in_specs = [row, vec, *_mod_specs(B, D, 3, 2), *_mod_specs(B, D, 4, 2), *_mod_specs(B, D, 5, 2),
                _resident(w13.shape), _resident(w2.shape)]
    args = [xs, g.reshape(1, D), mod, mod, mod, mod, mod, mod, w13, w2]
    out_specs, out_shape = [row], [jax.ShapeDtypeStruct((B, S, D), F32)]
    if next_norm is not None:
        g_next, mod_next = next_norm
        in_specs += [vec, *_mod_specs(B, D, 0, 2), *_mod_specs(B, D, 1, 2)]
        args += [g_next.reshape(1, D), mod_next, mod_next, mod_next, mod_next]
        out_specs.append(row)
        out_shape.append(jax.ShapeDtypeStruct((B, S, D), BF16))
    out = pl.pallas_call(
        functools.partial(_ffn_kernel, tm=tm, tf=tf, n_ctx=n_ctx, emit_next=next_norm is not None),
        grid=(B, S // tm),
        in_specs=in_specs,
        out_specs=out_specs,
        out_shape=out_shape,
        compiler_params=_cp("parallel", "parallel"),
        name="ffn",
    )(*args)
    return (out[0], out[1]) if next_norm is not None else (out[0], None)


def _chunk_index(s, nc_ctx, nc, reverse):
    if not reverse:
        return s
    return jnp.where(s < nc_ctx, nc_ctx - 1 - s, nc + nc_ctx - 1 - s)


def _gla_stages(qkv_f, ad_f, qkv_b, ad_b, aup_ref, bias_ref, of_ref, ob_ref, st_scr):
    L = CHUNK
    nb = qkv_f.shape[0]
    streams = [(bb, d) for bb in range(nb) for d in range(2)]
    qb, kb, kt, gt, v, incl = [], [], [], [], [], []
    for bb, d in streams:
        qkv_ref, ad_ref = ((qkv_f, ad_f), (qkv_b, ad_b))[d]
        m, _ = _tri_masks(L, d == 1)
        ad = ad_ref[bb][:, d * LANE:(d + 1) * LANE]
        y = _dot(ad, aup_ref[d]) + bias_ref[d:d + 1, :]
        la = (jnp.minimum(y, 0.0) - jnp.log(1.0 + jnp.exp(-jnp.abs(y)))) * (1.0 / GLA_TAU)
        cs = _split_dot(m.astype(BF16), la, 3)
        tot = cs[0:1] if d == 1 else cs[L - 1:L]
        qkv = qkv_ref[bb]
        q = qkv[:, 0:GLA_QK] * GLA_DK ** -0.5
        k = qkv[:, GLA_QK:2 * GLA_QK]
        incl.append(m)
        v.append(qkv[:, 2 * GLA_QK:2 * GLA_QK + GLA_V].astype(BF16))
        qb.append((q * jnp.exp(cs)).astype(BF16))
        kb.append((k * jnp.exp(-cs)).astype(BF16))
        kt.append((k * jnp.exp(tot - cs)).astype(BF16))
        gt.append(jnp.exp(tot))
    yield
    es = [(c, h) for c in range(len(streams)) for h in range(GLA_HEADS)]
    ks = lambda h: slice(h * GLA_DK, (h + 1) * GLA_DK)
    vs = lambda h: slice(h * GLA_DV, (h + 1) * GLA_DV)
    st = [st_scr[c, h] for c, h in es]
    att = [jnp.where(incl[c], _dot_nt(qb[c][:, ks(h)], kb[c][:, ks(h)]), 0.0) for c, h in es]
    o_state = [_dot_nt(qb[c][:, ks(h)], st[i]) for i, (c, h) in enumerate(es)]
    yield
    o_att = [_dot(att[i], v[c][:, vs(h)]) for i, (c, h) in enumerate(es)]
    upd = [_dot_tn(v[c][:, vs(h)], kt[c][:, ks(h)]) for c, h in es]
    yield
    for i, (c, h) in enumerate(es):
        bb, d = streams[c]
        (of_ref, ob_ref)[d][bb, :, vs(h)] = (o_att[i] + o_state[i]).astype(of_ref.dtype)
        st_scr[c, h] = st[i] * gt[c][:, ks(h)] + upd[i]


SCAN_BATCH = 4


def _scan_batch(B, want):
    return want if B % want == 0 else 1


def _scan_index_maps(S, n_ctx):
    nc, nc_ctx = S // CHUNK, n_ctx // CHUNK
    fwd = functools.partial(_chunk_index, nc_ctx=nc_ctx, nc=nc, reverse=False)
    bwd = functools.partial(_chunk_index, nc_ctx=nc_ctx, nc=nc, reverse=True)
    return nc, fwd, bwd


def _even_in_kernel(h_ref, hp_ref, hn_ref, wg_ref, wr_ref, mu_ref, w0_ref, wup_ref, a0_ref, aup_ref, gup_ref,
                    kk_ref, ka_ref, rk_ref, ones_ref, zg_o, r_o, lwf_o, lwb_o, k_o, v_o, kk_o, a_o, g_o, bonus_o,
                    *, tt, halo):
    j = pl.program_id(1)
    last = pl.num_programs(1) - 1
    h = h_ref[0]
    z_ext = jnp.dot(jnp.concatenate([hp_ref[0], h, hn_ref[0]], axis=0), wr_ref[...], preferred_element_type=F32)
    zg_o[0] = jnp.dot(h, wg_ref[...], preferred_element_type=F32)
    z = z_ext[halo:halo + tt]
    prev_row = jnp.where(j <= 1, 0.0, z_ext[halo - 1:halo])
    next_row = jnp.where((j == 0) | (j == last), 0.0, z_ext[halo + tt:halo + tt + 1])
    mu_p, mu_n = mu_ref[0:1, :], mu_ref[1:2, :]
    zs = z + mu_p * (pltpu.roll(z, 1, 0) - z) + mu_n * (pltpu.roll(z, tt - 1, 0) - z)
    r8 = lax.broadcasted_iota(jnp.int32, (8, 1), 0)
    head = zs[:8] + jnp.where(r8 == 0, mu_p * (prev_row - z[tt - 1:tt]), 0.0)
    tail = zs[tt - 8:] + jnp.where(r8 == 7, mu_n * (next_row - z[0:1]), 0.0)
    zs = jnp.concatenate([head, zs[8:tt - 8], tail], axis=0)

    W = RW_W
    r, k, v = zs[:, 0:W], zs[:, W:2 * W], zs[:, 2 * W:3 * W]
    base = 3 * W
    wd = (zs[:, base:base + LANE], zs[:, base + LANE:base + 2 * LANE])
    ad = zs[:, base + 2 * LANE:base + 3 * LANE]
    gd = zs[:, base + 3 * LANE:base + 4 * LANE]
    for d, lw_o in enumerate((lwf_o, lwb_o)):
        y = w0_ref[d:d + 1, :] + _dot(jnp.tanh(wd[d]), wup_ref[d])
        lw_o[0] = -math.exp(-0.5) * jax.nn.sigmoid(y)
    a = jax.nn.sigmoid(a0_ref[...] + _dot(ad, aup_ref[...]))
    g = _dot(jax.nn.sigmoid(gd), gup_ref[...])
    ones = ones_ref[...]
    kk = k * kk_ref[...]
    ss = _split_dot_r(kk * kk, ones, 2)
    kk = kk / jnp.maximum(jnp.sqrt(ss), 1e-12)
    kmod = k * (1.0 + (a - 1.0) * ka_ref[...])
    bonus = _split_dot_r(r * kmod * rk_ref[...], ones, 2) * v
    r_o[0] = r
    k_o[0] = kmod
    v_o[0] = v
    kk_o[0] = kk
    a_o[0] = a
    g_o[0] = g.astype(g_o.dtype)
    bonus_o[0] = bonus.astype(bonus_o.dtype)


def _even_in(h, p, n_ctx):
    B, S, D = h.shape
    tt = n_ctx
    halo = 16
    nt, nh = S // tt, S // halo
    W = RW_W
    full = lambda shape: pl.BlockSpec(shape, lambda b, j: (0,) * len(shape))
    row = lambda width: pl.BlockSpec((1, tt, width), lambda b, j: (b, j, 0))
    wg, wr = p["w_gla"], p["w_rw"]
    Z = wr.shape[1]
    return pl.pallas_call(
        functools.partial(_even_in_kernel, tt=tt, halo=halo),
        grid=(B, nt),
        in_specs=[row(D),
                  pl.BlockSpec((1, halo, D), lambda b, j: (b, jnp.maximum(j * (tt // halo) - 1, 0), 0)),
                  pl.BlockSpec((1, halo, D), lambda b, j: (b, jnp.minimum((j + 1) * (tt // halo), nh - 1), 0)),
                  _resident(wg.shape), _resident(wr.shape),
                  full((2, Z)), full((2, W)), full((2, LANE, W)), full((1, W)), full((LANE, W)), full((LANE, W)),
                  full((1, W)), full((1, W)), full((1, W)), full((W, W))],
        out_specs=[row(wg.shape[1])] + [row(W)] * 9,
        out_shape=[jax.ShapeDtypeStruct((B, S, wg.shape[1]), F32)] + [jax.ShapeDtypeStruct((B, S, W), F32)] * 7
                  + [jax.ShapeDtypeStruct((B, S, W), BF16)] * 2,
        compiler_params=_cp("parallel", "parallel"),
        name="even_in",
    )(h, h, h, wg, wr, p["mu"], p["w0"], p["wup"], p["a0"], p["aup"], p["gup"], p["k_k"], p["k_a"], p["r_k"],
      p["ones64"])


def _rw_stages(r_f, lw_f, k_f, v_f, kk_f, a_f, r_b, lw_b, k_b, v_b, kk_b, a_b, of_ref, ob_ref, s_scr):
    L = CHUNK
    nb = r_f.shape[0]
    ti, ji = _pair_ids(L)
    lane0 = lax.broadcasted_iota(jnp.int32, (1, 2 * L), 1) < L
    incl_w = (ji <= ti, ji >= ti)
    strict_w = (ji < ti, ji > ti)
    streams = [(bb, dr) for bb in range(nb) for dr in range(2)]
    at, rt, bh, kh, bl, kl, vb, g_tot = ([] for _ in range(8))
    for bb, dr in streams:
        r_ref, lw_ref, k_ref, v_ref, kk_ref, a_ref = ((r_f, lw_f, k_f, v_f, kk_f, a_f),
                                                      (r_b, lw_b, k_b, v_b, kk_b, a_b))[dr]
        m_incl, _ = _tri_masks(L, dr == 1)
        lw = lw_ref[bb]
        cs = _split_dot(m_incl.astype(BF16), lw, 3)
        tot = cs[0:1] if dr == 1 else cs[L - 1:L]
        kk, k = kk_ref[bb], k_ref[bb]
        bv = kk * a_ref[bb]
        g_inv = jnp.exp(-cs)
        g_rem = jnp.exp(tot - cs)
        g_tot.append(jnp.exp(tot))
        vb.append(v_ref[bb].astype(BF16))
        at.append((-(jnp.exp(cs - lw) * kk)).astype(BF16))
        rt.append((jnp.exp(cs) * r_ref[bb]).astype(BF16))
        bh.append((g_inv * bv).astype(BF16))
        kh.append((g_inv * k).astype(BF16))
        bl.append((g_rem * bv).astype(BF16))
        kl.append((g_rem * k).astype(BF16))
    es = [(d, p) for d in range(len(streams)) for p in range(RW_HEADS // 2)]
    ids = range(len(es))
    drs = [streams[d][1] for d, _ in es]
    ps = lambda p: slice(p * 2 * RW_DH, (p + 1) * 2 * RW_DH)
    bd = lambda x: _pair_blockdiag(x.astype(BF16), lane0)
    same_head = (lax.broadcasted_iota(jnp.int32, (2 * L, 2 * L), 0) // L) == (
        lax.broadcasted_iota(jnp.int32, (2 * L, 2 * L), 1) // L)
    yield
    s0 = [s_scr[d, p] for d, p in es]
    lhs = [jnp.concatenate([at[d][:, ps(p)], rt[d][:, ps(p)]], axis=0) for d, p in es]
    rhs = [jnp.concatenate([bd(bh[d][:, ps(p)]), bd(kh[d][:, ps(p)])], axis=0) for d, p in es]
    vbd = [bd(vb[d][:, ps(p)]) for d, p in es]
    sc = [_dot_nt(lhs[i], rhs[i]) for i in ids]
    zs = [_dot_nt(lhs[i], s0[i]) for i in ids]
    yield
    mab = [jnp.where(strict_w[drs[i]], sc[i][:L, :2 * L], 0.0) for i in ids]
    mak = [jnp.where(strict_w[drs[i]], sc[i][:L, 2 * L:], 0.0) for i in ids]
    nrb = [jnp.where(incl_w[drs[i]], sc[i][L:, :2 * L], 0.0) for i in ids]
    nrk = [jnp.where(incl_w[drs[i]], sc[i][L:, 2 * L:], 0.0) for i in ids]
    z0 = [zs[i][:L] + _dot(mak[i], vbd[i]) for i in ids]
    yield
    tinv = yield from _unit_tri_inverse(mab, (ti, ji), lane0)
    u = [_dot(tinv[i], bd(z0[i])) for i in ids]
    o1 = [_dot(nrk[i], vbd[i]) for i in ids]
    yield
    o2 = [_dot(nrb[i], bd(u[i])) for i in ids]
    upd = [_dot_tn(jnp.concatenate([u[i].astype(BF16), vb[d][:, ps(p)]], axis=0),
                   jnp.concatenate([bl[d][:, ps(p)], kl[d][:, ps(p)]], axis=0)) for i, (d, p) in enumerate(es)]
    yield
    for i, (d, p) in enumerate(es):
        bb, dr = streams[d]
        (of_ref, ob_ref)[dr][bb, :, ps(p)] = (zs[i][L:] + o1[i] + o2[i]).astype(of_ref.dtype)
        s_scr[d, p] = s0[i] * g_tot[d][:, ps(p)] + jnp.where(same_head, upd[i], 0.0)


def _even_scan_kernel(*refs):
    gla_in, rw_in = refs[0:6], refs[6:18]
    og_f, og_b, or_f, or_b, gla_state, rw_state = refs[18:]

    @pl.when(pl.program_id(1) == 0)
    def _():
        gla_state[...] = jnp.zeros_like(gla_state)
        rw_state[...] = jnp.zeros_like(rw_state)

    for _ in itertools.zip_longest(_rw_stages(*rw_in, or_f, or_b, rw_state),
                                   _gla_stages(*gla_in, og_f, og_b, gla_state)):
        pass


def _even_scans(z_gla, p, r, lwf, lwb, k, v, kk, a, n_ctx):
    B, S, W = r.shape
    nc, fwd, bwd = _scan_index_maps(S, n_ctx)
    nb = _scan_batch(B, SCAN_BATCH)
    ad_blk = (2 * GLA_QK + 2 * GLA_V) // (2 * LANE)
    qkv_spec = lambda cidx: pl.BlockSpec((nb, CHUNK, 2 * GLA_QK + GLA_V), lambda b, s: (b, cidx(s), 0))
    ad_spec = lambda cidx: pl.BlockSpec((nb, CHUNK, 2 * LANE), lambda b, s: (b, cidx(s), ad_blk))
    spec = lambda cidx, width: pl.BlockSpec((nb, CHUNK, width), lambda b, s: (b, cidx(s), 0))
    return pl.pallas_call(
        _even_scan_kernel,
        grid=(B // nb, nc),
        in_specs=[qkv_spec(fwd), ad_spec(fwd), qkv_spec(bwd), ad_spec(bwd),
                  pl.BlockSpec((2, LANE, GLA_QK), lambda b, s: (0, 0, 0)),
                  pl.BlockSpec((2, GLA_QK), lambda b, s: (0, 0))]
                 + [spec(fwd, W)] * 6 + [spec(bwd, W)] * 6,
        out_specs=[spec(fwd, GLA_V), spec(bwd, GLA_V), spec(fwd, W), spec(bwd, W)],
        out_shape=[jax.ShapeDtypeStruct((B, S, GLA_V), BF16)] * 2 + [jax.ShapeDtypeStruct((B, S, W), BF16)] * 2,
        scratch_shapes=[pltpu.VMEM((2 * nb, GLA_HEADS, GLA_DV, GLA_DK), F32),
                        pltpu.VMEM((2 * nb, RW_HEADS // 2, 2 * RW_DH, 2 * RW_DH), F32)],
        compiler_params=_cp("parallel", "arbitrary"),
        name="even_scans",
    )(z_gla, z_gla, z_gla, z_gla, p["gla_aup"], p["gla_bias"], r, lwf, k, v, kk, a, r, lwb, k, v, kk, a)


def _even_out_kernel(x_ref, ogf, ogb, gate_ref, orf, orb, bonus_ref, grw_ref, glag, lng, lnb, ones128, ones64,
                     w_ref, gl, gc, o_ref, *, tm, n_ctx):
    og = ogf[0].astype(F32) + ogb[0].astype(F32)
    ms = _split_dot_r(og * og, ones128[...], 1) * (1.0 / GLA_DV)
    gt = gate_ref[0]
    y_gla = og * lax.rsqrt(ms + 1e-6) * glag[...] * (gt * jax.nn.sigmoid(gt))
    of = orf[0].astype(F32) + orb[0].astype(F32)
    o64 = ones64[...]
    mean = _split_dot_r(of, o64, 2) * (1.0 / RW_DH)
    cen = of - mean
    var = _split_dot_r(cen * cen, o64, 1) * (1.0 / RW_DH)
    y_rw = (cen * lax.rsqrt(var + RW_GN_EPS) * lng[...] + lnb[...] + bonus_ref[0].astype(F32)) * grw_ref[0].astype(F32)
    y = jnp.concatenate([y_gla, y_rw], axis=1).astype(BF16)
    rows = pl.program_id(1) * tm + lax.broadcasted_iota(jnp.int32, (tm, 1), 0)
    gate = jnp.where(rows < n_ctx, gc[0], gl[0])
    o_ref[0] = x_ref[0] + gate * jnp.dot(y, w_ref[...], preferred_element_type=F32)


def _even_out(xs, og_f, og_b, z_gla, or_f, or_b, bonus, g_rw, p, mod, n_ctx, *, tm):
    B, S, D = xs.shape
    W = RW_W
    row = lambda width: pl.BlockSpec((1, tm, width), lambda b, j: (b, j, 0))
    full = lambda shape: pl.BlockSpec(shape, lambda b, j: (0,) * len(shape))
    return pl.pallas_call(
        functools.partial(_even_out_kernel, tm=tm, n_ctx=n_ctx),
        grid=(B, S // tm),
        in_specs=[row(D), row(GLA_V), row(GLA_V),
                  pl.BlockSpec((1, tm, GLA_V), lambda b, j: (b, j, (2 * GLA_QK + GLA_V) // GLA_V)),
                  row(W), row(W), row(W), row(W),
                  full((1, GLA_V)), full((1, W)), full((1, W)), full((GLA_V, GLA_V)), full((W, W)),
                  full((GLA_V + W, D)), *_mod_specs(B, D, 2, 2)],
        out_specs=row(D),
        out_shape=jax.ShapeDtypeStruct((B, S, D), F32),
        compiler_params=_cp("parallel", "parallel"),
        name="even_out",
    )(xs, og_f, og_b, z_gla, or_f, or_b, bonus, g_rw, p["gla_g"], p["ln_g"], p["ln_b"], p["ones128"], p["ones64"],
      p["w_out"], mod, mod)


def _odd_in_kernel(h_ref, w_ref, cos_ref, sin_ref, qg_ref, kg_ref, ones64, qc_o, kc_o, vc_o, qd_o, kd_o, vd_o):
    h = h_ref[0]
    cos, sin = cos_ref[...], sin_ref[...]
    o64 = ones64[...]
    nq, nk, nd = C_HEADS * HEAD_DIM, C_KV * HEAD_DIM, NA_HEADS * HEAD_DIM

    def piece(lo, width):
        return jnp.dot(h, w_ref[:, lo:lo + width], preferred_element_type=F32)

    def norm_rope(x, g, width):
        ms = _split_dot_r(x * x, o64[:width, :width], 1) * (1.0 / HEAD_DIM)
        y = x * lax.rsqrt(ms + 1e-6) * g
        lane = lax.broadcasted_iota(jnp.int32, (1, width), 1)
        first = (lane % (HEAD_DIM // 2)) < (HEAD_DIM // 4)
        swapped = jnp.where(first, pltpu.roll(y, width - HEAD_DIM // 4, 1), pltpu.roll(y, HEAD_DIM // 4, 1))
        return y * cos[:, :width] + swapped * sin[:, :width]

    qc_o[0] = (norm_rope(piece(0, nq), qg_ref[...], nq) * HEAD_DIM ** -0.5).astype(BF16)
    kc_o[0] = norm_rope(piece(nq, nk), kg_ref[:, :nk], nk).astype(BF16)
    vc_o[0] = piece(nq + nk, nk).astype(BF16)
    base = nq + 2 * nk
    qd_o[0] = (piece(base, nd) * HEAD_DIM ** -0.5).astype(BF16)
    kd_o[0] = piece(base + nd, nd).astype(BF16)
    vd_o[0] = piece(base + 2 * nd, nd).astype(BF16)


def _odd_in(h, w, cos_t, sin_t, qg, kg, ones64, *, tm):
    B, S, D = h.shape
    nq, nk, nd = C_HEADS * HEAD_DIM, C_KV * HEAD_DIM, NA_HEADS * HEAD_DIM
    full = lambda shape: pl.BlockSpec(shape, lambda b, j: (0,) * len(shape))
    row = lambda width: pl.BlockSpec((1, tm, width), lambda b, j: (b, j, 0))
    widths = (nq, nk, nk, nd, nd, nd)
    return pl.pallas_call(
        _odd_in_kernel,
        grid=(B, S // tm),
        in_specs=[row(D), _resident(w.shape),
                  pl.BlockSpec((tm, nq), lambda b, j: (j, 0)), pl.BlockSpec((tm, nq), lambda b, j: (j, 0)),
                  full((1, nq)), full((1, nq)), full((nq, nq))],
        out_specs=[row(wd) for wd in widths],
        out_shape=[jax.ShapeDtypeStruct((B, S, wd), BF16) for wd in widths],
        compiler_params=_cp("parallel", "parallel"),
        name="odd_in",
    )(h, w, cos_t, sin_t, qg, kg, ones64)


def _pair_attn_kernel(q_ref, k_ref, v_ref, o_ref, *, shared_kv):
    q = q_ref[0]
    tq = q.shape[0]
    ngroups = q.shape[1] // LANE
    lane = lax.broadcasted_iota(jnp.int32, (1, LANE), 1)
    masks = (lane < HEAD_DIM, lane >= HEAD_DIM)
    zero = jnp.zeros((), q.dtype)
    if shared_kv:
        kp, vp = k_ref[0], v_ref[0]
        es = [(g, hm) for g in range(ngroups) for hm in range(2)]
        ids = range(len(es))
        lhs = [jnp.where(masks[hm], q[:, g * LANE:(g + 1) * LANE], zero) for g, hm in es]
        s = [_dot_nt(lhs[i], kp) for i in ids]
        p = [jnp.exp(s[i] - jnp.max(s[i], axis=-1, keepdims=True)) for i in ids]
        den = [jnp.sum(p[i], axis=-1, keepdims=True) for i in ids]
        o = [_dot(p[i], vp) / den[i] for i in ids]
        for g in range(ngroups):
            o_ref[0, :, g * LANE:(g + 1) * LANE] = jnp.where(masks[0], o[2 * g], o[2 * g + 1]).astype(o_ref.dtype)
    else:
        for g in range(ngroups):
            qp = q[:, g * LANE:(g + 1) * LANE]
            kp = k_ref[0, :, g * LANE:(g + 1) * LANE]
            vp = v_ref[0, :, g * LANE:(g + 1) * LANE]
            lhs = jnp.concatenate([jnp.where(masks[0], qp, zero), jnp.where(masks[1], qp, zero)], axis=0)
            s = _dot_nt(lhs, kp)
            p = jnp.exp(s - jnp.max(s, axis=-1, keepdims=True))
            o = _dot(p, vp) / jnp.sum(p, axis=-1, keepdims=True)
            o_ref[0, :, g * LANE:(g + 1) * LANE] = jnp.where(masks[0], o[:tq], o[tq:]).astype(o_ref.dtype)


def _pair_attention(q, k, v, *, q_row0, n_q, n_k, tq, shared_kv, name):
    B, S, QW = q.shape
    KW = k.shape[2]
    return pl.pallas_call(
        functools.partial(_pair_attn_kernel, shared_kv=shared_kv),
        grid=(B, n_q // tq),
        in_specs=[pl.BlockSpec((1, tq, QW), lambda b, i: (b, q_row0 // tq + i, 0)),
                  pl.BlockSpec((1, n_k, KW), lambda b, i: (b, 0, 0)),
                  pl.BlockSpec((1, n_k, KW), lambda b, i: (b, 0, 0))],
        out_specs=pl.BlockSpec((1, tq, QW), lambda b, i: (b, i, 0)),
        out_shape=jax.ShapeDtypeStruct((B, n_q, QW), BF16),
        compiler_params=_cp("parallel", "parallel"),
        name=name,
    )(q, k, v)


NA_ROWS_PER_STEP = 4


def _na_kernel(q_ref, k_ref, v_ref, *rest, n_ctx, n_rows, kh, rps):
    bias_refs, o_ref = rest[:rps], rest[rps]
    nkeys = kh * GRID_W
    lane = lax.broadcasted_iota(jnp.int32, (1, LANE), 1)
    m0 = lane < HEAD_DIM
    zero = jnp.zeros((), q_ref.dtype)
    starts = []
    for j in range(rps):
        r = pl.program_id(1) * rps + j
        rs = jnp.clip(r - kh // 2, 0, n_rows - kh)
        starts.append(pl.multiple_of(n_ctx + rs * GRID_W, GRID_W))
    es = [(j, g) for j in range(rps) for g in range(NA_HEADS // 2)]
    ids = range(len(es))
    gs = lambda g: slice(g * LANE, (g + 1) * LANE)
    qrow = lambda j, g: q_ref[0, j * GRID_W:(j + 1) * GRID_W, gs(g)]
    lhs = [jnp.concatenate([jnp.where(m0, qrow(j, g), zero), jnp.where(m0, zero, qrow(j, g))], axis=0)
           for j, g in es]
    s_nb = [_dot_nt(lhs[i], k_ref[0, pl.ds(starts[j], nkeys), gs(g)]) for i, (j, g) in enumerate(es)]
    s_cx = [_dot_nt(lhs[i], k_ref[0, 0:n_ctx, gs(g)]) for i, (j, g) in enumerate(es)]
    s_nb = [s_nb[i] + jnp.concatenate([bias_refs[j][0, 2 * g], bias_refs[j][0, 2 * g + 1]], axis=0)
            for i, (j, g) in enumerate(es)]
    m = [jnp.maximum(jnp.max(s_nb[i], axis=-1, keepdims=True), jnp.max(s_cx[i], axis=-1, keepdims=True))
         for i in ids]
    p_nb = [jnp.exp(s_nb[i] - m[i]) for i in ids]
    p_cx = [jnp.exp(s_cx[i] - m[i]) for i in ids]
    den = [jnp.sum(p_nb[i], axis=-1, keepdims=True) + jnp.sum(p_cx[i], axis=-1, keepdims=True) for i in ids]
    o_nb = [_dot(p_nb[i], v_ref[0, pl.ds(starts[j], nkeys), gs(g)]) for i, (j, g) in enumerate(es)]
    o_cx = [_dot(p_cx[i], v_ref[0, 0:n_ctx, gs(g)]) for i, (j, g) in enumerate(es)]
    for i, (j, g) in enumerate(es):
        o = (o_nb[i] + o_cx[i]) / den[i]
        o_ref[0, j * GRID_W:(j + 1) * GRID_W, gs(g)] = jnp.where(m0, o[:GRID_W], o[GRID_W:]).astype(o_ref.dtype)


def _na_attention(qd, kd, vd, bias_tab, n_ctx):
    B, S, W = qd.shape
    T = S - n_ctx
    n_rows = T // GRID_W
    kh = min(NA_KH, n_rows)
    nkeys = kh * GRID_W
    rps = NA_ROWS_PER_STEP if n_rows % NA_ROWS_PER_STEP == 0 and n_ctx % (NA_ROWS_PER_STEP * GRID_W) == 0 else 1
    tq = rps * GRID_W

    def bias_spec(j):
        def idx(b, i):
            r = i * rps + j
            return (jnp.clip(r - kh // 2, 0, n_rows - kh) - r + kh - 1, 0, 0, 0)
        return pl.BlockSpec((1, NA_HEADS, GRID_W, nkeys), idx)

    return pl.pallas_call(
        functools.partial(_na_kernel, n_ctx=n_ctx, n_rows=n_rows, kh=kh, rps=rps),
        grid=(B, n_rows // rps),
        in_specs=[pl.BlockSpec((1, tq, W), lambda b, i: (b, n_ctx // tq + i, 0)),
                  pl.BlockSpec((1, S, W), lambda b, i: (b, 0, 0)),
                  pl.BlockSpec((1, S, W), lambda b, i: (b, 0, 0)),
                  *[bias_spec(j) for j in range(rps)]],
        out_specs=pl.BlockSpec((1, tq, W), lambda b, i: (b, i, 0)),
        out_shape=jax.ShapeDtypeStruct((B, T, W), BF16),
        compiler_params=_cp("parallel", "arbitrary"),
        name="na_attention",
    )(qd, kd, vd, *([bias_tab] * rps))


def _na_bias_table(rpb, n_rows):
    kh = min(NA_KH, n_rows)
    kw = min(NA_KW, GRID_W)
    col = np.arange(GRID_W)
    start = np.clip(col - kw // 2, 0, GRID_W - kw)
    in_win = (col[None, :] >= start[:, None]) & (col[None, :] < start[:, None] + kw)
    dc = np.clip(col[None, :] - col[:, None], -(NA_KW - 1), NA_KW - 1) + NA_KW - 1
    bias_cols = rpb[:, :, dc]
    slabs = []
    for st in range(kh):
        sl = bias_cols[:, st + NA_KH - kh:st + NA_KH - kh + kh]
        sl = jnp.where(in_win[None, None], sl, NEG).transpose(0, 2, 1, 3)
        slabs.append(sl.reshape(NA_HEADS, GRID_W, kh * GRID_W))
    return jnp.stack(slabs).astype(F32)


def _proj_res_kernel(x_ref, y_ref, w_ref, gl, gc, o_ref, *, tm, n_ctx):
    rows = pl.program_id(1) * tm + lax.broadcasted_iota(jnp.int32, (tm, 1), 0)
    gate = jnp.where(rows < n_ctx, gc[0], gl[0])
    o_ref[0] = x_ref[0] + gate * jnp.dot(y_ref[0], w_ref[...], preferred_element_type=F32)


def _proj_res(xs, y, w, mod, n_ctx, *, tm):
    B, S, D = xs.shape
    K = y.shape[2]
    row = lambda width: pl.BlockSpec((1, tm, width), lambda b, j: (b, j, 0))
    return pl.pallas_call(
        functools.partial(_proj_res_kernel, tm=tm, n_ctx=n_ctx),
        grid=(B, S // tm),
        in_specs=[row(D), row(K), pl.BlockSpec((K, D), lambda b, j: (0, 0)), *_mod_specs(B, D, 2, 2)],
        out_specs=row(D),
        out_shape=jax.ShapeDtypeStruct((B, S, D), F32),
        compiler_params=_cp("parallel", "parallel"),
        name="proj_res",
    )(xs, y, w, mod, mod)


def _final_norm_kernel(x_ref, g_ref, o_ref):
    x = x_ref[0]
    o_ref[0] = x * lax.rsqrt(jnp.mean(x * x, axis=-1, keepdims=True) + 1e-6) * g_ref[...]


def _final_norm(xs, g, n_ctx, *, tm):
    B, S, D = xs.shape
    T = S - n_ctx
    return pl.pallas_call(
        _final_norm_kernel,
        grid=(B, T // tm),
        in_specs=[pl.BlockSpec((1, tm, D), lambda b, j: (b, n_ctx // tm + j, 0)),
                  pl.BlockSpec((1, D), lambda b, j: (0, 0))],
        out_specs=pl.BlockSpec((1, tm, D), lambda b, j: (b, j, 0)),
        out_shape=jax.ShapeDtypeStruct((B, T, D), F32),
        compiler_params=_cp("parallel", "parallel"),
        name="final_norm",
    )(xs, g.reshape(1, D))


def _block_ones(width, block):
    idx = np.arange(width) // block
    return jnp.asarray(idx[:, None] == idx[None, :], BF16)


def _pad_cols(w, width):
    return jnp.pad(w, ((0, 0), (0, width - w.shape[1])))


def _pad_rows(w, height):
    return jnp.pad(w, ((0, height - w.shape[0]), (0, 0)))


def _even_params(j, ev_w_in, ev_w_out, gla_a_up, gla_a_bias, gla_norm_g, rw_mu, rw_w0, rw_w_up, rw_a0, rw_a_up,
                 rw_g_up, rw_k_k, rw_k_a, rw_r_k, rw_ln_g, rw_ln_b):
    w_in = ev_w_in[j]
    D = w_in.shape[0]
    gq = 2 * GLA_QK + 2 * GLA_V
    w_gla = jnp.concatenate([w_in[:, :gq], _pad_cols(w_in[:, gq:gq + GLA_RANK], LANE),
                             _pad_cols(w_in[:, gq + GLA_RANK:gq + 2 * GLA_RANK], LANE)], axis=1)
    o = gq + 2 * GLA_RANK
    sizes = (3 * RW_W, RW_DECAY_RANK, RW_DECAY_RANK, RW_A_RANK, RW_G_RANK)
    offs = np.cumsum((0,) + sizes)
    pieces = [w_in[:, o:o + 3 * RW_W]] + [_pad_cols(w_in[:, o + offs[i]:o + offs[i + 1]], LANE) for i in range(1, 5)]
    w_rw = jnp.concatenate(pieces, axis=1)
    mu = rw_mu[j]
    mu_pieces = [mu[:, :3 * RW_W]] + [_pad_cols(mu[:, offs[i]:offs[i + 1]], LANE) for i in range(1, 5)]
    return {
        "w_gla": w_gla.astype(BF16), "w_rw": w_rw.astype(BF16), "w_out": ev_w_out[j].astype(BF16),
        "gla_aup": jnp.stack([_pad_rows(gla_a_up[j, d], LANE) for d in range(2)]).astype(BF16),
        "gla_bias": gla_a_bias[j], "gla_g": jnp.tile(gla_norm_g[j], GLA_HEADS).reshape(1, GLA_V),
        "mu": jnp.concatenate(mu_pieces, axis=1), "w0": rw_w0[j],
        "wup": jnp.stack([_pad_rows(rw_w_up[j, d], LANE) for d in range(2)]).astype(BF16),
        "a0": rw_a0[j].reshape(1, RW_W), "aup": _pad_rows(rw_a_up[j], LANE).astype(BF16),
        "gup": _pad_rows(rw_g_up[j], LANE).astype(BF16),
        "k_k": rw_k_k[j].reshape(1, RW_W), "k_a": rw_k_a[j].reshape(1, RW_W), "r_k": rw_r_k[j].reshape(1, RW_W),
        "ln_g": rw_ln_g[j].reshape(1, RW_W), "ln_b": rw_ln_b[j].reshape(1, RW_W),
        "ones64": _block_ones(RW_W, RW_DH), "ones128": _block_ones(GLA_V, GLA_DV),
    }


def _gqa_head_order():
    per_kv = C_HEADS // C_KV
    order = []
    for jj in range(per_kv):
        for kv in range(C_KV):
            h = kv * per_kv + jj
            order.extend(range(h * HEAD_DIM, (h + 1) * HEAD_DIM))
    return np.asarray(order)


def _rope_tables(n_ctx, T, width):
    t = jnp.arange(T)
    pos = jnp.stack([t // GRID_W, t % GRID_W], axis=-1).astype(F32)
    half = HEAD_DIM // 2
    inv = ROPE_THETA ** (-jnp.arange(0, half, 2, dtype=F32) / half)
    ang = pos[:, :, None] * inv
    cos, sin = jnp.cos(ang), jnp.sin(ang)
    cos_h = jnp.concatenate([cos, cos], axis=-1).reshape(T, HEAD_DIM)
    sin_h = jnp.concatenate([-sin, sin], axis=-1).reshape(T, HEAD_DIM)
    cos_h = jnp.concatenate([jnp.ones((n_ctx, HEAD_DIM), F32), cos_h], axis=0)
    sin_h = jnp.concatenate([jnp.zeros((n_ctx, HEAD_DIM), F32), sin_h], axis=0)
    reps = width // HEAD_DIM
    return jnp.tile(cos_h, (1, reps)), jnp.tile(sin_h, (1, reps))


def _even_layer(xs, h, mod, n_ctx, p):
    z_gla, r, lwf, lwb, k, v, kk, a, g_rw, bonus = _even_in(h, p, n_ctx)
    og_f, og_b, or_f, or_b = _even_scans(z_gla, p, r, lwf, lwb, k, v, kk, a, n_ctx)
    return _even_out(xs, og_f, og_b, z_gla, or_f, or_b, bonus, g_rw, p, mod, n_ctx,
                     tm=_row_tile(xs.shape[1], ROW_CAP_FFN))


def _odd_layer(xs, h, mod, n_ctx, p, need_ctx):
    B, S, D = xs.shape
    T = S - n_ctx
    qc, kc, vc, qd, kd, vd = _odd_in(h, p["w_in"], p["cos"], p["sin"], p["qg"], p["kg"], p["ones64"],
                                     tm=_row_tile(S, ROW_CAP_FFN))
    y_gqa = _pair_attention(qc, kc, vc, q_row0=n_ctx, n_q=T, n_k=S, tq=256, shared_kv=True, name="gqa")
    y_na = _na_attention(qd, kd, vd, p["bias_tab"], n_ctx)
    parts_g, parts_n = [y_gqa], [y_na]
    if need_ctx:
        parts_g.insert(0, _pair_attention(qc, kc, vc, q_row0=0, n_q=n_ctx, n_k=n_ctx, tq=n_ctx, shared_kv=True,
                                          name="gqa_ctx"))
        parts_n.insert(0, _pair_attention(qd, kd, vd, q_row0=0, n_q=n_ctx, n_k=n_ctx, tq=n_ctx, shared_kv=False,
                                          name="na_ctx"))
    else:
        zeros = jnp.zeros((B, n_ctx, y_gqa.shape[2]), BF16)
        parts_g.insert(0, zeros)
        parts_n.insert(0, zeros)
    y = jnp.concatenate([jnp.concatenate(parts_g, axis=1), jnp.concatenate(parts_n, axis=1)], axis=2)
    return _proj_res(xs, y, p["w_out"], mod, n_ctx, tm=_row_tile(S, ROW_CAP_MATMUL))


def kernel(x, c, ctx, c_ctx, w_mod, b_mod, norm1_g, norm2_g, ffn_w13, ffn_w2, ev_w_in, ev_w_out, gla_a_up,
           gla_a_bias, gla_norm_g, rw_mu, rw_w0, rw_w_up, rw_a0, rw_a_up, rw_g_up, rw_k_k, rw_k_a, rw_r_k, rw_ln_g,
           rw_ln_b, od_w_in, od_w_out, cq_norm_g, ck_norm_g, na_rpb, final_g):
    B, T, D = x.shape
    n_ctx = ctx.shape[1]
    S = n_ctx + T
    xs = jnp.concatenate([ctx, x], axis=1)
    mods = _mod_vectors(c, c_ctx, w_mod, b_mod)
    nq, nk = C_HEADS * HEAD_DIM, C_KV * HEAD_DIM
    order = _gqa_head_order()
    cos_t, sin_t = _rope_tables(n_ctx, T, nq)
    ones64 = _block_ones(nq, HEAD_DIM)
    h = _norm_mod(xs, norm1_g[0], mods[0], n_ctx, tm=_row_tile(S, ROW_CAP_MATMUL))
    for i in range(DEPTH):
        j = i // 2
        mod = mods[i]
        if i % 2 == 0:
            p = _even_params(j, ev_w_in, ev_w_out, gla_a_up, gla_a_bias, gla_norm_g, rw_mu, rw_w0, rw_w_up, rw_a0,
                             rw_a_up, rw_g_up, rw_k_k, rw_k_a, rw_r_k, rw_ln_g, rw_ln_b)
            xs = _even_layer(xs, h, mod, n_ctx, p)
        else:
            w_in = od_w_in[j]
            w_out = od_w_out[j]
            p = {
                "w_in": jnp.concatenate([w_in[:, :nq][:, order], w_in[:, nq:]], axis=1).astype(BF16),
                "w_out": jnp.concatenate([w_out[:nq][order], w_out[nq:]], axis=0).astype(BF16),
                "cos": cos_t, "sin": sin_t, "ones64": ones64,
                "qg": jnp.tile(cq_norm_g[j], C_HEADS).reshape(1, nq),
                "kg": jnp.tile(ck_norm_g[j], C_HEADS).reshape(1, nq),
                "bias_tab": _na_bias_table(na_rpb[j], T // GRID_W),
            }
            xs = _odd_layer(xs, h, mod, n_ctx, p, need_ctx=i < DEPTH - 1)
        next_norm = (norm1_g[i + 1], mods[i + 1]) if i + 1 < DEPTH else None
        xs, h = _ffn(xs, norm2_g[i], mod, ffn_w13[i].astype(BF16), ffn_w2[i].astype(BF16), n_ctx,
                     tm=_row_tile(S, ROW_CAP_FFN), tf=FFN_COL_TILE, next_norm=next_norm)
    return _final_norm(xs, final_g, n_ctx, tm=_row_tile(n_ctx, ROW_CAP_ELEMENTWISE))
```
